```python
import math
import jax, jax.numpy as jnp
from jax import lax
import numpy as np

D_MODEL = 2048
BATCH = 2
SEQ = 8192
DEPTH = 1

HEAD_DIM = 64
ATTN_WIDTH = D_MODEL // 2
RWKV_WIDTH = D_MODEL - ATTN_WIDTH
N_ATTN_HEADS = ATTN_WIDTH // HEAD_DIM
N_RWKV_HEADS = RWKV_WIDTH // HEAD_DIM
DILATED_PATTERNS = ((128, 1), (512, 4), (2048, 16))
REL_BUCKETS = 32
REL_MAX_DIST = 2048
RWKV_DECAY_RANK = max(32, int(round(1.8 * RWKV_WIDTH ** 0.5 / 32)) * 32)
RWKV_A_RANK = max(32, int(round(1.8 * RWKV_WIDTH ** 0.5 / 32)) * 32)
RWKV_GATE_RANK = max(32, int(round(0.6 * RWKV_WIDTH ** 0.8 / 32)) * 32)
RWKV_COLS = 3 * RWKV_WIDTH + RWKV_DECAY_RANK + RWKV_A_RANK + RWKV_GATE_RANK
RWKV_SPLITS = (RWKV_WIDTH, 2 * RWKV_WIDTH, 3 * RWKV_WIDTH,
               3 * RWKV_WIDTH + RWKV_DECAY_RANK,
               3 * RWKV_WIDTH + RWKV_DECAY_RANK + RWKV_A_RANK)
IN_COLS = 3 * ATTN_WIDTH + RWKV_COLS
FFN_HIDDEN = -(-(8 * D_MODEL) // (3 * 256)) * 256
RMS_EPS = 1e-6
GN_EPS = 64e-5
DECAY_SCALE = math.exp(-0.5)
ATTN_SCALE = HEAD_DIM ** -0.5

kernel_name = 'hymba_rwkv7_dilated_attn_block'


def rms_norm(x, g):
    xf = x.astype(jnp.float32)
    y = xf * lax.rsqrt(jnp.mean(xf * xf, axis=-1, keepdims=True) + RMS_EPS)
    return (y * g.astype(jnp.float32)).astype(x.dtype)


def t5_bucket(dist):
    exact = REL_BUCKETS // 2
    d_f = jnp.maximum(dist, 1).astype(jnp.float32)
    large = exact + (jnp.log(d_f / exact) / math.log(REL_MAX_DIST / exact)
                     * (REL_BUCKETS - exact)).astype(jnp.int32)
    large = jnp.minimum(large, REL_BUCKETS - 1)
    return jnp.where(dist < exact, dist, large)


def dilated_branch(q, k, v, bias_table, window, dilation):
    b, s, h, dh = q.shape
    blk = window // dilation
    span = blk * dilation
    sp = -(-s // span) * span
    nblk = sp // span

    def blocks(t):
        t = jnp.pad(t, ((0, 0), (0, sp - s), (0, 0), (0, 0)))
        return t.reshape(b, nblk, blk, dilation, h, dh)

    def with_prev(t):
        prev = jnp.pad(t, ((0, 0), (1, 0), (0, 0), (0, 0), (0, 0), (0, 0)))[:, :-1]
        return jnp.concatenate([prev, t], axis=2)

    qb = blocks(q)
    kc = with_prev(blocks(k))
    vc = with_prev(blocks(v))
    qi = jnp.arange(blk)[:, None]
    ki = jnp.arange(2 * blk)[None, :]
    rel = qi + blk - ki
    band = (rel >= 0) & (rel <= blk)
    bias = bias_table[t5_bucket(jnp.clip(rel, 0, blk) * dilation)]
    bias = jnp.transpose(bias, (2, 0, 1)).astype(jnp.float32)
    not_first = jnp.arange(nblk)[:, None, None] > 0
    valid = band[None] & (not_first | (ki >= blk)[None])
    logits = jnp.einsum('bnqrhd,bnkrhd->bnrhqk', qb, kc) * ATTN_SCALE + bias
    logits = jnp.where(valid[None, :, None, None], logits, -jnp.inf)
    m = jnp.max(logits, axis=-1, keepdims=True)
    p = jnp.exp(logits - m)
    l = jnp.sum(p, axis=-1, keepdims=True)
    o = jnp.einsum('bnrhqk,bnkrhd->bnqrhd', p / l, vc)
    lse = (m + jnp.log(l))[..., 0]
    o = o.reshape(b, sp, h, dh)[:, :s]
    lse = jnp.transpose(lse, (0, 1, 4, 2, 3)).reshape(b, sp, h)[:, :s]
    return o, lse


def dilated_attention(q, k, v, bias_table):
    outs, lses = [], []
    for window, dilation in DILATED_PATTERNS:
        o, lse = dilated_branch(q, k, v, bias_table, window, dilation)
        outs.append(o)
        lses.append(lse)
    wts = jax.nn.softmax(jnp.stack(lses), axis=0)
    return jnp.sum(wts[..., None] * jnp.stack(outs), axis=0)


def rwkv7_step(state, inp):
    r, w, k, v, a, bb = inp
    sa = jnp.einsum('bhij,bhj->bhi', state, a)
    state = (state * w[:, :, None, :] + sa[..., None] * bb[:, :, None, :]
             + v[..., None] * k[:, :, None, :])
    y = jnp.einsum('bhij,bhj->bhi', state, r)
    return state, y


def rwkv7_time_mix(z, shift_mix, w0, w_up, a0, a_up, g_up, k_k, k_a, r_k, ln_w, ln_b):
    b, s, _ = z.shape
    prev = jnp.pad(z, ((0, 0), (1, 0), (0, 0)))[:, :-1]
    z = z + (prev - z) * shift_mix
    r, k, v, w_lo, a_lo, g_lo = jnp.split(z, list(RWKV_SPLITS), axis=-1)
    decay = jnp.exp(-DECAY_SCALE * jax.nn.sigmoid(w0 + jnp.tanh(w_lo) @ w_up))
    a = jax.nn.sigmoid(a0 + a_lo @ a_up)
    g = jax.nn.sigmoid(g_lo) @ g_up

    def heads(t):
        return t.reshape(b, s, N_RWKV_HEADS, HEAD_DIM)

    kk = heads(k * k_k)
    kk = kk / jnp.maximum(jnp.sqrt(jnp.sum(kk * kk, axis=-1, keepdims=True)), 1e-12)
    k = k * (1.0 + (a - 1.0) * k_a)
    rh, kh, vh, wh, ah = heads(r), heads(k), heads(v), heads(decay), heads(a)
    xs = tuple(jnp.moveaxis(t, 1, 0) for t in (rh, wh, kh, vh, -kk, kk * ah))
    state0 = jnp.zeros((b, N_RWKV_HEADS, HEAD_DIM, HEAD_DIM), jnp.float32)
    _, y = lax.scan(rwkv7_step, state0, xs)
    y = jnp.moveaxis(y, 0, 1)
    mu = jnp.mean(y, axis=-1, keepdims=True)
    var = jnp.mean(jnp.square(y - mu), axis=-1, keepdims=True)
    y = ((y - mu) * lax.rsqrt(var + GN_EPS)).reshape(b, s, RWKV_WIDTH) * ln_w + ln_b
    bonus = jnp.sum(rh * kh * r_k, axis=-1, keepdims=True) * vh
    return (y + bonus.reshape(b, s, RWKV_WIDTH)) * g


def setup_inputs(seed: int = 0) -> dict:
    key = jax.random.key(seed)
    ks = jax.random.split(key, 20)
    L = DEPTH

    def nrm(k, shape, scale):
        return scale * jax.random.normal(k, shape, jnp.float32)

    return {
        'x': nrm(ks[0], (BATCH, SEQ, D_MODEL), 1.0),
        'norm1_g': 1.0 + nrm(ks[1], (L, D_MODEL), 0.05),
        'w_in': nrm(ks[2], (L, D_MODEL, IN_COLS), D_MODEL ** -0.5),
        'rwkv_shift_mix': jax.random.uniform(ks[3], (L, RWKV_COLS), jnp.float32),
        'rwkv_w0': jnp.linspace(-4.0, 2.0, RWKV_WIDTH, dtype=jnp.float32) + nrm(ks[4], (L, RWKV_WIDTH), 0.1),
        'rwkv_w_up': nrm(ks[5], (L, RWKV_DECAY_RANK, RWKV_WIDTH), 0.5 * RWKV_DECAY_RANK ** -0.5),
        'rwkv_a0': nrm(ks[6], (L, RWKV_WIDTH), 0.1),
        'rwkv_a_up': nrm(ks[7], (L, RWKV_A_RANK, RWKV_WIDTH), RWKV_A_RANK ** -0.5),
        'rwkv_g_up': nrm(ks[8], (L, RWKV_GATE_RANK, RWKV_WIDTH), RWKV_GATE_RANK ** -0.5),
        'rwkv_k_k': 0.85 + nrm(ks[9], (L, RWKV_WIDTH), 0.05),
        'rwkv_k_a': 1.0 + nrm(ks[10], (L, RWKV_WIDTH), 0.05),
        'rwkv_r_k': nrm(ks[11], (L, N_RWKV_HEADS, HEAD_DIM), 0.1),
        'rwkv_ln_w': 1.0 + nrm(ks[12], (L, RWKV_WIDTH), 0.05),
        'rwkv_ln_b': nrm(ks[13], (L, RWKV_WIDTH), 0.02),
        'rel_bias_table': nrm(ks[14], (REL_BUCKETS, N_ATTN_HEADS), 0.5),
        'w_out': nrm(ks[15], (L, D_MODEL, D_MODEL), D_MODEL ** -0.5),
        'norm2_g': 1.0 + nrm(ks[16], (L, D_MODEL), 0.05),
        'w_gate_up': nrm(ks[17], (L, D_MODEL, 2 * FFN_HIDDEN), D_MODEL ** -0.5),
        'w_down': nrm(ks[18], (L, FFN_HIDDEN, D_MODEL), FFN_HIDDEN ** -0.5),
        'final_g': 1.0 + nrm(ks[19], (D_MODEL,), 0.05),
    }


def reference(x, norm1_g, w_in, rwkv_shift_mix, rwkv_w0, rwkv_w_up, rwkv_a0, rwkv_a_up,
              rwkv_g_up, rwkv_k_k, rwkv_k_a, rwkv_r_k, rwkv_ln_w, rwkv_ln_b, rel_bias_table,
              w_out, norm2_g, w_gate_up, w_down, final_g):
    b, s, _ = x.shape
    h = x
    for l in range(DEPTH):
        u = rms_norm(h, norm1_g[l])
        proj = (u @ w_in[l]).astype(jnp.float32)
        q, k, v, z = jnp.split(proj, [ATTN_WIDTH, 2 * ATTN_WIDTH, 3 * ATTN_WIDTH], axis=-1)
        hd = (b, s, N_ATTN_HEADS, HEAD_DIM)
        attn = dilated_attention(q.reshape(hd), k.reshape(hd), v.reshape(hd),
                                 rel_bias_table.astype(jnp.float32))
        rw = rwkv7_time_mix(z, rwkv_shift_mix[l], rwkv_w0[l], rwkv_w_up[l], rwkv_a0[l],
                            rwkv_a_up[l], rwkv_g_up[l], rwkv_k_k[l], rwkv_k_a[l],
                            rwkv_r_k[l], rwkv_ln_w[l], rwkv_ln_b[l])
        mixed = jnp.concatenate([attn.reshape(b, s, ATTN_WIDTH), rw], axis=-1).astype(h.dtype)
        h = h + mixed @ w_out[l]
        u = rms_norm(h, norm2_g[l])
        gate, up = jnp.split(u @ w_gate_up[l], 2, axis=-1)
        h = h + (jax.nn.silu(gate) * up) @ w_down[l]
    return rms_norm(h, final_g)
```

```python
import functools
import math

import jax
import jax.numpy as jnp
from jax import lax
from jax.experimental import pallas as pl
from jax.experimental.pallas import tpu as pltpu

F32 = jnp.float32
BF16 = jnp.bfloat16

HEAD_DIM = 64
LANES = 128
DILATED_PATTERNS = ((128, 1), (512, 4), (2048, 16))
REL_BUCKETS = 32
REL_MAX_DIST = 2048
RMS_EPS = 1e-6
GN_EPS = 64e-5
DECAY_SCALE = math.exp(-0.5)
ATTN_SCALE = HEAD_DIM ** -0.5
MASK_VALUE = -1e30
CHUNK = 64
VMEM_LIMIT = 56 * 1024 * 1024


def _mm(a, b):
    return jnp.dot(a.astype(BF16), b.astype(BF16), preferred_element_type=F32)


def _mm_nt(a, b):
    return lax.dot_general(a.astype(BF16), b.astype(BF16), (((1,), (1,)), ((), ())),
                           preferred_element_type=F32)


def _mm_tn(a, b):
    return lax.dot_general(a.astype(BF16), b.astype(BF16), (((0,), (0,)), ((), ())),
                           preferred_element_type=F32)


def _mm_split(a, b_exact):
    hi = a.astype(BF16)
    lo = (a - hi.astype(F32)).astype(BF16)
    return (jnp.dot(hi, b_exact, preferred_element_type=F32)
            + jnp.dot(lo, b_exact, preferred_element_type=F32))


def _params(semantics):
    return pltpu.CompilerParams(dimension_semantics=semantics, vmem_limit_bytes=VMEM_LIMIT)


def _in_proj_kernel(x_ref, g_ref, w_ref, wl_ref, qkv_ref, rkv_ref, lora_ref, xn_ref, *, n_qkv):
    j = pl.program_id(1)

    @pl.when(j == 0)
    def _():
        x = x_ref[...]
        ms = jnp.mean(x * x, axis=-1, keepdims=True)
        xn_ref[...] = (x * lax.rsqrt(ms + RMS_EPS) * g_ref[...]).astype(BF16)

    @pl.when(j < n_qkv)
    def _():
        qkv_ref[...] = jnp.dot(xn_ref[...], w_ref[...], preferred_element_type=F32)

    @pl.when((j >= n_qkv) & (j < 2 * n_qkv))
    def _():
        rkv_ref[...] = jnp.dot(xn_ref[...], w_ref[...], preferred_element_type=F32)

    @pl.when(j == 2 * n_qkv)
    def _():
        lora_ref[...] = jnp.dot(xn_ref[...], wl_ref[...], preferred_element_type=F32)


def _in_proj(x2, g, w_main, w_lora, *, tm, tn):
    t, d = x2.shape
    n_main = w_main.shape[1]
    n_lora = w_lora.shape[1]
    n_qkv = (n_main // 2) // tn
    last = 2 * n_qkv - 1
    return pl.pallas_call(
        functools.partial(_in_proj_kernel, n_qkv=n_qkv),
        grid=(t // tm, 2 * n_qkv + 1),
        in_specs=[
            pl.BlockSpec((tm, d), lambda i, j: (i, 0)),
            pl.BlockSpec((1, d), lambda i, j: (0, 0)),
            pl.BlockSpec((d, tn), lambda i, j: (0, jnp.minimum(j, last))),
            pl.BlockSpec((d, n_lora), lambda i, j: (0, 0)),
        ],
        out_specs=[
            pl.BlockSpec((tm, tn), lambda i, j: (i, jnp.minimum(j, n_qkv - 1))),
            pl.BlockSpec((tm, tn), lambda i, j: (i, jnp.clip(j - n_qkv, 0, n_qkv - 1))),
            pl.BlockSpec((tm, n_lora), lambda i, j: (i, 0)),
        ],
        out_shape=[
            jax.ShapeDtypeStruct((t, n_main // 2), F32),
            jax.ShapeDtypeStruct((t, n_main // 2), F32),
            jax.ShapeDtypeStruct((t, n_lora), F32),
        ],
        scratch_shapes=[pltpu.VMEM((tm, d), BF16)],
        compiler_params=_params(("arbitrary", "arbitrary")),
        name="in_proj",
    )(x2, g, w_main, w_lora)


def _attn_kernel(q_ref, kp_ref, kc_ref, vp_ref, vc_ref, bias_ref, o_ref, lse_ref, *, blk, n_heads):
    n = pl.program_id(1)
    lane = lax.broadcasted_iota(jnp.int32, (1, LANES), 1)
    key_idx = lax.broadcasted_iota(jnp.int32, (1, 2 * blk), 1)
    key_ok = (key_idx >= blk) | (n > 0)
    q = q_ref[...] * ATTN_SCALE
    for hp in range(n_heads // 2):
        cols = slice(hp * LANES, (hp + 1) * LANES)
        q2 = q[:, cols]
        k2 = jnp.concatenate([kp_ref[:, cols], kc_ref[:, cols]], axis=0)
        v2 = jnp.concatenate([vp_ref[:, cols], vc_ref[:, cols]], axis=0).astype(BF16)
        outs = []
        for half in range(2):
            in_head = (lane >= half * HEAD_DIM) & (lane < (half + 1) * HEAD_DIM)
            s = _mm_nt(jnp.where(in_head, q2, 0.0), k2) + bias_ref[2 * hp + half]
            s = jnp.where(key_ok, s, MASK_VALUE)
            m = jnp.max(s, axis=-1, keepdims=True)
            p = jnp.exp(s - m)
            l = jnp.sum(p, axis=-1, keepdims=True)
            o = jnp.dot(p.astype(BF16), v2, preferred_element_type=F32) / l
            outs.append((in_head, o, m + jnp.log(l)))
        (ma, oa, la), (_, ob, lb) = outs
        o_ref[:, cols] = jnp.where(ma, oa, ob)
        lse_ref[:, cols] = jnp.where(ma, la, lb)


def _attn_branch(qkv, bias, *, batch, seq, window, dilation, n_heads):
    blk = window // dilation
    span = blk * dilation
    nblk = seq // span
    width = n_heads * HEAD_DIM
    qkv4 = qkv.reshape(batch, nblk, blk, dilation * 3 * width)
    blk_spec = lambda f: pl.BlockSpec((None, None, blk, width), f)
    out_shape = jax.ShapeDtypeStruct((batch, nblk, blk, dilation * width), F32)
    o, lse = pl.pallas_call(
        functools.partial(_attn_kernel, blk=blk, n_heads=n_heads),
        grid=(batch, nblk, dilation),
        in_specs=[
            blk_spec(lambda b, n, r: (b, n, 0, 3 * r)),
            blk_spec(lambda b, n, r: (b, jnp.maximum(n - 1, 0), 0, 3 * r + 1)),
            blk_spec(lambda b, n, r: (b, n, 0, 3 * r + 1)),
            blk_spec(lambda b, n, r: (b, jnp.maximum(n - 1, 0), 0, 3 * r + 2)),
            blk_spec(lambda b, n, r: (b, n, 0, 3 * r + 2)),
            pl.BlockSpec((n_heads, blk, 2 * blk), lambda b, n, r: (0, 0, 0)),
        ],
        out_specs=[blk_spec(lambda b, n, r: (b, n, 0, r)), blk_spec(lambda b, n, r: (b, n, 0, r))],
        out_shape=[out_shape, out_shape],
        compiler_params=_params(("arbitrary", "arbitrary", "arbitrary")),
        name=f"attn_w{window}_d{dilation}",
    )(qkv4, qkv4, qkv4, qkv4, qkv4, bias)
    return o.reshape(batch * seq, width), lse.reshape(batch * seq, width)


def _attn_mix_kernel(o1, l1, o2, l2, o3, l3, out_ref):
    a, b, c = l1[...], l2[...], l3[...]
    m = jnp.maximum(jnp.maximum(a, b), c)
    ea, eb, ec = jnp.exp(a - m), jnp.exp(b - m), jnp.exp(c - m)
    out_ref[...] = (ea * o1[...] + eb * o2[...] + ec * o3[...]) / (ea + eb + ec)


def _attn_mix(pairs, *, tm):
    t, width = pairs[0][0].shape
    spec = pl.BlockSpec((tm, width), lambda i: (i, 0))
    flat = [a for pair in pairs for a in pair]
    return pl.pallas_call(
        _attn_mix_kernel,
        grid=(t // tm,),
        in_specs=[spec] * 6,
        out_specs=spec,
        out_shape=jax.ShapeDtypeStruct((t, width), F32),
        compiler_params=_params(("arbitrary",)),
        name="attn_mix",
    )(*flat)


def _t5_bucket(dist):
    exact = REL_BUCKETS // 2
    d_f = jnp.maximum(dist, 1).astype(F32)
    large = exact + (jnp.log(d_f / exact) / math.log(REL_MAX_DIST / exact)
                     * (REL_BUCKETS - exact)).astype(jnp.int32)
    large = jnp.minimum(large, REL_BUCKETS - 1)
    return jnp.where(dist < exact, dist, large)


def _band_bias(bias_table, blk, dilation):
    qi = jnp.arange(blk)[:, None]
    ki = jnp.arange(2 * blk)[None, :]
    rel = qi + blk - ki
    band = (rel >= 0) & (rel <= blk)
    bias = bias_table[_t5_bucket(jnp.clip(rel, 0, blk) * dilation)]
    bias = jnp.where(band[..., None], bias.astype(F32), MASK_VALUE)
    return jnp.transpose(bias, (2, 0, 1))


def _rwkv_kernel(rkv_ref, lora_ref, mix_rkv_ref, mix_lora_ref, w0_ref, w_up_ref, a0_ref, a_up_ref,
                 g_up_ref, kk_ref, ka_ref, rk_ref, lnw_ref, lnb_ref, seg_ref, tri_ref,
                 out_ref,
                 state_ref, carry_rkv_ref, carry_lora_ref,
                 rt_ref, at_ref, bh_ref, kh_ref, bc_ref, kc_ref, v_ref, pc_ref, y_ref, *, width):
    c = pl.program_id(1)
    n_pairs = width // LANES
    ch = rkv_ref.shape[0]

    @pl.when(c == 0)
    def _():
        state_ref[...] = jnp.zeros_like(state_ref)
        carry_rkv_ref[...] = jnp.zeros_like(carry_rkv_ref)
        carry_lora_ref[...] = jnp.zeros_like(carry_lora_ref)

    row = lax.broadcasted_iota(jnp.int32, (ch, 1), 0)

    def token_shift(z, carry_ref, mix):
        prev = jnp.where(row == 0, carry_ref[...], pltpu.roll(z, 1, axis=0))
        carry_ref[...] = z[ch - 1:ch, :]
        return z + (prev - z) * mix

    z = token_shift(rkv_ref[...], carry_rkv_ref, mix_rkv_ref[...])
    zl = token_shift(lora_ref[...], carry_lora_ref, mix_lora_ref[...])
    r, k, v = z[:, :width], z[:, width:2 * width], z[:, 2 * width:]

    lw = -DECAY_SCALE * jax.nn.sigmoid(w0_ref[...] + _mm(jnp.tanh(zl), w_up_ref[...]))
    a_sig = jax.nn.sigmoid(a0_ref[...] + _mm(zl, a_up_ref[...]))
    gate = _mm(jax.nn.sigmoid(zl), g_up_ref[...])

    seg = seg_ref[...]
    kk = k * kk_ref[...]
    kk = kk / jnp.maximum(jnp.sqrt(_mm_split(kk * kk, seg)), 1e-12)
    k = k * (1.0 + (a_sig - 1.0) * ka_ref[...])
    bonus = _mm_split(r * k * rk_ref[...], seg) * v
    a_in = -kk
    b_in = kk * a_sig

    lw_hi = lw.astype(BF16)
    lw_lo = (lw - lw_hi.astype(F32)).astype(BF16)
    tri = tri_ref[...]
    cum = (jnp.dot(tri, lw_hi, preferred_element_type=F32)
           + jnp.dot(tri, lw_lo, preferred_element_type=F32))
    cum_end = cum[ch - 1:ch, :]
    e_neg = jnp.exp(-cum)
    e_end = jnp.exp(cum_end - cum)

    def put(ref, val):
        for p in range(n_pairs):
            ref[p] = val[:, p * LANES:(p + 1) * LANES]

    put(rt_ref, r * jnp.exp(cum))
    put(at_ref, a_in * jnp.exp(cum - lw))
    put(bh_ref, b_in * e_neg)
    put(kh_ref, k * e_neg)
    put(bc_ref, b_in * e_end)
    put(kc_ref, k * e_end)
    put(v_ref, v)
    put(pc_ref, jnp.exp(cum_end))

    ti = lax.broadcasted_iota(jnp.int32, (ch, ch), 0)
    si = lax.broadcasted_iota(jnp.int32, (ch, ch), 1)
    incl = ti >= si
    strict = ti > si
    lane = lax.broadcasted_iota(jnp.int32, (1, LANES), 1)
    di = lax.broadcasted_iota(jnp.int32, (LANES, LANES), 0)
    dj = lax.broadcasted_iota(jnp.int32, (LANES, LANES), 1)
    same_head = (di < HEAD_DIM) == (dj < HEAD_DIM)
    diag = di == dj

    def pair_body(p, carry):
        rt, at, bh, kh = rt_ref[p], at_ref[p], bh_ref[p], kh_ref[p]
        bc, kc, vv, pc = bc_ref[p], kc_ref[p], v_ref[p], pc_ref[p]
        w_acc = jnp.zeros((ch, LANES), F32)
        u_acc = jnp.zeros((ch, LANES), F32)
        q_acc = rt
        y_acc = jnp.zeros((ch, LANES), F32)
        for half in range(2):
            in_head = (lane >= half * HEAD_DIM) & (lane < (half + 1) * HEAD_DIM)
            lhs = jnp.concatenate([jnp.where(in_head, rt, 0.0), jnp.where(in_head, at, 0.0)], axis=0)
            a_b = _mm_nt(lhs, bh)
            a_k = _mm_nt(lhs, kh)
            a_rb = jnp.where(incl, a_b[:ch], 0.0)
            a_ab = jnp.where(strict, a_b[ch:], 0.0)
            a_rk = jnp.where(incl, a_k[:ch], 0.0)
            a_ak = jnp.where(strict, a_k[ch:], 0.0)
            vh = jnp.where(in_head, vv, 0.0)
            x = jnp.concatenate([jnp.where(in_head, at, 0.0), _mm(a_ak, vh)], axis=1)
            nk = a_ab
            x = x + _mm(nk, x)
            steps = int(math.log2(ch)) - 1
            for _ in range(steps):
                nk = _mm(nk, nk)
                x = x + _mm(nk, x)
            ax = _mm(a_rb, x)
            w_acc = w_acc + x[:, :LANES]
            u_acc = u_acc + x[:, LANES:]
            q_acc = q_acc + ax[:, :LANES]
            y_acc = y_acc + ax[:, LANES:] + _mm(a_rk, vh)
        bt = _mm_tn(bc, jnp.concatenate([w_acc, u_acc], axis=1))
        kt = _mm_tn(kc, vv)
        m_mat = jnp.where(same_head, bt[:, :LANES], 0.0) + jnp.where(diag, pc, 0.0)
        g_mat = jnp.where(same_head, bt[:, LANES:] + kt, 0.0)
        h = state_ref[p]
        y_ref[p] = _mm(q_acc, h) + y_acc
        state_ref[p] = _mm(m_mat, h) + g_mat
        return carry

    lax.fori_loop(0, n_pairs, pair_body, 0)

    y = jnp.concatenate([y_ref[p] for p in range(n_pairs)], axis=1)
    inv_n = 1.0 / HEAD_DIM
    mu = _mm_split(y, seg) * inv_n
    d = y - mu
    var = _mm_split(d * d, seg) * inv_n
    yn = d * lax.rsqrt(var + GN_EPS) * lnw_ref[...] + lnb_ref[...]
    out_ref[...] = (yn + bonus) * gate


def _rwkv(rkv, lora, prm, *, batch, seq):
    t, w3 = rkv.shape
    width = w3 // 3
    n_lora = lora.shape[1]
    ch = CHUNK
    n_chunks = seq // ch
    n_pairs = width // LANES
    row = lambda n: pl.BlockSpec((1, n), lambda b, c: (0, 0))
    full = lambda a: pl.BlockSpec(a.shape, lambda b, c: (0,) * a.ndim)
    head_id = jnp.arange(width) // HEAD_DIM
    seg = (head_id[:, None] == head_id[None, :]).astype(BF16)
    tri = (jnp.arange(ch)[:, None] >= jnp.arange(ch)[None, :]).astype(BF16)
    big = lambda: pltpu.VMEM((n_pairs, ch, LANES), F32)
    args = (rkv, lora, prm["mix_rkv"], prm["mix_lora"], prm["w0"], prm["w_up"], prm["a0"], prm["a_up"],
            prm["g_up"], prm["k_k"], prm["k_a"], prm["r_k"], prm["ln_w"], prm["ln_b"], seg, tri)
    in_specs = [
        pl.BlockSpec((ch, w3), lambda b, c: (b * n_chunks + c, 0)),
        pl.BlockSpec((ch, n_lora), lambda b, c: (b * n_chunks + c, 0)),
        row(w3), row(n_lora), row(width), full(prm["w_up"]), row(width), full(prm["a_up"]),
        full(prm["g_up"]), row(width), row(width), row(width), row(width), row(width),
        full(seg), full(tri),
    ]
    return pl.pallas_call(
        functools.partial(_rwkv_kernel, width=width),
        grid=(batch, n_chunks),
        in_specs=in_specs,
        out_specs=pl.BlockSpec((ch, width), lambda b, c: (b * n_chunks + c, 0)),
        out_shape=jax.ShapeDtypeStruct((t, width), F32),
        scratch_shapes=[
            pltpu.VMEM((n_pairs, LANES, LANES), F32),
            pltpu.VMEM((1, w3), F32),
            pltpu.VMEM((1, n_lora), F32),
            big(), big(), big(), big(), big(), big(), big(),
            pltpu.VMEM((n_pairs, 1, LANES), F32),
            big(),
        ],
        compiler_params=_params(("arbitrary", "arbitrary")),
        name="rwkv7",
    )(*args)


def _out_proj_kernel(attn_ref, rw_ref, w_ref, x_ref, out_ref, lhs_ref, *, width):
    @pl.when(pl.program_id(1) == 0)
    def _():
        lhs_ref[:, :width] = attn_ref[...].astype(BF16)
        lhs_ref[:, width:] = rw_ref[...].astype(BF16)

    out_ref[...] = x_ref[...] + jnp.dot(lhs_ref[...], w_ref[...], preferred_element_type=F32)


def _out_proj(attn, rw, w_out, x2, *, tm, tn):
    t, width = attn.shape
    d = w_out.shape[1]
    return pl.pallas_call(
        functools.partial(_out_proj_kernel, width=width),
        grid=(t // tm, d // tn),
        in_specs=[
            pl.BlockSpec((tm, width), lambda i, j: (i, 0)),
            pl.BlockSpec((tm, width), lambda i, j: (i, 0)),
            pl.BlockSpec((2 * width, tn), lambda i, j: (0, j)),
            pl.BlockSpec((tm, tn), lambda i, j: (i, j)),
        ],
        out_specs=pl.BlockSpec((tm, tn), lambda i, j: (i, j)),
        out_shape=jax.ShapeDtypeStruct((t, d), F32),
        scratch_shapes=[pltpu.VMEM((tm, 2 * width), BF16)],
        compiler_params=_params(("arbitrary", "arbitrary")),
        name="out_proj",
    )(attn, rw, w_out, x2)


def _ffn_up_kernel(h_ref, g_ref, wg_ref, wu_ref, act_ref, hn_ref):
    @pl.when(pl.program_id(1) == 0)
    def _():
        h = h_ref[...]
        ms = jnp.mean(h * h, axis=-1, keepdims=True)
        hn_ref[...] = (h * lax.rsqrt(ms + RMS_EPS) * g_ref[...]).astype(BF16)

    hn = hn_ref[...]
    gate = jnp.dot(hn, wg_ref[...], preferred_element_type=F32)
    up = jnp.dot(hn, wu_ref[...], preferred_element_type=F32)
    act_ref[...] = (gate * jax.nn.sigmoid(gate) * up).astype(BF16)


def _ffn_up(h, g, w_gate_up, *, tm, th):
    t, d = h.shape
    hidden = w_gate_up.shape[1] // 2
    n_h = hidden // th
    return pl.pallas_call(
        _ffn_up_kernel,
        grid=(t // tm, n_h),
        in_specs=[
            pl.BlockSpec((tm, d), lambda i, j: (i, 0)),
            pl.BlockSpec((1, d), lambda i, j: (0, 0)),
            pl.BlockSpec((d, th), lambda i, j: (0, j)),
            pl.BlockSpec((d, th), lambda i, j: (0, n_h + j)),
        ],
        out_specs=pl.BlockSpec((tm, th), lambda i, j: (i, j)),
        out_shape=jax.ShapeDtypeStruct((t, hidden), BF16),
        scratch_shapes=[pltpu.VMEM((tm, d), BF16)],
        compiler_params=_params(("arbitrary", "arbitrary")),
        name="ffn_up",
    )(h, g, w_gate_up, w_gate_up)


def _ffn_down_kernel(act_ref, w_ref, h_ref, g_ref, out_ref, res_ref, ssq_ref, *, d):
    j = pl.program_id(1)
    n_j, _, tn = res_ref.shape
    res = h_ref[...] + jnp.dot(act_ref[...], w_ref[...], preferred_element_type=F32)
    res_ref[j] = res
    part = jnp.sum(res * res, axis=-1, keepdims=True)

    @pl.when(j == 0)
    def _():
        ssq_ref[...] = part

    @pl.when(j > 0)
    def _():
        ssq_ref[...] += part

    @pl.when(j == n_j - 1)
    def _():
        scale = lax.rsqrt(ssq_ref[...] * (1.0 / d) + RMS_EPS)
        for jj in range(n_j):
            cols = slice(jj * tn, (jj + 1) * tn)
            out_ref[:, cols] = res_ref[jj] * scale * g_ref[:, cols]


def _ffn_down(act, w_down, h, g, *, tm, tn):
    t, hidden = act.shape
    d = w_down.shape[1]
    return pl.pallas_call(
        functools.partial(_ffn_down_kernel, d=d),
        grid=(t // tm, d // tn),
        in_specs=[
            pl.BlockSpec((tm, hidden), lambda i, j: (i, 0)),
            pl.BlockSpec((hidden, tn), lambda i, j: (0, j)),
            pl.BlockSpec((tm, tn), lambda i, j: (i, j)),
            pl.BlockSpec((1, d), lambda i, j: (0, 0)),
        ],
        out_specs=pl.BlockSpec((tm, d), lambda i, j: (i, 0)),
        out_shape=jax.ShapeDtypeStruct((t, d), F32),
        scratch_shapes=[pltpu.VMEM((d // tn, tm, tn), F32), pltpu.VMEM((tm, 1), F32)],
        compiler_params=_params(("arbitrary", "arbitrary")),
        name="ffn_down",
    )(act, w_down, h, g)


def _pad_rows(a, rows):
    return jnp.pad(a, ((0, rows - a.shape[0]), (0, 0)))


def _layer(h, norm1_g, w_in, shift_mix, w0, w_up, a0, a_up, g_up, k_k, k_a, r_k, ln_w, ln_b,
           rel_bias_table, w_out, norm2_g, w_gate_up, w_down, out_g, *, batch, seq, tile_m):
    d = h.shape[1]
    width = w_out.shape[0] // 2
    n_heads = width // HEAD_DIM
    n_main = 6 * width
    rank_w, rank_a, rank_g = w_up.shape[0], a_up.shape[0], g_up.shape[0]
    n_lora = rank_w + rank_a + rank_g
    n_lora_pad = -(-n_lora // LANES) * LANES

    w_main = w_in[:, :n_main].astype(BF16)
    w_lora = jnp.pad(w_in[:, n_main:], ((0, 0), (0, n_lora_pad - n_lora))).astype(BF16)
    qkv, rkv, lora = _in_proj(h, norm1_g.reshape(1, d), w_main, w_lora, tm=tile_m, tn=3 * width // 4)

    pairs = []
    for window, dilation in DILATED_PATTERNS:
        bias = _band_bias(rel_bias_table.astype(F32), window // dilation, dilation)
        pairs.append(_attn_branch(qkv, bias, batch=batch, seq=seq, window=window, dilation=dilation,
                                  n_heads=n_heads))
    attn = _attn_mix(pairs, tm=tile_m // 2)

    mix_rkv = shift_mix[:3 * width].reshape(1, 3 * width)
    mix_lora = jnp.pad(shift_mix[3 * width:], (0, n_lora_pad - n_lora)).reshape(1, n_lora_pad)
    zeros = lambda n: jnp.zeros((n, width), F32)
    prm = dict(
        mix_rkv=mix_rkv, mix_lora=mix_lora,
        w0=w0.reshape(1, width), a0=a0.reshape(1, width),
        w_up=_pad_rows(w_up, n_lora_pad).astype(BF16),
        a_up=_pad_rows(jnp.concatenate([zeros(rank_w), a_up]), n_lora_pad).astype(BF16),
        g_up=_pad_rows(jnp.concatenate([zeros(rank_w + rank_a), g_up]), n_lora_pad).astype(BF16),
        k_k=k_k.reshape(1, width), k_a=k_a.reshape(1, width), r_k=r_k.reshape(1, width),
        ln_w=ln_w.reshape(1, width), ln_b=ln_b.reshape(1, width),
    )
    rw = _rwkv(rkv, lora, prm, batch=batch, seq=seq)

    h1 = _out_proj(attn, rw, w_out.astype(BF16), h, tm=tile_m, tn=d // 2)
    act = _ffn_up(h1, norm2_g.reshape(1, d), w_gate_up.astype(BF16), tm=tile_m, th=512)
    return _ffn_down(act, w_down.astype(BF16), h1, out_g.reshape(1, d), tm=tile_m // 2, tn=512)


def kernel(x, norm1_g, w_in, rwkv_shift_mix, rwkv_w0, rwkv_w_up, rwkv_a0, rwkv_a_up, rwkv_g_up, rwkv_k_k, rwkv_k_a, rwkv_r_k, rwkv_ln_w, rwkv_ln_b, rel_bias_table, w_out, norm2_g, w_gate_up, w_down, final_g):
    batch, seq, d = x.shape
    depth = w_in.shape[0]
    assert depth == 1, "the final RMSNorm is fused into the last layer's FFN kernel"
    assert seq % DILATED_PATTERNS[-1][0] == 0 and seq % CHUNK == 0
    h = x.reshape(batch * seq, d)
    out = _layer(h, norm1_g[0], w_in[0], rwkv_shift_mix[0], rwkv_w0[0], rwkv_w_up[0], rwkv_a0[0],
                 rwkv_a_up[0], rwkv_g_up[0], rwkv_k_k[0], rwkv_k_a[0], rwkv_r_k[0], rwkv_ln_w[0],
                 rwkv_ln_b[0], rel_bias_table, w_out[0], norm2_g[0], w_gate_up[0], w_down[0], final_g,
                 batch=batch, seq=seq, tile_m=min(1024, batch * seq))
    return out.reshape(batch, seq, d)
```

```python
import functools
import math

import jax
import jax.numpy as jnp
from jax import lax
from jax.experimental import pallas as pl
from jax.experimental.pallas import tpu as pltpu

F32 = jnp.float32
BF16 = jnp.bfloat16

HEAD_DIM = 64
LANES = 128
DILATED_PATTERNS = ((128, 1), (512, 4), (2048, 16))
REL_BUCKETS = 32
REL_MAX_DIST = 2048
RMS_EPS = 1e-6
GN_EPS = 64e-5
DECAY_SCALE = math.exp(-0.5)
ATTN_SCALE = HEAD_DIM ** -0.5
MASK_VALUE = -1e30
CHUNK = 64
VMEM_LIMIT = 56 * 1024 * 1024


def _mm(a, b):
    return jnp.dot(a.astype(BF16), b.astype(BF16), preferred_element_type=F32)


def _mm_nt(a, b):
    return lax.dot_general(a.astype(BF16), b.astype(BF16), (((1,), (1,)), ((), ())),
                           preferred_element_type=F32)


def _mm_tn(a, b):
    return lax.dot_general(a.astype(BF16), b.astype(BF16), (((0,), (0,)), ((), ())),
                           preferred_element_type=F32)


def _mm_split(a, b_exact):
    hi = a.astype(BF16)
    lo = (a - hi.astype(F32)).astype(BF16)
    return (jnp.dot(hi, b_exact, preferred_element_type=F32)
            + jnp.dot(lo, b_exact, preferred_element_type=F32))


def _params(semantics):
    return pltpu.CompilerParams(dimension_semantics=semantics, vmem_limit_bytes=VMEM_LIMIT)


def _in_proj_kernel(x_ref, g_ref, w_ref, wl_ref, qkv_ref, rkv_ref, lora_ref, xn_ref, *, n_qkv):
    j = pl.program_id(1)

    @pl.when(j == 0)
    def _():
        x = x_ref[...]
        ms = jnp.mean(x * x, axis=-1, keepdims=True)
        xn_ref[...] = (x * lax.rsqrt(ms + RMS_EPS) * g_ref[...]).astype(BF16)

    @pl.when(j < n_qkv)
    def _():
        qkv_ref[...] = jnp.dot(xn_ref[...], w_ref[...], preferred_element_type=F32)

    @pl.when((j >= n_qkv) & (j < 2 * n_qkv))
    def _():
        rkv_ref[...] = jnp.dot(xn_ref[...], w_ref[...], preferred_element_type=F32)

    @pl.when(j == 2 * n_qkv)
    def _():
        lora_ref[...] = jnp.dot(xn_ref[...], wl_ref[...], preferred_element_type=F32)


def _in_proj(x2, g, w_main, w_lora, *, tm, tn):
    t, d = x2.shape
    n_main = w_main.shape[1]
    n_lora = w_lora.shape[1]
    n_qkv = (n_main // 2) // tn
    last = 2 * n_qkv - 1
    return pl.pallas_call(
        functools.partial(_in_proj_kernel, n_qkv=n_qkv),
        grid=(t // tm, 2 * n_qkv + 1),
        in_specs=[
            pl.BlockSpec((tm, d), lambda i, j: (i, 0)),
            pl.BlockSpec((1, d), lambda i, j: (0, 0)),
            pl.BlockSpec((d, tn), lambda i, j: (0, jnp.minimum(j, last))),
            pl.BlockSpec((d, n_lora), lambda i, j: (0, 0)),
        ],
        out_specs=[
            pl.BlockSpec((tm, tn), lambda i, j: (i, jnp.minimum(j, n_qkv - 1))),
            pl.BlockSpec((tm, tn), lambda i, j: (i, jnp.clip(j - n_qkv, 0, n_qkv - 1))),
            pl.BlockSpec((tm, n_lora), lambda i, j: (i, 0)),
        ],
        out_shape=[
            jax.ShapeDtypeStruct((t, n_main // 2), F32),
            jax.ShapeDtypeStruct((t, n_main // 2), F32),
            jax.ShapeDtypeStruct((t, n_lora), F32),
        ],
        scratch_shapes=[pltpu.VMEM((tm, d), BF16)],
        compiler_params=_params(("arbitrary", "arbitrary")),
        name="in_proj",
    )(x2, g, w_main, w_lora)


def _attn_kernel(q_ref, kp_ref, kc_ref, vp_ref, vc_ref, bias_ref, o_ref, lse_ref, *, blk, n_heads):
    n = pl.program_id(1)
    lane = lax.broadcasted_iota(jnp.int32, (1, LANES), 1)
    key_idx = lax.broadcasted_iota(jnp.int32, (1, 2 * blk), 1)
    key_ok = (key_idx >= blk) | (n > 0)
    q = q_ref[...] * ATTN_SCALE
    for hp in range(n_heads // 2):
        cols = slice(hp * LANES, (hp + 1) * LANES)
        q2 = q[:, cols]
        k2 = jnp.concatenate([kp_ref[:, cols], kc_ref[:, cols]], axis=0)
        v2 = jnp.concatenate([vp_ref[:, cols], vc_ref[:, cols]], axis=0).astype(BF16)
        outs = []
        for half in range(2):
            in_head = (lane >= half * HEAD_DIM) & (lane < (half + 1) * HEAD_DIM)
            s = _mm_nt(jnp.where(in_head, q2, 0.0), k2) + bias_ref[2 * hp + half]
            s = jnp.where(key_ok, s, MASK_VALUE)
            m = jnp.max(s, axis=-1, keepdims=True)
            p = jnp.exp(s - m)
            l = jnp.sum(p, axis=-1, keepdims=True)
            o = jnp.dot(p.astype(BF16), v2, preferred_element_type=F32) / l
            outs.append((in_head, o, m + jnp.log(l)))
        (ma, oa, la), (_, ob, lb) = outs
        o_ref[:, cols] = jnp.where(ma, oa, ob)
        lse_ref[:, cols] = jnp.where(ma, la, lb)


def _attn_branch(qkv, bias, *, batch, seq, window, dilation, n_heads):
    blk = window // dilation
    span = blk * dilation
    nblk = seq // span
    width = n_heads * HEAD_DIM
    qkv4 = qkv.reshape(batch, nblk, blk, dilation * 3 * width)
    blk_spec = lambda f: pl.BlockSpec((None, None, blk, width), f)
    out_shape = jax.ShapeDtypeStruct((batch, nblk, blk, dilation * width), F32)
    o, lse = pl.pallas_call(
        functools.partial(_attn_kernel, blk=blk, n_heads=n_heads),
        grid=(batch, nblk, dilation),
        in_specs=[
            blk_spec(lambda b, n, r: (b, n, 0, 3 * r)),
            blk_spec(lambda b, n, r: (b, jnp.maximum(n - 1, 0), 0, 3 * r + 1)),
            blk_spec(lambda b, n, r: (b, n, 0, 3 * r + 1)),
            blk_spec(lambda b, n, r: (b, jnp.maximum(n - 1, 0), 0, 3 * r + 2)),
            blk_spec(lambda b, n, r: (b, n, 0, 3 * r + 2)),
            pl.BlockSpec((n_heads, blk, 2 * blk), lambda b, n, r: (0, 0, 0)),
        ],
        out_specs=[blk_spec(lambda b, n, r: (b, n, 0, r)), blk_spec(lambda b, n, r: (b, n, 0, r))],
        out_shape=[out_shape, out_shape],
        compiler_params=_params(("arbitrary", "arbitrary", "arbitrary")),
        name=f"attn_w{window}_d{dilation}",
    )(qkv4, qkv4, qkv4, qkv4, qkv4, bias)
    return o.reshape(batch * seq, width), lse.reshape(batch * seq, width)


def _attn_mix_kernel(o1, l1, o2, l2, o3, l3, out_ref):
    a, b, c = l1[...], l2[...], l3[...]
    m = jnp.maximum(jnp.maximum(a, b), c)
    ea, eb, ec = jnp.exp(a - m), jnp.exp(b - m), jnp.exp(c - m)
    out_ref[...] = (ea * o1[...] + eb * o2[...] + ec * o3[...]) / (ea + eb + ec)


def _attn_mix(pairs, *, tm):
    t, width = pairs[0][0].shape
    spec = pl.BlockSpec((tm, width), lambda i: (i, 0))
    flat = [a for pair in pairs for a in pair]
    return pl.pallas_call(
        _attn_mix_kernel,
        grid=(t // tm,),
        in_specs=[spec] * 6,
        out_specs=spec,
        out_shape=jax.ShapeDtypeStruct((t, width), F32),
        compiler_params=_params(("arbitrary",)),
        name="attn_mix",
    )(*flat)


def _t5_bucket(dist):
    exact = REL_BUCKETS // 2
    d_f = jnp.maximum(dist, 1).astype(F32)
    large = exact + (jnp.log(d_f / exact) / math.log(REL_MAX_DIST / exact)
                     * (REL_BUCKETS - exact)).astype(jnp.int32)
    large = jnp.minimum(large, REL_BUCKETS - 1)
    return jnp.where(dist < exact, dist, large)


def _band_bias(bias_table, blk, dilation):
    qi = jnp.arange(blk)[:, None]
    ki = jnp.arange(2 * blk)[None, :]
    rel = qi + blk - ki
    band = (rel >= 0) & (rel <= blk)
    bias = bias_table[_t5_bucket(jnp.clip(rel, 0, blk) * dilation)]
    bias = jnp.where(band[..., None], bias.astype(F32), MASK_VALUE)
    return jnp.transpose(bias, (2, 0, 1))


def _rwkv_kernel(rkv_ref, lora_ref, mix_rkv_ref, mix_lora_ref, w0_ref, w_up_ref, a0_ref, a_up_ref,
                 g_up_ref, kk_ref, ka_ref, rk_ref, lnw_ref, lnb_ref, seg_ref, tri_ref,
                 out_ref,
                 state_ref, carry_rkv_ref, carry_lora_ref,
                 rt_ref, at_ref, bh_ref, kh_ref, bc_ref, kc_ref, v_ref, pc_ref, y_ref, *, width):
    c = pl.program_id(1)
    n_pairs = width // LANES
    ch = rkv_ref.shape[0]

    @pl.when(c == 0)
    def _():
        state_ref[...] = jnp.zeros_like(state_ref)
        carry_rkv_ref[...] = jnp.zeros_like(carry_rkv_ref)
        carry_lora_ref[...] = jnp.zeros_like(carry_lora_ref)

    row = lax.broadcasted_iota(jnp.int32, (ch, 1), 0)

    def token_shift(z, carry_ref, mix):
        prev = jnp.where(row == 0, carry_ref[...], pltpu.roll(z, 1, axis=0))
        carry_ref[...] = z[ch - 1:ch, :]
        return z + (prev - z) * mix

    z = token_shift(rkv_ref[...], carry_rkv_ref, mix_rkv_ref[...])
    zl = token_shift(lora_ref[...], carry_lora_ref, mix_lora_ref[...])
    r, k, v = z[:, :width], z[:, width:2 * width], z[:, 2 * width:]

    lw = -DECAY_SCALE * jax.nn.sigmoid(w0_ref[...] + _mm(jnp.tanh(zl), w_up_ref[...]))
    a_sig = jax.nn.sigmoid(a0_ref[...] + _mm(zl, a_up_ref[...]))
    gate = _mm(jax.nn.sigmoid(zl), g_up_ref[...])

    seg = seg_ref[...]
    kk = k * kk_ref[...]
    kk = kk / jnp.maximum(jnp.sqrt(_mm_split(kk * kk, seg)), 1e-12)
    k = k * (1.0 + (a_sig - 1.0) * ka_ref[...])
    bonus = _mm_split(r * k * rk_ref[...], seg) * v
    a_in = -kk
    b_in = kk * a_sig

    lw_hi = lw.astype(BF16)
    lw_lo = (lw - lw_hi.astype(F32)).astype(BF16)
    tri = tri_ref[...]
    cum = (jnp.dot(tri, lw_hi, preferred_element_type=F32)
           + jnp.dot(tri, lw_lo, preferred_element_type=F32))
    cum_end = cum[ch - 1:ch, :]
    e_neg = jnp.exp(-cum)
    e_end = jnp.exp(cum_end - cum)

    def put(ref, val):
        for p in range(n_pairs):
            ref[p] = val[:, p * LANES:(p + 1) * LANES]

    put(rt_ref, r * jnp.exp(cum))
    put(at_ref, a_in * jnp.exp(cum - lw))
    put(bh_ref, b_in * e_neg)
    put(kh_ref, k * e_neg)
    put(bc_ref, b_in * e_end)
    put(kc_ref, k * e_end)
    put(v_ref, v)
    put(pc_ref, jnp.exp(cum_end))

    ti = lax.broadcasted_iota(jnp.int32, (ch, ch), 0)
    si = lax.broadcasted_iota(jnp.int32, (ch, ch), 1)
    incl = ti >= si
    strict = ti > si
    lane = lax.broadcasted_iota(jnp.int32, (1, LANES), 1)
    di = lax.broadcasted_iota(jnp.int32, (LANES, LANES), 0)
    dj = lax.broadcasted_iota(jnp.int32, (LANES, LANES), 1)
    same_head = (di < HEAD_DIM) == (dj < HEAD_DIM)
    diag = di == dj

    heads = [(p, half) for p in range(n_pairs) for half in range(2)]
    in_head = [(lane >= half * HEAD_DIM) & (lane < (half + 1) * HEAD_DIM) for half in range(2)]

    a_rb, a_ab, a_rk, a_ak = [], [], [], []
    for p in range(n_pairs):
        rt, at = rt_ref[p], at_ref[p]
        lhs = jnp.concatenate([jnp.where(m, t, 0.0) for m in in_head for t in (rt, at)], axis=0)
        a_b = _mm_nt(lhs, bh_ref[p])
        a_k = _mm_nt(lhs, kh_ref[p])
        for half in range(2):
            o = 2 * half * ch
            a_rb.append(jnp.where(incl, a_b[o:o + ch], 0.0))
            a_ab.append(jnp.where(strict, a_b[o + ch:o + 2 * ch], 0.0))
            a_rk.append(jnp.where(incl, a_k[o:o + ch], 0.0))
            a_ak.append(jnp.where(strict, a_k[o + ch:o + 2 * ch], 0.0))

    vh = [jnp.where(in_head[half], v_ref[p], 0.0) for p, half in heads]
    xs = [jnp.concatenate([jnp.where(in_head[half], at_ref[p], 0.0), _mm(a_ak[i], vh[i])], axis=1)
          for i, (p, half) in enumerate(heads)]
    nk = a_ab
    xs = [x + _mm(n, x) for n, x in zip(nk, xs)]
    for _ in range(int(math.log2(ch)) - 1):
        nk = [_mm(n, n) for n in nk]
        xs = [x + _mm(n, x) for n, x in zip(nk, xs)]
    ax = [_mm(a, x) for a, x in zip(a_rb, xs)]
    ykv = [_mm(a, vv) for a, vv in zip(a_rk, vh)]

    for p in range(n_pairs):
        i0, i1 = 2 * p, 2 * p + 1
        w_acc = xs[i0][:, :LANES] + xs[i1][:, :LANES]
        u_acc = xs[i0][:, LANES:] + xs[i1][:, LANES:]
        q_acc = rt_ref[p] + ax[i0][:, :LANES] + ax[i1][:, :LANES]
        y_acc = ax[i0][:, LANES:] + ax[i1][:, LANES:] + ykv[i0] + ykv[i1]
        bt = _mm_tn(bc_ref[p], jnp.concatenate([w_acc, u_acc], axis=1))
        kt = _mm_tn(kc_ref[p], v_ref[p])
        m_mat = jnp.where(same_head, bt[:, :LANES], 0.0) + jnp.where(diag, pc_ref[p], 0.0)
        g_mat = jnp.where(same_head, bt[:, LANES:] + kt, 0.0)
        h = state_ref[p]
        y_ref[p] = _mm(q_acc, h) + y_acc
        state_ref[p] = _mm(m_mat, h) + g_mat

    y = jnp.concatenate([y_ref[p] for p in range(n_pairs)], axis=1)
    inv_n = 1.0 / HEAD_DIM
    mu = _mm_split(y, seg) * inv_n
    d = y - mu
    var = _mm_split(d * d, seg) * inv_n
    yn = d * lax.rsqrt(var + GN_EPS) * lnw_ref[...] + lnb_ref[...]
    out_ref[...] = (yn + bonus) * gate


def _rwkv(rkv, lora, prm, *, batch, seq):
    t, w3 = rkv.shape
    width = w3 // 3
    n_lora = lora.shape[1]
    ch = CHUNK
    n_chunks = seq // ch
    n_pairs = width // LANES
    row = lambda n: pl.BlockSpec((1, n), lambda b, c: (0, 0))
    full = lambda a: pl.BlockSpec(a.shape, lambda b, c: (0,) * a.ndim)
    head_id = jnp.arange(width) // HEAD_DIM
    seg = (head_id[:, None] == head_id[None, :]).astype(BF16)
    tri = (jnp.arange(ch)[:, None] >= jnp.arange(ch)[None, :]).astype(BF16)
    big = lambda: pltpu.VMEM((n_pairs, ch, LANES), F32)
    args = (rkv, lora, prm["mix_rkv"], prm["mix_lora"], prm["w0"], prm["w_up"], prm["a0"], prm["a_up"],
            prm["g_up"], prm["k_k"], prm["k_a"], prm["r_k"], prm["ln_w"], prm["ln_b"], seg, tri)
    in_specs = [
        pl.BlockSpec((ch, w3), lambda b, c: (b * n_chunks + c, 0)),
        pl.BlockSpec((ch, n_lora), lambda b, c: (b * n_chunks + c, 0)),
        row(w3), row(n_lora), row(width), full(prm["w_up"]), row(width), full(prm["a_up"]),
        full(prm["g_up"]), row(width), row(width), row(width), row(width), row(width),
        full(seg), full(tri),
    ]
    return pl.pallas_call(
        functools.partial(_rwkv_kernel, width=width),
        grid=(batch, n_chunks),
        in_specs=in_specs,
        out_specs=pl.BlockSpec((ch, width), lambda b, c: (b * n_chunks + c, 0)),
        out_shape=jax.ShapeDtypeStruct((t, width), F32),
        scratch_shapes=[
            pltpu.VMEM((n_pairs, LANES, LANES), F32),
            pltpu.VMEM((1, w3), F32),
            pltpu.VMEM((1, n_lora), F32),
            big(), big(), big(), big(), big(), big(), big(),
            pltpu.VMEM((n_pairs, 1, LANES), F32),
            big(),
        ],
        compiler_params=_params(("arbitrary", "arbitrary")),
        name="rwkv7",
    )(*args)


def _out_proj_kernel(attn_ref, rw_ref, w_ref, x_ref, out_ref, lhs_ref, *, width):
    @pl.when(pl.program_id(1) == 0)
    def _():
        lhs_ref[:, :width] = attn_ref[...].astype(BF16)
        lhs_ref[:, width:] = rw_ref[...].astype(BF16)

    out_ref[...] = x_ref[...] + jnp.dot(lhs_ref[...], w_ref[...], preferred_element_type=F32)


def _out_proj(attn, rw, w_out, x2, *, tm, tn):
    t, width = attn.shape
    d = w_out.shape[1]
    return pl.pallas_call(
        functools.partial(_out_proj_kernel, width=width),
        grid=(t // tm, d // tn),
        in_specs=[
            pl.BlockSpec((tm, width), lambda i, j: (i, 0)),
            pl.BlockSpec((tm, width), lambda i, j: (i, 0)),
            pl.BlockSpec((2 * width, tn), lambda i, j: (0, j)),
            pl.BlockSpec((tm, tn), lambda i, j: (i, j)),
        ],
        out_specs=pl.BlockSpec((tm, tn), lambda i, j: (i, j)),
        out_shape=jax.ShapeDtypeStruct((t, d), F32),
        scratch_shapes=[pltpu.VMEM((tm, 2 * width), BF16)],
        compiler_params=_params(("arbitrary", "arbitrary")),
        name="out_proj",
    )(attn, rw, w_out, x2)


def _ffn_up_kernel(h_ref, g_ref, wg_ref, wu_ref, act_ref, hn_ref):
    @pl.when(pl.program_id(1) == 0)
    def _():
        h = h_ref[...]
        ms = jnp.mean(h * h, axis=-1, keepdims=True)
        hn_ref[...] = (h * lax.rsqrt(ms + RMS_EPS) * g_ref[...]).astype(BF16)

    hn = hn_ref[...]
    gate = jnp.dot(hn, wg_ref[...], preferred_element_type=F32)
    up = jnp.dot(hn, wu_ref[...], preferred_element_type=F32)
    act_ref[...] = (gate * jax.nn.sigmoid(gate) * up).astype(BF16)


def _ffn_up(h, g, w_gate_up, *, tm, th):
    t, d = h.shape
    hidden = w_gate_up.shape[1] // 2
    n_h = hidden // th
    return pl.pallas_call(
        _ffn_up_kernel,
        grid=(t // tm, n_h),
        in_specs=[
            pl.BlockSpec((tm, d), lambda i, j: (i, 0)),
            pl.BlockSpec((1, d), lambda i, j: (0, 0)),
            pl.BlockSpec((d, th), lambda i, j: (0, j)),
            pl.BlockSpec((d, th), lambda i, j: (0, n_h + j)),
        ],
        out_specs=pl.BlockSpec((tm, th), lambda i, j: (i, j)),
        out_shape=jax.ShapeDtypeStruct((t, hidden), BF16),
        scratch_shapes=[pltpu.VMEM((tm, d), BF16)],
        compiler_params=_params(("arbitrary", "arbitrary")),
        name="ffn_up",
    )(h, g, w_gate_up, w_gate_up)


def _ffn_down_kernel(act_ref, w_ref, h_ref, g_ref, out_ref, res_ref, ssq_ref, *, d):
    j = pl.program_id(1)
    n_j, _, tn = res_ref.shape
    res = h_ref[...] + jnp.dot(act_ref[...], w_ref[...], preferred_element_type=F32)
    res_ref[j] = res
    part = jnp.sum(res * res, axis=-1, keepdims=True)

    @pl.when(j == 0)
    def _():
        ssq_ref[...] = part

    @pl.when(j > 0)
    def _():
        ssq_ref[...] += part

    @pl.when(j == n_j - 1)
    def _():
        scale = lax.rsqrt(ssq_ref[...] * (1.0 / d) + RMS_EPS)
        for jj in range(n_j):
            cols = slice(jj * tn, (jj + 1) * tn)
            out_ref[:, cols] = res_ref[jj] * scale * g_ref[:, cols]


def _ffn_down(act, w_down, h, g, *, tm, tn):
    t, hidden = act.shape
    d = w_down.shape[1]
    return pl.pallas_call(
        functools.partial(_ffn_down_kernel, d=d),
        grid=(t // tm, d // tn),
        in_specs=[
            pl.BlockSpec((tm, hidden), lambda i, j: (i, 0)),
            pl.BlockSpec((hidden, tn), lambda i, j: (0, j)),
            pl.BlockSpec((tm, tn), lambda i, j: (i, j)),
            pl.BlockSpec((1, d), lambda i, j: (0, 0)),
        ],
        out_specs=pl.BlockSpec((tm, d), lambda i, j: (i, 0)),
        out_shape=jax.ShapeDtypeStruct((t, d), F32),
        scratch_shapes=[pltpu.VMEM((d // tn, tm, tn), F32), pltpu.VMEM((tm, 1), F32)],
        compiler_params=_params(("arbitrary", "arbitrary")),
        name="ffn_down",
    )(act, w_down, h, g)


def _pad_rows(a, rows):
    return jnp.pad(a, ((0, rows - a.shape[0]), (0, 0)))


def _layer(h, norm1_g, w_in, shift_mix, w0, w_up, a0, a_up, g_up, k_k, k_a, r_k, ln_w, ln_b,
           rel_bias_table, w_out, norm2_g, w_gate_up, w_down, out_g, *, batch, seq, tile_m):
    d = h.shape[1]
    width = w_out.shape[0] // 2
    n_heads = width // HEAD_DIM
    n_main = 6 * width
    rank_w, rank_a, rank_g = w_up.shape[0], a_up.shape[0], g_up.shape[0]
    n_lora = rank_w + rank_a + rank_g
    n_lora_pad = -(-n_lora // LANES) * LANES

    w_main = w_in[:, :n_main].astype(BF16)
    w_lora = jnp.pad(w_in[:, n_main:], ((0, 0), (0, n_lora_pad - n_lora))).astype(BF16)
    qkv, rkv, lora = _in_proj(h, norm1_g.reshape(1, d), w_main, w_lora, tm=tile_m, tn=3 * width // 4)

    pairs = []
    for window, dilation in DILATED_PATTERNS:
        bias = _band_bias(rel_bias_table.astype(F32), window // dilation, dilation)
        pairs.append(_attn_branch(qkv, bias, batch=batch, seq=seq, window=window, dilation=dilation,
                                  n_heads=n_heads))
    attn = _attn_mix(pairs, tm=tile_m // 2)

    mix_rkv = shift_mix[:3 * width].reshape(1, 3 * width)
    mix_lora = jnp.pad(shift_mix[3 * width:], (0, n_lora_pad - n_lora)).reshape(1, n_lora_pad)
    zeros = lambda n: jnp.zeros((n, width), F32)
    prm = dict(
        mix_rkv=mix_rkv, mix_lora=mix_lora,
        w0=w0.reshape(1, width), a0=a0.reshape(1, width),
        w_up=_pad_rows(w_up, n_lora_pad).astype(BF16),
        a_up=_pad_rows(jnp.concatenate([zeros(rank_w), a_up]), n_lora_pad).astype(BF16),
        g_up=_pad_rows(jnp.concatenate([zeros(rank_w + rank_a), g_up]), n_lora_pad).astype(BF16),
        k_k=k_k.reshape(1, width), k_a=k_a.reshape(1, width), r_k=r_k.reshape(1, width),
        ln_w=ln_w.reshape(1, width), ln_b=ln_b.reshape(1, width),
    )
    rw = _rwkv(rkv, lora, prm, batch=batch, seq=seq)

    h1 = _out_proj(attn, rw, w_out.astype(BF16), h, tm=tile_m, tn=d // 2)
    act = _ffn_up(h1, norm2_g.reshape(1, d), w_gate_up.astype(BF16), tm=tile_m, th=512)
    return _ffn_down(act, w_down.astype(BF16), h1, out_g.reshape(1, d), tm=tile_m // 2, tn=512)


def kernel(x, norm1_g, w_in, rwkv_shift_mix, rwkv_w0, rwkv_w_up, rwkv_a0, rwkv_a_up, rwkv_g_up, rwkv_k_k, rwkv_k_a, rwkv_r_k, rwkv_ln_w, rwkv_ln_b, rel_bias_table, w_out, norm2_g, w_gate_up, w_down, final_g):
    batch, seq, d = x.shape
    depth = w_in.shape[0]
    assert depth == 1, "the final RMSNorm is fused into the last layer's FFN kernel"
    assert seq % DILATED_PATTERNS[-1][0] == 0 and seq % CHUNK == 0
    h = x.reshape(batch * seq, d)
    out = _layer(h, norm1_g[0], w_in[0], rwkv_shift_mix[0], rwkv_w0[0], rwkv_w_up[0], rwkv_a0[0],
                 rwkv_a_up[0], rwkv_g_up[0], rwkv_k_k[0], rwkv_k_a[0], rwkv_r_k[0], rwkv_ln_w[0],
                 rwkv_ln_b[0], rel_bias_table, w_out[0], norm2_g[0], w_gate_up[0], w_down[0], final_g,
                 batch=batch, seq=seq, tile_m=min(1024, batch * seq))
    return out.reshape(batch, seq, d)
```

```python
import functools
import math

import jax
import jax.numpy as jnp
from jax import lax
from jax.experimental import pallas as pl
from jax.experimental.pallas import tpu as pltpu

F32 = jnp.float32
BF16 = jnp.bfloat16

HEAD_DIM = 64
LANES = 128
DILATED_PATTERNS = ((128, 1), (512, 4), (2048, 16))
REL_BUCKETS = 32
REL_MAX_DIST = 2048
RMS_EPS = 1e-6
GN_EPS = 64e-5
DECAY_SCALE = math.exp(-0.5)
ATTN_SCALE = HEAD_DIM ** -0.5
MASK_VALUE = -1e30
CHUNK = 64
VMEM_LIMIT = 56 * 1024 * 1024


def _mm(a, b):
    return jnp.dot(a.astype(BF16), b.astype(BF16), preferred_element_type=F32)


def _mm_nt(a, b):
    return lax.dot_general(a.astype(BF16), b.astype(BF16), (((1,), (1,)), ((), ())),
                           preferred_element_type=F32)


def _mm_tn(a, b):
    return lax.dot_general(a.astype(BF16), b.astype(BF16), (((0,), (0,)), ((), ())),
                           preferred_element_type=F32)


def _mm_split(a, b_exact):
    hi = a.astype(BF16)
    lo = (a - hi.astype(F32)).astype(BF16)
    return (jnp.dot(hi, b_exact, preferred_element_type=F32)
            + jnp.dot(lo, b_exact, preferred_element_type=F32))


def _params(semantics):
    return pltpu.CompilerParams(dimension_semantics=semantics, vmem_limit_bytes=VMEM_LIMIT)


def _in_proj_kernel(x_ref, g_ref, w_ref, wl_ref, qkv_ref, rkv_ref, lora_ref, xn_ref, *, n_qkv):
    j = pl.program_id(1)

    @pl.when(j == 0)
    def _():
        x = x_ref[...]
        ms = jnp.mean(x * x, axis=-1, keepdims=True)
        xn_ref[...] = (x * lax.rsqrt(ms + RMS_EPS) * g_ref[...]).astype(BF16)

    @pl.when(j < n_qkv)
    def _():
        qkv_ref[...] = jnp.dot(xn_ref[...], w_ref[...], preferred_element_type=F32)

    @pl.when((j >= n_qkv) & (j < 2 * n_qkv))
    def _():
        rkv_ref[...] = jnp.dot(xn_ref[...], w_ref[...], preferred_element_type=F32)

    @pl.when(j == 2 * n_qkv)
    def _():
        lora_ref[...] = jnp.dot(xn_ref[...], wl_ref[...], preferred_element_type=F32)


def _in_proj(x2, g, w_main, w_lora, *, tm, tn):
    t, d = x2.shape
    n_main = w_main.shape[1]
    n_lora = w_lora.shape[1]
    n_qkv = (n_main // 2) // tn
    last = 2 * n_qkv - 1
    return pl.pallas_call(
        functools.partial(_in_proj_kernel, n_qkv=n_qkv),
        grid=(t // tm, 2 * n_qkv + 1),
        in_specs=[
            pl.BlockSpec((tm, d), lambda i, j: (i, 0)),
            pl.BlockSpec((1, d), lambda i, j: (0, 0)),
            pl.BlockSpec((d, tn), lambda i, j: (0, jnp.minimum(j, last))),
            pl.BlockSpec((d, n_lora), lambda i, j: (0, 0)),
        ],
        out_specs=[
            pl.BlockSpec((tm, tn), lambda i, j: (i, jnp.minimum(j, n_qkv - 1))),
            pl.BlockSpec((tm, tn), lambda i, j: (i, jnp.clip(j - n_qkv, 0, n_qkv - 1))),
            pl.BlockSpec((tm, n_lora), lambda i, j: (i, 0)),
        ],
        out_shape=[
            jax.ShapeDtypeStruct((t, n_main // 2), F32),
            jax.ShapeDtypeStruct((t, n_main // 2), F32),
            jax.ShapeDtypeStruct((t, n_lora), F32),
        ],
        scratch_shapes=[pltpu.VMEM((tm, d), BF16)],
        compiler_params=_params(("arbitrary", "arbitrary")),
        name="in_proj",
    )(x2, g, w_main, w_lora)


ATTN_BLK = 128
ATTN_ROWS = max(w for w, _ in DILATED_PATTERNS)
ATTN_GROUP = 4


def _attn_kernel(q_ref, kp_ref, kc_ref, vp_ref, vc_ref, band_ref, out_ref, bias_ref, o_ref, lse_ref,
                 *, n_heads):
    b, n, hp = pl.program_id(0), pl.program_id(1), pl.program_id(2)
    blk = ATTN_BLK

    @pl.when((b == 0) & (n == 0) & (hp == 0))
    def _():
        for i in range(bias_ref.shape[0]):
            row = jnp.broadcast_to(band_ref[i:i + 1, :], (blk, 2 * blk))
            bias_ref[i] = pltpu.roll(row, 0, 1, stride=1, stride_axis=0)

    lane = lax.broadcasted_iota(jnp.int32, (1, LANES), 1)
    in_head = [(lane >= half * HEAD_DIM) & (lane < (half + 1) * HEAD_DIM) for half in range(2)]
    key_idx = lax.broadcasted_iota(jnp.int32, (1, 2 * blk), 1)
    first_ok = (key_idx >= blk) | (n > 0)

    for bi, (window, dilation) in enumerate(DILATED_PATTERNS):
        units = [(sub, r) for sub in range(ATTN_ROWS // window) for r in range(dilation)]
        for g0 in range(0, len(units), ATTN_GROUP):
            group = units[g0:g0 + ATTN_GROUP]
            rows, q2, k2, v2, key_ok = [], [], [], [], []
            for sub, r in group:
                cur = pl.ds(sub * window + r, blk, stride=dilation)
                if sub == 0:
                    prev_k, prev_v = kp_ref, vp_ref
                    prev = pl.ds(ATTN_ROWS - window + r, blk, stride=dilation)
                else:
                    prev_k, prev_v = kc_ref, vc_ref
                    prev = pl.ds((sub - 1) * window + r, blk, stride=dilation)
                rows.append(cur)
                q2.append(q_ref[cur, :] * ATTN_SCALE)
                k2.append(jnp.concatenate([prev_k[prev, :], kc_ref[cur, :]], axis=0).astype(BF16))
                v2.append(jnp.concatenate([prev_v[prev, :], vc_ref[cur, :]], axis=0).astype(BF16))
                key_ok.append(first_ok if sub == 0 else None)
            s = [[_mm_nt(jnp.where(m, q, 0.0), k) for m in in_head] for q, k in zip(q2, k2)]
            outs = []
            for u in range(len(group)):
                halves = []
                for half in range(2):
                    su = s[u][half] + bias_ref[(bi * n_heads) + 2 * hp + half]
                    if key_ok[u] is not None:
                        su = jnp.where(key_ok[u], su, MASK_VALUE)
                    mx = jnp.max(su, axis=-1, keepdims=True)
                    p = jnp.exp(su - mx)
                    halves.append((p, mx, jnp.sum(p, axis=-1, keepdims=True)))
                outs.append(halves)
            for u in range(len(group)):
                (pa, ma, la), (pb, mb, lb) = outs[u]
                oa = jnp.dot(pa.astype(BF16), v2[u], preferred_element_type=F32) / la
                ob = jnp.dot(pb.astype(BF16), v2[u], preferred_element_type=F32) / lb
                o_ref[bi, rows[u], :] = jnp.where(in_head[0], oa, ob)
                lse_ref[bi, rows[u], :] = jnp.where(in_head[0], ma + jnp.log(la), mb + jnp.log(lb))

    lses = [lse_ref[bi] for bi in range(len(DILATED_PATTERNS))]
    top = functools.reduce(jnp.maximum, lses)
    es = [jnp.exp(l - top) for l in lses]
    num = functools.reduce(jnp.add, [e * o_ref[bi] for bi, e in enumerate(es)])
    out_ref[...] = num / functools.reduce(jnp.add, es)


def _attention(qkv, band, *, batch, seq, n_heads):
    width = n_heads * HEAD_DIM
    n_pairs = width // LANES
    nb = seq // ATTN_ROWS
    n_pat = len(DILATED_PATTERNS)
    spec = lambda f: pl.BlockSpec((ATTN_ROWS, LANES), f)
    prev = lambda b, n: b * nb + jnp.maximum(n - 1, 0)
    return pl.pallas_call(
        functools.partial(_attn_kernel, n_heads=n_heads),
        grid=(batch, nb, n_pairs),
        in_specs=[
            spec(lambda b, n, hp: (b * nb + n, hp)),
            spec(lambda b, n, hp: (prev(b, n), n_pairs + hp)),
            spec(lambda b, n, hp: (b * nb + n, n_pairs + hp)),
            spec(lambda b, n, hp: (prev(b, n), 2 * n_pairs + hp)),
            spec(lambda b, n, hp: (b * nb + n, 2 * n_pairs + hp)),
            pl.BlockSpec(band.shape, lambda b, n, hp: (0, 0)),
        ],
        out_specs=spec(lambda b, n, hp: (b * nb + n, hp)),
        out_shape=jax.ShapeDtypeStruct((batch * seq, width), F32),
        scratch_shapes=[
            pltpu.VMEM((n_pat * n_heads, ATTN_BLK, 2 * ATTN_BLK), F32),
            pltpu.VMEM((n_pat, ATTN_ROWS, LANES), F32),
            pltpu.VMEM((n_pat, ATTN_ROWS, LANES), F32),
        ],
        compiler_params=_params(("arbitrary", "arbitrary", "arbitrary")),
        name="dilated_attn",
    )(qkv, qkv, qkv, qkv, qkv, band)


def _t5_bucket(dist):
    exact = REL_BUCKETS // 2
    d_f = jnp.maximum(dist, 1).astype(F32)
    large = exact + (jnp.log(d_f / exact) / math.log(REL_MAX_DIST / exact)
                     * (REL_BUCKETS - exact)).astype(jnp.int32)
    large = jnp.minimum(large, REL_BUCKETS - 1)
    return jnp.where(dist < exact, dist, large)


def _band_rows(bias_table):
    blk = ATTN_BLK
    rel = blk - jnp.arange(2 * blk)
    band = (rel >= 0) & (rel <= blk)
    rows = []
    for _, dilation in DILATED_PATTERNS:
        bias = bias_table[_t5_bucket(jnp.clip(rel, 0, blk) * dilation)]
        rows.append(jnp.where(band[:, None], bias.astype(F32), MASK_VALUE).T)
    return jnp.concatenate(rows, axis=0)


def _rwkv_kernel(rkv_ref, lora_ref, mix_rkv_ref, mix_lora_ref, w0_ref, w_up_ref, a0_ref, a_up_ref,
                 g_up_ref, kk_ref, ka_ref, rk_ref, lnw_ref, lnb_ref, seg_ref, tri_ref,
                 out_ref,
                 state_ref, carry_rkv_ref, carry_lora_ref,
                 rt_ref, at_ref, bh_ref, kh_ref, bc_ref, kc_ref, v_ref, pc_ref, y_ref, *, width):
    c = pl.program_id(1)
    n_pairs = width // LANES
    ch = rkv_ref.shape[0]

    @pl.when(c == 0)
    def _():
        state_ref[...] = jnp.zeros_like(state_ref)
        carry_rkv_ref[...] = jnp.zeros_like(carry_rkv_ref)
        carry_lora_ref[...] = jnp.zeros_like(carry_lora_ref)

    row = lax.broadcasted_iota(jnp.int32, (ch, 1), 0)

    def token_shift(z, carry_ref, mix):
        prev = jnp.where(row == 0, carry_ref[...], pltpu.roll(z, 1, axis=0))
        carry_ref[...] = z[ch - 1:ch, :]
        return z + (prev - z) * mix

    z = token_shift(rkv_ref[...], carry_rkv_ref, mix_rkv_ref[...])
    zl = token_shift(lora_ref[...], carry_lora_ref, mix_lora_ref[...])
    r, k, v = z[:, :width], z[:, width:2 * width], z[:, 2 * width:]

    lw = -DECAY_SCALE * jax.nn.sigmoid(w0_ref[...] + _mm(jnp.tanh(zl), w_up_ref[...]))
    a_sig = jax.nn.sigmoid(a0_ref[...] + _mm(zl, a_up_ref[...]))
    gate = _mm(jax.nn.sigmoid(zl), g_up_ref[...])

    seg = seg_ref[...]
    kk = k * kk_ref[...]
    kk = kk / jnp.maximum(jnp.sqrt(_mm_split(kk * kk, seg)), 1e-12)
    k = k * (1.0 + (a_sig - 1.0) * ka_ref[...])
    bonus = _mm_split(r * k * rk_ref[...], seg) * v
    a_in = -kk
    b_in = kk * a_sig

    lw_hi = lw.astype(BF16)
    lw_lo = (lw - lw_hi.astype(F32)).astype(BF16)
    tri = tri_ref[...]
    cum = (jnp.dot(tri, lw_hi, preferred_element_type=F32)
           + jnp.dot(tri, lw_lo, preferred_element_type=F32))
    cum_end = cum[ch - 1:ch, :]
    e_neg = jnp.exp(-cum)
    e_end = jnp.exp(cum_end - cum)

    def put(ref, val):
        for p in range(n_pairs):
            ref[p] = val[:, p * LANES:(p + 1) * LANES]

    put(rt_ref, r * jnp.exp(cum))
    put(at_ref, a_in * jnp.exp(cum - lw))
    put(bh_ref, b_in * e_neg)
    put(kh_ref, k * e_neg)
    put(bc_ref, b_in * e_end)
    put(kc_ref, k * e_end)
    put(v_ref, v)
    put(pc_ref, jnp.exp(cum_end))

    ti = lax.broadcasted_iota(jnp.int32, (ch, ch), 0)
    si = lax.broadcasted_iota(jnp.int32, (ch, ch), 1)
    incl = ti >= si
    strict = ti > si
    lane = lax.broadcasted_iota(jnp.int32, (1, LANES), 1)
    di = lax.broadcasted_iota(jnp.int32, (LANES, LANES), 0)
    dj = lax.broadcasted_iota(jnp.int32, (LANES, LANES), 1)
    same_head = (di < HEAD_DIM) == (dj < HEAD_DIM)
    diag = di == dj

    heads = [(p, half) for p in range(n_pairs) for half in range(2)]
    in_head = [(lane >= half * HEAD_DIM) & (lane < (half + 1) * HEAD_DIM) for half in range(2)]

    a_rb, a_ab, a_rk, a_ak = [], [], [], []
    for p in range(n_pairs):
        rt, at = rt_ref[p], at_ref[p]
        lhs = jnp.concatenate([jnp.where(m, t, 0.0) for m in in_head for t in (rt, at)], axis=0)
        a_b = _mm_nt(lhs, bh_ref[p])
        a_k = _mm_nt(lhs, kh_ref[p])
        for half in range(2):
            o = 2 * half * ch
            a_rb.append(jnp.where(incl, a_b[o:o + ch], 0.0))
            a_ab.append(jnp.where(strict, a_b[o + ch:o + 2 * ch], 0.0))
            a_rk.append(jnp.where(incl, a_k[o:o + ch], 0.0))
            a_ak.append(jnp.where(strict, a_k[o + ch:o + 2 * ch], 0.0))

    vh = [jnp.where(in_head[half], v_ref[p], 0.0) for p, half in heads]
    xs = [jnp.concatenate([jnp.where(in_head[half], at_ref[p], 0.0), _mm(a_ak[i], vh[i])], axis=1)
          for i, (p, half) in enumerate(heads)]
    nk = a_ab
    xs = [x + _mm(n, x) for n, x in zip(nk, xs)]
    for _ in range(int(math.log2(ch)) - 1):
        nk = [_mm(n, n) for n in nk]
        xs = [x + _mm(n, x) for n, x in zip(nk, xs)]
    ax = [_mm(a, x) for a, x in zip(a_rb, xs)]
    ykv = [_mm(a, vv) for a, vv in zip(a_rk, vh)]

    for p in range(n_pairs):
        i0, i1 = 2 * p, 2 * p + 1
        w_acc = xs[i0][:, :LANES] + xs[i1][:, :LANES]
        u_acc = xs[i0][:, LANES:] + xs[i1][:, LANES:]
        q_acc = rt_ref[p] + ax[i0][:, :LANES] + ax[i1][:, :LANES]
        y_acc = ax[i0][:, LANES:] + ax[i1][:, LANES:] + ykv[i0] + ykv[i1]
        bt = _mm_tn(bc_ref[p], jnp.concatenate([w_acc, u_acc], axis=1))
        kt = _mm_tn(kc_ref[p], v_ref[p])
        m_mat = jnp.where(same_head, bt[:, :LANES], 0.0) + jnp.where(diag, pc_ref[p], 0.0)
        g_mat = jnp.where(same_head, bt[:, LANES:] + kt, 0.0)
        h = state_ref[p]
        y_ref[p] = _mm(q_acc, h) + y_acc
        state_ref[p] = _mm(m_mat, h) + g_mat

    y = jnp.concatenate([y_ref[p] for p in range(n_pairs)], axis=1)
    inv_n = 1.0 / HEAD_DIM
    mu = _mm_split(y, seg) * inv_n
    d = y - mu
    var = _mm_split(d * d, seg) * inv_n
    yn = d * lax.rsqrt(var + GN_EPS) * lnw_ref[...] + lnb_ref[...]
    out_ref[...] = (yn + bonus) * gate


def _rwkv(rkv, lora, prm, *, batch, seq):
    t, w3 = rkv.shape
    width = w3 // 3
    n_lora = lora.shape[1]
    ch = CHUNK
    n_chunks = seq // ch
    n_pairs = width // LANES
    row = lambda n: pl.BlockSpec((1, n), lambda b, c: (0, 0))
    full = lambda a: pl.BlockSpec(a.shape, lambda b, c: (0,) * a.ndim)
    head_id = jnp.arange(width) // HEAD_DIM
    seg = (head_id[:, None] == head_id[None, :]).astype(BF16)
    tri = (jnp.arange(ch)[:, None] >= jnp.arange(ch)[None, :]).astype(BF16)
    big = lambda: pltpu.VMEM((n_pairs, ch, LANES), F32)
    args = (rkv, lora, prm["mix_rkv"], prm["mix_lora"], prm["w0"], prm["w_up"], prm["a0"], prm["a_up"],
            prm["g_up"], prm["k_k"], prm["k_a"], prm["r_k"], prm["ln_w"], prm["ln_b"], seg, tri)
    in_specs = [
        pl.BlockSpec((ch, w3), lambda b, c: (b * n_chunks + c, 0)),
        pl.BlockSpec((ch, n_lora), lambda b, c: (b * n_chunks + c, 0)),
        row(w3), row(n_lora), row(width), full(prm["w_up"]), row(width), full(prm["a_up"]),
        full(prm["g_up"]), row(width), row(width), row(width), row(width), row(width),
        full(seg), full(tri),
    ]
    return pl.pallas_call(
        functools.partial(_rwkv_kernel, width=width),
        grid=(batch, n_chunks),
        in_specs=in_specs,
        out_specs=pl.BlockSpec((ch, width), lambda b, c: (b * n_chunks + c, 0)),
        out_shape=jax.ShapeDtypeStruct((t, width), F32),
        scratch_shapes=[
            pltpu.VMEM((n_pairs, LANES, LANES), F32),
            pltpu.VMEM((1, w3), F32),
            pltpu.VMEM((1, n_lora), F32),
            big(), big(), big(), big(), big(), big(), big(),
            pltpu.VMEM((n_pairs, 1, LANES), F32),
            big(),
        ],
        compiler_params=_params(("arbitrary", "arbitrary")),
        name="rwkv7",
    )(*args)


def _out_proj_kernel(attn_ref, rw_ref, w_ref, x_ref, out_ref, lhs_ref, *, width):
    @pl.when(pl.program_id(1) == 0)
    def _():
        lhs_ref[:, :width] = attn_ref[...].astype(BF16)
        lhs_ref[:, width:] = rw_ref[...].astype(BF16)

    out_ref[...] = x_ref[...] + jnp.dot(lhs_ref[...], w_ref[...], preferred_element_type=F32)


def _out_proj(attn, rw, w_out, x2, *, tm, tn):
    t, width = attn.shape
    d = w_out.shape[1]
    return pl.pallas_call(
        functools.partial(_out_proj_kernel, width=width),
        grid=(t // tm, d // tn),
        in_specs=[
            pl.BlockSpec((tm, width), lambda i, j: (i, 0)),
            pl.BlockSpec((tm, width), lambda i, j: (i, 0)),
            pl.BlockSpec((2 * width, tn), lambda i, j: (0, j)),
            pl.BlockSpec((tm, tn), lambda i, j: (i, j)),
        ],
        out_specs=pl.BlockSpec((tm, tn), lambda i, j: (i, j)),
        out_shape=jax.ShapeDtypeStruct((t, d), F32),
        scratch_shapes=[pltpu.VMEM((tm, 2 * width), BF16)],
        compiler_params=_params(("arbitrary", "arbitrary")),
        name="out_proj",
    )(attn, rw, w_out, x2)


def _ffn_up_kernel(h_ref, g_ref, wg_ref, wu_ref, act_ref, hn_ref):
    @pl.when(pl.program_id(1) == 0)
    def _():
        h = h_ref[...]
        ms = jnp.mean(h * h, axis=-1, keepdims=True)
        hn_ref[...] = (h * lax.rsqrt(ms + RMS_EPS) * g_ref[...]).astype(BF16)

    hn = hn_ref[...]
    gate = jnp.dot(hn, wg_ref[...], preferred_element_type=F32)
    up = jnp.dot(hn, wu_ref[...], preferred_element_type=F32)
    act_ref[...] = (gate * jax.nn.sigmoid(gate) * up).astype(BF16)


def _ffn_up(h, g, w_gate_up, *, tm, th):
    t, d = h.shape
    hidden = w_gate_up.shape[1] // 2
    n_h = hidden // th
    return pl.pallas_call(
        _ffn_up_kernel,
        grid=(t // tm, n_h),
        in_specs=[
            pl.BlockSpec((tm, d), lambda i, j: (i, 0)),
            pl.BlockSpec((1, d), lambda i, j: (0, 0)),
            pl.BlockSpec((d, th), lambda i, j: (0, j)),
            pl.BlockSpec((d, th), lambda i, j: (0, n_h + j)),
        ],
        out_specs=pl.BlockSpec((tm, th), lambda i, j: (i, j)),
        out_shape=jax.ShapeDtypeStruct((t, hidden), BF16),
        scratch_shapes=[pltpu.VMEM((tm, d), BF16)],
        compiler_params=_params(("arbitrary", "arbitrary")),
        name="ffn_up",
    )(h, g, w_gate_up, w_gate_up)


def _ffn_down_kernel(act_ref, w_ref, h_ref, g_ref, out_ref, res_ref, ssq_ref, *, d):
    j = pl.program_id(1)
    n_j, _, tn = res_ref.shape
    res = h_ref[...] + jnp.dot(act_ref[...], w_ref[...], preferred_element_type=F32)
    res_ref[j] = res
    part = jnp.sum(res * res, axis=-1, keepdims=True)

    @pl.when(j == 0)
    def _():
        ssq_ref[...] = part

    @pl.when(j > 0)
    def _():
        ssq_ref[...] += part

    @pl.when(j == n_j - 1)
    def _():
        scale = lax.rsqrt(ssq_ref[...] * (1.0 / d) + RMS_EPS)
        for jj in range(n_j):
            cols = slice(jj * tn, (jj + 1) * tn)
            out_ref[:, cols] = res_ref[jj] * scale * g_ref[:, cols]


def _ffn_down(act, w_down, h, g, *, tm, tn):
    t, hidden = act.shape
    d = w_down.shape[1]
    return pl.pallas_call(
        functools.partial(_ffn_down_kernel, d=d),
        grid=(t // tm, d // tn),
        in_specs=[
            pl.BlockSpec((tm, hidden), lambda i, j: (i, 0)),
            pl.BlockSpec((hidden, tn), lambda i, j: (0, j)),
            pl.BlockSpec((tm, tn), lambda i, j: (i, j)),
            pl.BlockSpec((1, d), lambda i, j: (0, 0)),
        ],
        out_specs=pl.BlockSpec((tm, d), lambda i, j: (i, 0)),
        out_shape=jax.ShapeDtypeStruct((t, d), F32),
        scratch_shapes=[pltpu.VMEM((d // tn, tm, tn), F32), pltpu.VMEM((tm, 1), F32)],
        compiler_params=_params(("arbitrary", "arbitrary")),
        name="ffn_down",
    )(act, w_down, h, g)


def _pad_rows(a, rows):
    return jnp.pad(a, ((0, rows - a.shape[0]), (0, 0)))


def _layer(h, norm1_g, w_in, shift_mix, w0, w_up, a0, a_up, g_up, k_k, k_a, r_k, ln_w, ln_b,
           rel_bias_table, w_out, norm2_g, w_gate_up, w_down, out_g, *, batch, seq, tile_m):
    d = h.shape[1]
    width = w_out.shape[0] // 2
    n_heads = width // HEAD_DIM
    n_main = 6 * width
    rank_w, rank_a, rank_g = w_up.shape[0], a_up.shape[0], g_up.shape[0]
    n_lora = rank_w + rank_a + rank_g
    n_lora_pad = -(-n_lora // LANES) * LANES

    w_main = w_in[:, :n_main].astype(BF16)
    w_lora = jnp.pad(w_in[:, n_main:], ((0, 0), (0, n_lora_pad - n_lora))).astype(BF16)
    qkv, rkv, lora = _in_proj(h, norm1_g.reshape(1, d), w_main, w_lora, tm=tile_m, tn=3 * width // 4)

    attn = _attention(qkv, _band_rows(rel_bias_table), batch=batch, seq=seq, n_heads=n_heads)

    mix_rkv = shift_mix[:3 * width].reshape(1, 3 * width)
    mix_lora = jnp.pad(shift_mix[3 * width:], (0, n_lora_pad - n_lora)).reshape(1, n_lora_pad)
    zeros = lambda n: jnp.zeros((n, width), F32)
    prm = dict(
        mix_rkv=mix_rkv, mix_lora=mix_lora,
        w0=w0.reshape(1, width), a0=a0.reshape(1, width),
        w_up=_pad_rows(w_up, n_lora_pad).astype(BF16),
        a_up=_pad_rows(jnp.concatenate([zeros(rank_w), a_up]), n_lora_pad).astype(BF16),
        g_up=_pad_rows(jnp.concatenate([zeros(rank_w + rank_a), g_up]), n_lora_pad).astype(BF16),
        k_k=k_k.reshape(1, width), k_a=k_a.reshape(1, width), r_k=r_k.reshape(1, width),
        ln_w=ln_w.reshape(1, width), ln_b=ln_b.reshape(1, width),
    )
    rw = _rwkv(rkv, lora, prm, batch=batch, seq=seq)

    h1 = _out_proj(attn, rw, w_out.astype(BF16), h, tm=tile_m, tn=d // 2)
    act = _ffn_up(h1, norm2_g.reshape(1, d), w_gate_up.astype(BF16), tm=tile_m, th=512)
    return _ffn_down(act, w_down.astype(BF16), h1, out_g.reshape(1, d), tm=tile_m // 2, tn=512)


def kernel(x, norm1_g, w_in, rwkv_shift_mix, rwkv_w0, rwkv_w_up, rwkv_a0, rwkv_a_up, rwkv_g_up, rwkv_k_k, rwkv_k_a, rwkv_r_k, rwkv_ln_w, rwkv_ln_b, rel_bias_table, w_out, norm2_g, w_gate_up, w_down, final_g):
    batch, seq, d = x.shape
    depth = w_in.shape[0]
    assert depth == 1, "the final RMSNorm is fused into the last layer's FFN kernel"
    assert seq % DILATED_PATTERNS[-1][0] == 0 and seq % CHUNK == 0
    h = x.reshape(batch * seq, d)
    out = _layer(h, norm1_g[0], w_in[0], rwkv_shift_mix[0], rwkv_w0[0], rwkv_w_up[0], rwkv_a0[0],
                 rwkv_a_up[0], rwkv_g_up[0], rwkv_k_k[0], rwkv_k_a[0], rwkv_r_k[0], rwkv_ln_w[0],
                 rwkv_ln_b[0], rel_bias_table, w_out[0], norm2_g[0], w_gate_up[0], w_down[0], final_g,
                 batch=batch, seq=seq, tile_m=min(1024, batch * seq))
    return out.reshape(batch, seq, d)
```

```python
import functools
import math

import jax
import jax.numpy as jnp
from jax import lax
from jax.experimental import pallas as pl
from jax.experimental.pallas import tpu as pltpu

F32 = jnp.float32
BF16 = jnp.bfloat16

HEAD_DIM = 64
LANES = 128
DILATED_PATTERNS = ((128, 1), (512, 4), (2048, 16))
REL_BUCKETS = 32
REL_MAX_DIST = 2048
RMS_EPS = 1e-6
GN_EPS = 64e-5
DECAY_SCALE = math.exp(-0.5)
ATTN_SCALE = HEAD_DIM ** -0.5
MASK_VALUE = -1e30
CHUNK = 64
SEG_LANES = 256
VMEM_LIMIT = 56 * 1024 * 1024


def _mm(a, b):
    return jnp.dot(a.astype(BF16), b.astype(BF16), preferred_element_type=F32)


def _mm_nt(a, b):
    return lax.dot_general(a.astype(BF16), b.astype(BF16), (((1,), (1,)), ((), ())),
                           preferred_element_type=F32)


def _mm_tn(a, b):
    return lax.dot_general(a.astype(BF16), b.astype(BF16), (((0,), (0,)), ((), ())),
                           preferred_element_type=F32)


def _mm_split(a, b_exact):
    hi = a.astype(BF16)
    lo = (a - hi.astype(F32)).astype(BF16)
    return (jnp.dot(hi, b_exact, preferred_element_type=F32)
            + jnp.dot(lo, b_exact, preferred_element_type=F32))


def _params(semantics):
    return pltpu.CompilerParams(dimension_semantics=semantics, vmem_limit_bytes=VMEM_LIMIT)


def _in_proj_kernel(x_ref, g_ref, w_ref, wl_ref, qkv_ref, rkv_ref, lora_ref, xn_ref, *, n_qkv):
    j = pl.program_id(1)

    @pl.when(j == 0)
    def _():
        x = x_ref[...]
        ms = jnp.mean(x * x, axis=-1, keepdims=True)
        xn_ref[...] = (x * lax.rsqrt(ms + RMS_EPS) * g_ref[...]).astype(BF16)

    @pl.when(j < n_qkv)
    def _():
        qkv_ref[...] = jnp.dot(xn_ref[...], w_ref[...], preferred_element_type=F32)

    @pl.when((j >= n_qkv) & (j < 2 * n_qkv))
    def _():
        rkv_ref[...] = jnp.dot(xn_ref[...], w_ref[...], preferred_element_type=F32)

    @pl.when(j == 2 * n_qkv)
    def _():
        lora_ref[...] = jnp.dot(xn_ref[...], wl_ref[...], preferred_element_type=F32)


def _in_proj(x2, g, w_main, w_lora, *, tm, tn):
    t, d = x2.shape
    n_main = w_main.shape[1]
    n_lora = w_lora.shape[1]
    n_qkv = (n_main // 2) // tn
    last = 2 * n_qkv - 1
    return pl.pallas_call(
        functools.partial(_in_proj_kernel, n_qkv=n_qkv),
        grid=(t // tm, 2 * n_qkv + 1),
        in_specs=[
            pl.BlockSpec((tm, d), lambda i, j: (i, 0)),
            pl.BlockSpec((1, d), lambda i, j: (0, 0)),
            pl.BlockSpec((d, tn), lambda i, j: (0, jnp.minimum(j, last))),
            pl.BlockSpec((d, n_lora), lambda i, j: (0, 0)),
        ],
        out_specs=[
            pl.BlockSpec((tm, tn), lambda i, j: (i, jnp.minimum(j, n_qkv - 1))),
            pl.BlockSpec((tm, tn), lambda i, j: (i, jnp.clip(j - n_qkv, 0, n_qkv - 1))),
            pl.BlockSpec((tm, n_lora), lambda i, j: (i, 0)),
        ],
        out_shape=[
            jax.ShapeDtypeStruct((t, n_main // 2), F32),
            jax.ShapeDtypeStruct((t, n_main // 2), F32),
            jax.ShapeDtypeStruct((t, n_lora), F32),
        ],
        scratch_shapes=[pltpu.VMEM((tm, d), BF16)],
        compiler_params=_params(("arbitrary", "arbitrary")),
        name="in_proj",
    )(x2, g, w_main, w_lora)


ATTN_BLK = 128
ATTN_ROWS = max(w for w, _ in DILATED_PATTERNS)
ATTN_GROUP = 4


def _attn_kernel(q_ref, kp_ref, kc_ref, vp_ref, vc_ref, band_ref, out_ref, bias_ref, o_ref, lse_ref,
                 *, n_heads):
    b, n, hp = pl.program_id(0), pl.program_id(1), pl.program_id(2)
    blk = ATTN_BLK

    @pl.when((b == 0) & (n == 0) & (hp == 0))
    def _():
        for i in range(bias_ref.shape[0]):
            row = jnp.broadcast_to(band_ref[i:i + 1, :], (blk, 2 * blk))
            bias_ref[i] = pltpu.roll(row, 0, 1, stride=1, stride_axis=0)

    lane = lax.broadcasted_iota(jnp.int32, (1, LANES), 1)
    in_head = [(lane >= half * HEAD_DIM) & (lane < (half + 1) * HEAD_DIM) for half in range(2)]
    key_idx = lax.broadcasted_iota(jnp.int32, (1, 2 * blk), 1)
    first_ok = (key_idx >= blk) | (n > 0)

    for bi, (window, dilation) in enumerate(DILATED_PATTERNS):
        units = [(sub, r) for sub in range(ATTN_ROWS // window) for r in range(dilation)]
        for g0 in range(0, len(units), ATTN_GROUP):
            group = units[g0:g0 + ATTN_GROUP]
            rows, q2, k2, v2, key_ok = [], [], [], [], []
            for sub, r in group:
                cur = pl.ds(sub * window + r, blk, stride=dilation)
                if sub == 0:
                    prev_k, prev_v = kp_ref, vp_ref
                    prev = pl.ds(ATTN_ROWS - window + r, blk, stride=dilation)
                else:
                    prev_k, prev_v = kc_ref, vc_ref
                    prev = pl.ds((sub - 1) * window + r, blk, stride=dilation)
                rows.append(cur)
                q2.append(q_ref[cur, :] * ATTN_SCALE)
                k2.append(jnp.concatenate([prev_k[prev, :], kc_ref[cur, :]], axis=0).astype(BF16))
                v2.append(jnp.concatenate([prev_v[prev, :], vc_ref[cur, :]], axis=0).astype(BF16))
                key_ok.append(first_ok if sub == 0 else None)
            s = [[_mm_nt(jnp.where(m, q, 0.0), k) for m in in_head] for q, k in zip(q2, k2)]
            outs = []
            for u in range(len(group)):
                halves = []
                for half in range(2):
                    su = s[u][half] + bias_ref[(bi * n_heads) + 2 * hp + half]
                    if key_ok[u] is not None:
                        su = jnp.where(key_ok[u], su, MASK_VALUE)
                    mx = jnp.max(su, axis=-1, keepdims=True)
                    p = jnp.exp(su - mx)
                    halves.append((p, mx, jnp.sum(p, axis=-1, keepdims=True)))
                outs.append(halves)
            for u in range(len(group)):
                (pa, ma, la), (pb, mb, lb) = outs[u]
                oa = jnp.dot(pa.astype(BF16), v2[u], preferred_element_type=F32) / la
                ob = jnp.dot(pb.astype(BF16), v2[u], preferred_element_type=F32) / lb
                o_ref[bi, rows[u], :] = jnp.where(in_head[0], oa, ob)
                lse_ref[bi, rows[u], :] = jnp.where(in_head[0], ma + jnp.log(la), mb + jnp.log(lb))

    lses = [lse_ref[bi] for bi in range(len(DILATED_PATTERNS))]
    top = functools.reduce(jnp.maximum, lses)
    es = [jnp.exp(l - top) for l in lses]
    num = functools.reduce(jnp.add, [e * o_ref[bi] for bi, e in enumerate(es)])
    out_ref[...] = num / functools.reduce(jnp.add, es)


def _attention(qkv, band, *, batch, seq, n_heads):
    width = n_heads * HEAD_DIM
    n_pairs = width // LANES
    nb = seq // ATTN_ROWS
    n_pat = len(DILATED_PATTERNS)
    spec = lambda f: pl.BlockSpec((ATTN_ROWS, LANES), f)
    prev = lambda b, n: b * nb + jnp.maximum(n - 1, 0)
    return pl.pallas_call(
        functools.partial(_attn_kernel, n_heads=n_heads),
        grid=(batch, nb, n_pairs),
        in_specs=[
            spec(lambda b, n, hp: (b * nb + n, hp)),
            spec(lambda b, n, hp: (prev(b, n), n_pairs + hp)),
            spec(lambda b, n, hp: (b * nb + n, n_pairs + hp)),
            spec(lambda b, n, hp: (prev(b, n), 2 * n_pairs + hp)),
            spec(lambda b, n, hp: (b * nb + n, 2 * n_pairs + hp)),
            pl.BlockSpec(band.shape, lambda b, n, hp: (0, 0)),
        ],
        out_specs=spec(lambda b, n, hp: (b * nb + n, hp)),
        out_shape=jax.ShapeDtypeStruct((batch * seq, width), F32),
        scratch_shapes=[
            pltpu.VMEM((n_pat * n_heads, ATTN_BLK, 2 * ATTN_BLK), F32),
            pltpu.VMEM((n_pat, ATTN_ROWS, LANES), F32),
            pltpu.VMEM((n_pat, ATTN_ROWS, LANES), F32),
        ],
        compiler_params=_params(("arbitrary", "arbitrary", "arbitrary")),
        name="dilated_attn",
    )(qkv, qkv, qkv, qkv, qkv, band)


def _t5_bucket(dist):
    exact = REL_BUCKETS // 2
    d_f = jnp.maximum(dist, 1).astype(F32)
    large = exact + (jnp.log(d_f / exact) / math.log(REL_MAX_DIST / exact)
                     * (REL_BUCKETS - exact)).astype(jnp.int32)
    large = jnp.minimum(large, REL_BUCKETS - 1)
    return jnp.where(dist < exact, dist, large)


def _band_rows(bias_table):
    blk = ATTN_BLK
    rel = blk - jnp.arange(2 * blk)
    band = (rel >= 0) & (rel <= blk)
    rows = []
    for _, dilation in DILATED_PATTERNS:
        bias = bias_table[_t5_bucket(jnp.clip(rel, 0, blk) * dilation)]
        rows.append(jnp.where(band[:, None], bias.astype(F32), MASK_VALUE).T)
    return jnp.concatenate(rows, axis=0)


def _rwkv_kernel(rkv_ref, lora_ref, mix_rkv_ref, mix_lora_ref, w0_ref, w_up_ref, a0_ref, a_up_ref,
                 g_up_ref, kk_ref, ka_ref, rk_ref, lnw_ref, lnb_ref, seg_ref, tri_ref,
                 out_ref,
                 state_ref, carry_rkv_ref, carry_lora_ref,
                 rt_ref, at_ref, bh_ref, kh_ref, bc_ref, kc_ref, v_ref, pc_ref, y_ref, *, width):
    c = pl.program_id(1)
    n_pairs = width // LANES
    ch = rkv_ref.shape[0]

    @pl.when(c == 0)
    def _():
        state_ref[...] = jnp.zeros_like(state_ref)
        carry_rkv_ref[...] = jnp.zeros_like(carry_rkv_ref)
        carry_lora_ref[...] = jnp.zeros_like(carry_lora_ref)

    row = lax.broadcasted_iota(jnp.int32, (ch, 1), 0)

    def token_shift(z, carry_ref, mix):
        prev = jnp.where(row == 0, carry_ref[...], pltpu.roll(z, 1, axis=0))
        carry_ref[...] = z[ch - 1:ch, :]
        return z + (prev - z) * mix

    z = token_shift(rkv_ref[...], carry_rkv_ref, mix_rkv_ref[...])
    zl = token_shift(lora_ref[...], carry_lora_ref, mix_lora_ref[...])
    r, k, v = z[:, :width], z[:, width:2 * width], z[:, 2 * width:]

    lw = -DECAY_SCALE * jax.nn.sigmoid(w0_ref[...] + _mm(jnp.tanh(zl), w_up_ref[...]))
    a_sig = jax.nn.sigmoid(a0_ref[...] + _mm(zl, a_up_ref[...]))
    gate = _mm(jax.nn.sigmoid(zl), g_up_ref[...])

    seg = seg_ref[...]

    def head_sums(x):
        tiles = [_mm_split(x[:, t:t + SEG_LANES], seg) for t in range(0, width, SEG_LANES)]
        return jnp.concatenate(tiles, axis=1)

    kk = k * kk_ref[...]
    kk = kk / jnp.maximum(jnp.sqrt(head_sums(kk * kk)), 1e-12)
    k = k * (1.0 + (a_sig - 1.0) * ka_ref[...])
    bonus = head_sums(r * k * rk_ref[...]) * v
    a_in = -kk
    b_in = kk * a_sig

    lw_hi = lw.astype(BF16)
    lw_lo = (lw - lw_hi.astype(F32)).astype(BF16)
    tri = tri_ref[...]
    cum = (jnp.dot(tri, lw_hi, preferred_element_type=F32)
           + jnp.dot(tri, lw_lo, preferred_element_type=F32))
    cum_end = cum[ch - 1:ch, :]
    e_neg = jnp.exp(-cum)
    e_end = jnp.exp(cum_end - cum)

    def put(ref, val):
        for p in range(n_pairs):
            ref[p] = val[:, p * LANES:(p + 1) * LANES]

    put(rt_ref, r * jnp.exp(cum))
    put(at_ref, a_in * jnp.exp(cum - lw))
    put(bh_ref, b_in * e_neg)
    put(kh_ref, k * e_neg)
    put(bc_ref, b_in * e_end)
    put(kc_ref, k * e_end)
    put(v_ref, v)
    put(pc_ref, jnp.exp(cum_end))

    ti = lax.broadcasted_iota(jnp.int32, (ch, 2 * ch), 0)
    si = lax.broadcasted_iota(jnp.int32, (ch, 2 * ch), 1) % ch
    incl = ti >= si
    strict = ti > si
    lane = lax.broadcasted_iota(jnp.int32, (1, 2 * LANES), 1)
    head_a2 = (lane % LANES) < HEAD_DIM
    head_a = head_a2[:, :LANES]
    di = lax.broadcasted_iota(jnp.int32, (LANES, LANES), 0)
    dj = lax.broadcasted_iota(jnp.int32, (LANES, LANES), 1)
    same_head = (di < HEAD_DIM) == (dj < HEAD_DIM)
    diag = di == dj

    def stack(x, mask):
        return jnp.concatenate([jnp.where(mask, x, 0.0), jnp.where(mask, 0.0, x)], axis=0).astype(BF16)

    def block_diag(m):
        return jnp.where(same_head, jnp.concatenate([m, m], axis=0), 0.0).astype(BF16)

    pairs = range(n_pairs)
    a_rb, a_ab, a_rk, a_ak, vst = [], [], [], [], []
    for p in pairs:
        lhs = jnp.concatenate([rt_ref[p], at_ref[p]], axis=0)
        a_b = _mm_nt(lhs, stack(bh_ref[p], head_a))
        a_k = _mm_nt(lhs, stack(kh_ref[p], head_a))
        a_rb.append(jnp.where(incl, a_b[:ch], 0.0))
        a_ab.append(jnp.where(strict, a_b[ch:], 0.0))
        a_rk.append(jnp.where(incl, a_k[:ch], 0.0))
        a_ak.append(jnp.where(strict, a_k[ch:], 0.0))
        vst.append(stack(v_ref[p], head_a))

    xs = [jnp.concatenate([at_ref[p], _mm(a_ak[p], vst[p])], axis=1) for p in pairs]
    nk = a_ab
    xs = [x + _mm(n, stack(x, head_a2)) for n, x in zip(nk, xs)]
    for _ in range(int(math.log2(ch)) - 1):
        nk = [_mm(n, block_diag(n)) for n in nk]
        xs = [x + _mm(n, stack(x, head_a2)) for n, x in zip(nk, xs)]
    ax = [_mm(a, stack(x, head_a2)) for a, x in zip(a_rb, xs)]
    ykv = [_mm(a, vv) for a, vv in zip(a_rk, vst)]

    for p in pairs:
        w_acc = xs[p][:, :LANES]
        u_acc = xs[p][:, LANES:]
        q_acc = rt_ref[p] + ax[p][:, :LANES]
        y_acc = ax[p][:, LANES:] + ykv[p]
        bt = _mm_tn(bc_ref[p], xs[p])
        kt = _mm_tn(kc_ref[p], v_ref[p])
        m_mat = jnp.where(same_head, bt[:, :LANES], 0.0) + jnp.where(diag, pc_ref[p], 0.0)
        g_mat = jnp.where(same_head, bt[:, LANES:] + kt, 0.0)
        h = state_ref[p]
        y_ref[p] = _mm(q_acc, h) + y_acc
        state_ref[p] = _mm(m_mat, h) + g_mat

    y = jnp.concatenate([y_ref[p] for p in range(n_pairs)], axis=1)
    inv_n = 1.0 / HEAD_DIM
    mu = head_sums(y) * inv_n
    d = y - mu
    var = head_sums(d * d) * inv_n
    yn = d * lax.rsqrt(var + GN_EPS) * lnw_ref[...] + lnb_ref[...]
    out_ref[...] = (yn + bonus) * gate


def _rwkv(rkv, lora, prm, *, batch, seq):
    t, w3 = rkv.shape
    width = w3 // 3
    n_lora = lora.shape[1]
    ch = CHUNK
    n_chunks = seq // ch
    n_pairs = width // LANES
    row = lambda n: pl.BlockSpec((1, n), lambda b, c: (0, 0))
    full = lambda a: pl.BlockSpec(a.shape, lambda b, c: (0,) * a.ndim)
    head_id = jnp.arange(SEG_LANES) // HEAD_DIM
    seg = (head_id[:, None] == head_id[None, :]).astype(BF16)
    tri = (jnp.arange(ch)[:, None] >= jnp.arange(ch)[None, :]).astype(BF16)
    big = lambda: pltpu.VMEM((n_pairs, ch, LANES), F32)
    args = (rkv, lora, prm["mix_rkv"], prm["mix_lora"], prm["w0"], prm["w_up"], prm["a0"], prm["a_up"],
            prm["g_up"], prm["k_k"], prm["k_a"], prm["r_k"], prm["ln_w"], prm["ln_b"], seg, tri)
    in_specs = [
        pl.BlockSpec((ch, w3), lambda b, c: (b * n_chunks + c, 0)),
        pl.BlockSpec((ch, n_lora), lambda b, c: (b * n_chunks + c, 0)),
        row(w3), row(n_lora), row(width), full(prm["w_up"]), row(width), full(prm["a_up"]),
        full(prm["g_up"]), row(width), row(width), row(width), row(width), row(width),
        full(seg), full(tri),
    ]
    return pl.pallas_call(
        functools.partial(_rwkv_kernel, width=width),
        grid=(batch, n_chunks),
        in_specs=in_specs,
        out_specs=pl.BlockSpec((ch, width), lambda b, c: (b * n_chunks + c, 0)),
        out_shape=jax.ShapeDtypeStruct((t, width), F32),
        scratch_shapes=[
            pltpu.VMEM((n_pairs, LANES, LANES), F32),
            pltpu.VMEM((1, w3), F32),
            pltpu.VMEM((1, n_lora), F32),
            big(), big(), big(), big(), big(), big(), big(),
            pltpu.VMEM((n_pairs, 1, LANES), F32),
            big(),
        ],
        compiler_params=_params(("arbitrary", "arbitrary")),
        name="rwkv7",
    )(*args)


def _out_proj_kernel(attn_ref, rw_ref, w_ref, x_ref, out_ref, lhs_ref, *, width):
    @pl.when(pl.program_id(1) == 0)
    def _():
        lhs_ref[:, :width] = attn_ref[...].astype(BF16)
        lhs_ref[:, width:] = rw_ref[...].astype(BF16)

    out_ref[...] = x_ref[...] + jnp.dot(lhs_ref[...], w_ref[...], preferred_element_type=F32)


def _out_proj(attn, rw, w_out, x2, *, tm, tn):
    t, width = attn.shape
    d = w_out.shape[1]
    return pl.pallas_call(
        functools.partial(_out_proj_kernel, width=width),
        grid=(t // tm, d // tn),
        in_specs=[
            pl.BlockSpec((tm, width), lambda i, j: (i, 0)),
            pl.BlockSpec((tm, width), lambda i, j: (i, 0)),
            pl.BlockSpec((2 * width, tn), lambda i, j: (0, j)),
            pl.BlockSpec((tm, tn), lambda i, j: (i, j)),
        ],
        out_specs=pl.BlockSpec((tm, tn), lambda i, j: (i, j)),
        out_shape=jax.ShapeDtypeStruct((t, d), F32),
        scratch_shapes=[pltpu.VMEM((tm, 2 * width), BF16)],
        compiler_params=_params(("arbitrary", "arbitrary")),
        name="out_proj",
    )(attn, rw, w_out, x2)


def _ffn_up_kernel(h_ref, g_ref, wg_ref, wu_ref, act_ref, hn_ref):
    @pl.when(pl.program_id(1) == 0)
    def _():
        h = h_ref[...]
        ms = jnp.mean(h * h, axis=-1, keepdims=True)
        hn_ref[...] = (h * lax.rsqrt(ms + RMS_EPS) * g_ref[...]).astype(BF16)

    hn = hn_ref[...]
    gate = jnp.dot(hn, wg_ref[...], preferred_element_type=F32)
    up = jnp.dot(hn, wu_ref[...], preferred_element_type=F32)
    act_ref[...] = (gate * jax.nn.sigmoid(gate) * up).astype(BF16)


def _ffn_up(h, g, w_gate_up, *, tm, th):
    t, d = h.shape
    hidden = w_gate_up.shape[1] // 2
    n_h = hidden // th
    return pl.pallas_call(
        _ffn_up_kernel,
        grid=(t // tm, n_h),
        in_specs=[
            pl.BlockSpec((tm, d), lambda i, j: (i, 0)),
            pl.BlockSpec((1, d), lambda i, j: (0, 0)),
            pl.BlockSpec((d, th), lambda i, j: (0, j)),
            pl.BlockSpec((d, th), lambda i, j: (0, n_h + j)),
        ],
        out_specs=pl.BlockSpec((tm, th), lambda i, j: (i, j)),
        out_shape=jax.ShapeDtypeStruct((t, hidden), BF16),
        scratch_shapes=[pltpu.VMEM((tm, d), BF16)],
        compiler_params=_params(("arbitrary", "arbitrary")),
        name="ffn_up",
    )(h, g, w_gate_up, w_gate_up)


def _ffn_down_kernel(act_ref, w_ref, h_ref, g_ref, out_ref, res_ref, ssq_ref, *, d):
    j = pl.program_id(1)
    n_j, _, tn = res_ref.shape
    res = h_ref[...] + jnp.dot(act_ref[...], w_ref[...], preferred_element_type=F32)
    res_ref[j] = res
    part = jnp.sum(res * res, axis=-1, keepdims=True)

    @pl.when(j == 0)
    def _():
        ssq_ref[...] = part

    @pl.when(j > 0)
    def _():
        ssq_ref[...] += part

    @pl.when(j == n_j - 1)
    def _():
        scale = lax.rsqrt(ssq_ref[...] * (1.0 / d) + RMS_EPS)
        for jj in range(n_j):
            cols = slice(jj * tn, (jj + 1) * tn)
            out_ref[:, cols] = res_ref[jj] * scale * g_ref[:, cols]


def _ffn_down(act, w_down, h, g, *, tm, tn):
    t, hidden = act.shape
    d = w_down.shape[1]
    return pl.pallas_call(
        functools.partial(_ffn_down_kernel, d=d),
        grid=(t // tm, d // tn),
        in_specs=[
            pl.BlockSpec((tm, hidden), lambda i, j: (i, 0)),
            pl.BlockSpec((hidden, tn), lambda i, j: (0, j)),
            pl.BlockSpec((tm, tn), lambda i, j: (i, j)),
            pl.BlockSpec((1, d), lambda i, j: (0, 0)),
        ],
        out_specs=pl.BlockSpec((tm, d), lambda i, j: (i, 0)),
        out_shape=jax.ShapeDtypeStruct((t, d), F32),
        scratch_shapes=[pltpu.VMEM((d // tn, tm, tn), F32), pltpu.VMEM((tm, 1), F32)],
        compiler_params=_params(("arbitrary", "arbitrary")),
        name="ffn_down",
    )(act, w_down, h, g)


def _pad_rows(a, rows):
    return jnp.pad(a, ((0, rows - a.shape[0]), (0, 0)))


def _layer(h, norm1_g, w_in, shift_mix, w0, w_up, a0, a_up, g_up, k_k, k_a, r_k, ln_w, ln_b,
           rel_bias_table, w_out, norm2_g, w_gate_up, w_down, out_g, *, batch, seq, tile_m):
    d = h.shape[1]
    width = w_out.shape[0] // 2
    n_heads = width // HEAD_DIM
    n_main = 6 * width
    rank_w, rank_a, rank_g = w_up.shape[0], a_up.shape[0], g_up.shape[0]
    n_lora = rank_w + rank_a + rank_g
    n_lora_pad = -(-n_lora // LANES) * LANES

    w_main = w_in[:, :n_main].astype(BF16)
    w_lora = jnp.pad(w_in[:, n_main:], ((0, 0), (0, n_lora_pad - n_lora))).astype(BF16)
    qkv, rkv, lora = _in_proj(h, norm1_g.reshape(1, d), w_main, w_lora, tm=tile_m, tn=3 * width // 4)

    attn = _attention(qkv, _band_rows(rel_bias_table), batch=batch, seq=seq, n_heads=n_heads)

    mix_rkv = shift_mix[:3 * width].reshape(1, 3 * width)
    mix_lora = jnp.pad(shift_mix[3 * width:], (0, n_lora_pad - n_lora)).reshape(1, n_lora_pad)
    zeros = lambda n: jnp.zeros((n, width), F32)
    prm = dict(
        mix_rkv=mix_rkv, mix_lora=mix_lora,
        w0=w0.reshape(1, width), a0=a0.reshape(1, width),
        w_up=_pad_rows(w_up, n_lora_pad).astype(BF16),
        a_up=_pad_rows(jnp.concatenate([zeros(rank_w), a_up]), n_lora_pad).astype(BF16),
        g_up=_pad_rows(jnp.concatenate([zeros(rank_w + rank_a), g_up]), n_lora_pad).astype(BF16),
        k_k=k_k.reshape(1, width), k_a=k_a.reshape(1, width), r_k=r_k.reshape(1, width),
        ln_w=ln_w.reshape(1, width), ln_b=ln_b.reshape(1, width),
    )
    rw = _rwkv(rkv, lora, prm, batch=batch, seq=seq)

    h1 = _out_proj(attn, rw, w_out.astype(BF16), h, tm=tile_m, tn=d // 2)
    act = _ffn_up(h1, norm2_g.reshape(1, d), w_gate_up.astype(BF16), tm=tile_m, th=512)
    return _ffn_down(act, w_down.astype(BF16), h1, out_g.reshape(1, d), tm=tile_m // 2, tn=512)


def kernel(x, norm1_g, w_in, rwkv_shift_mix, rwkv_w0, rwkv_w_up, rwkv_a0, rwkv_a_up, rwkv_g_up, rwkv_k_k, rwkv_k_a, rwkv_r_k, rwkv_ln_w, rwkv_ln_b, rel_bias_table, w_out, norm2_g, w_gate_up, w_down, final_g):
    batch, seq, d = x.shape
    depth = w_in.shape[0]
    assert depth == 1, "the final RMSNorm is fused into the last layer's FFN kernel"
    assert seq % DILATED_PATTERNS[-1][0] == 0 and seq % CHUNK == 0
    h = x.reshape(batch * seq, d)
    out = _layer(h, norm1_g[0], w_in[0], rwkv_shift_mix[0], rwkv_w0[0], rwkv_w_up[0], rwkv_a0[0],
                 rwkv_a_up[0], rwkv_g_up[0], rwkv_k_k[0], rwkv_k_a[0], rwkv_r_k[0], rwkv_ln_w[0],
                 rwkv_ln_b[0], rel_bias_table, w_out[0], norm2_g[0], w_gate_up[0], w_down[0], final_g,
                 batch=batch, seq=seq, tile_m=min(1024, batch * seq))
    return out.reshape(batch, seq, d)
```

```python
import functools
import math

import jax
import jax.numpy as jnp
from jax import lax
from jax.experimental import pallas as pl
from jax.experimental.pallas import tpu as pltpu

F32 = jnp.float32
BF16 = jnp.bfloat16

HEAD_DIM = 64
LANES = 128
DILATED_PATTERNS = ((128, 1), (512, 4), (2048, 16))
REL_BUCKETS = 32
REL_MAX_DIST = 2048
RMS_EPS = 1e-6
GN_EPS = 64e-5
DECAY_SCALE = math.exp(-0.5)
ATTN_SCALE = HEAD_DIM ** -0.5
MASK_VALUE = -1e30
CHUNK = 64
SEG_LANES = 256
VMEM_LIMIT = 56 * 1024 * 1024


def _mm(a, b):
    return jnp.dot(a.astype(BF16), b.astype(BF16), preferred_element_type=F32)


def _mm_nt(a, b):
    return lax.dot_general(a.astype(BF16), b.astype(BF16), (((1,), (1,)), ((), ())),
                           preferred_element_type=F32)


def _mm_tn(a, b):
    return lax.dot_general(a.astype(BF16), b.astype(BF16), (((0,), (0,)), ((), ())),
                           preferred_element_type=F32)


def _mm_split(a, b_exact):
    hi = a.astype(BF16)
    lo = (a - hi.astype(F32)).astype(BF16)
    return (jnp.dot(hi, b_exact, preferred_element_type=F32)
            + jnp.dot(lo, b_exact, preferred_element_type=F32))


def _params(semantics):
    return pltpu.CompilerParams(dimension_semantics=semantics, vmem_limit_bytes=VMEM_LIMIT)


def _in_proj_kernel(x_ref, g_ref, w_ref, wl_ref, qkv_ref, rkv_ref, lora_ref, xn_ref, *, n_qkv):
    j = pl.program_id(1)

    @pl.when(j == 0)
    def _():
        x = x_ref[...]
        ms = jnp.mean(x * x, axis=-1, keepdims=True)
        xn_ref[...] = (x * lax.rsqrt(ms + RMS_EPS) * g_ref[...]).astype(BF16)

    @pl.when(j < n_qkv)
    def _():
        qkv_ref[...] = jnp.dot(xn_ref[...], w_ref[...], preferred_element_type=F32)

    @pl.when((j >= n_qkv) & (j < 2 * n_qkv))
    def _():
        rkv_ref[...] = jnp.dot(xn_ref[...], w_ref[...], preferred_element_type=F32)

    @pl.when(j == 2 * n_qkv)
    def _():
        lora_ref[...] = jnp.dot(xn_ref[...], wl_ref[...], preferred_element_type=F32)


def _in_proj(x2, g, w_main, w_lora, *, tm, tn):
    t, d = x2.shape
    n_main = w_main.shape[1]
    n_lora = w_lora.shape[1]
    n_qkv = (n_main // 2) // tn
    last = 2 * n_qkv - 1
    return pl.pallas_call(
        functools.partial(_in_proj_kernel, n_qkv=n_qkv),
        grid=(t // tm, 2 * n_qkv + 1),
        in_specs=[
            pl.BlockSpec((tm, d), lambda i, j: (i, 0)),
            pl.BlockSpec((1, d), lambda i, j: (0, 0)),
            pl.BlockSpec((d, tn), lambda i, j: (0, jnp.minimum(j, last))),
            pl.BlockSpec((d, n_lora), lambda i, j: (0, 0)),
        ],
        out_specs=[
            pl.BlockSpec((tm, tn), lambda i, j: (i, jnp.minimum(j, n_qkv - 1))),
            pl.BlockSpec((tm, tn), lambda i, j: (i, jnp.clip(j - n_qkv, 0, n_qkv - 1))),
            pl.BlockSpec((tm, n_lora), lambda i, j: (i, 0)),
        ],
        out_shape=[
            jax.ShapeDtypeStruct((t, n_main // 2), F32),
            jax.ShapeDtypeStruct((t, n_main // 2), F32),
            jax.ShapeDtypeStruct((t, n_lora), F32),
        ],
        scratch_shapes=[pltpu.VMEM((tm, d), BF16)],
        compiler_params=_params(("arbitrary", "arbitrary")),
        name="in_proj",
    )(x2, g, w_main, w_lora)


ATTN_BLK = 128
ATTN_ROWS = max(w for w, _ in DILATED_PATTERNS)
ATTN_GROUP = 4


def _attn_kernel(q_ref, kp_ref, kc_ref, vp_ref, vc_ref, band_ref, out_ref, bias_ref, o_ref, lse_ref,
                 *, n_heads):
    b, n, hp = pl.program_id(0), pl.program_id(1), pl.program_id(2)
    blk = ATTN_BLK

    @pl.when((b == 0) & (n == 0) & (hp == 0))
    def _():
        for i in range(bias_ref.shape[0]):
            row = jnp.broadcast_to(band_ref[i:i + 1, :], (blk, 2 * blk))
            bias_ref[i] = pltpu.roll(row, 0, 1, stride=1, stride_axis=0)

    lane = lax.broadcasted_iota(jnp.int32, (1, LANES), 1)
    in_head = [(lane >= half * HEAD_DIM) & (lane < (half + 1) * HEAD_DIM) for half in range(2)]
    key_idx = lax.broadcasted_iota(jnp.int32, (1, 2 * blk), 1)
    first_ok = (key_idx >= blk) | (n > 0)

    for bi, (window, dilation) in enumerate(DILATED_PATTERNS):
        units = [(sub, r) for sub in range(ATTN_ROWS // window) for r in range(dilation)]
        for g0 in range(0, len(units), ATTN_GROUP):
            group = units[g0:g0 + ATTN_GROUP]
            rows, q2, k2, v2, key_ok = [], [], [], [], []
            for sub, r in group:
                cur = pl.ds(sub * window + r, blk, stride=dilation)
                if sub == 0:
                    prev_k, prev_v = kp_ref, vp_ref
                    prev = pl.ds(ATTN_ROWS - window + r, blk, stride=dilation)
                else:
                    prev_k, prev_v = kc_ref, vc_ref
                    prev = pl.ds((sub - 1) * window + r, blk, stride=dilation)
                rows.append(cur)
                q2.append(q_ref[cur, :] * ATTN_SCALE)
                k2.append(jnp.concatenate([prev_k[prev, :], kc_ref[cur, :]], axis=0).astype(BF16))
                v2.append(jnp.concatenate([prev_v[prev, :], vc_ref[cur, :]], axis=0).astype(BF16))
                key_ok.append(first_ok if sub == 0 else None)
            s = [[_mm_nt(jnp.where(m, q, 0.0), k) for m in in_head] for q, k in zip(q2, k2)]
            outs = []
            for u in range(len(group)):
                halves = []
                for half in range(2):
                    su = s[u][half] + bias_ref[(bi * n_heads) + 2 * hp + half]
                    if key_ok[u] is not None:
                        su = jnp.where(key_ok[u], su, MASK_VALUE)
                    mx = jnp.max(su, axis=-1, keepdims=True)
                    p = jnp.exp(su - mx)
                    halves.append((p, mx, jnp.sum(p, axis=-1, keepdims=True)))
                outs.append(halves)
            for u in range(len(group)):
                (pa, ma, la), (pb, mb, lb) = outs[u]
                oa = jnp.dot(pa.astype(BF16), v2[u], preferred_element_type=F32) / la
                ob = jnp.dot(pb.astype(BF16), v2[u], preferred_element_type=F32) / lb
                o_ref[bi, rows[u], :] = jnp.where(in_head[0], oa, ob)
                lse_ref[bi, rows[u], :] = jnp.where(in_head[0], ma + jnp.log(la), mb + jnp.log(lb))

    lses = [lse_ref[bi] for bi in range(len(DILATED_PATTERNS))]
    top = functools.reduce(jnp.maximum, lses)
    es = [jnp.exp(l - top) for l in lses]
    num = functools.reduce(jnp.add, [e * o_ref[bi] for bi, e in enumerate(es)])
    out_ref[...] = num / functools.reduce(jnp.add, es)


def _attention(qkv, band, *, batch, seq, n_heads):
    width = n_heads * HEAD_DIM
    n_pairs = width // LANES
    nb = seq // ATTN_ROWS
    n_pat = len(DILATED_PATTERNS)
    spec = lambda f: pl.BlockSpec((ATTN_ROWS, LANES), f)
    prev = lambda b, n: b * nb + jnp.maximum(n - 1, 0)
    return pl.pallas_call(
        functools.partial(_attn_kernel, n_heads=n_heads),
        grid=(batch, nb, n_pairs),
        in_specs=[
            spec(lambda b, n, hp: (b * nb + n, hp)),
            spec(lambda b, n, hp: (prev(b, n), n_pairs + hp)),
            spec(lambda b, n, hp: (b * nb + n, n_pairs + hp)),
            spec(lambda b, n, hp: (prev(b, n), 2 * n_pairs + hp)),
            spec(lambda b, n, hp: (b * nb + n, 2 * n_pairs + hp)),
            pl.BlockSpec(band.shape, lambda b, n, hp: (0, 0)),
        ],
        out_specs=spec(lambda b, n, hp: (b * nb + n, hp)),
        out_shape=jax.ShapeDtypeStruct((batch * seq, width), F32),
        scratch_shapes=[
            pltpu.VMEM((n_pat * n_heads, ATTN_BLK, 2 * ATTN_BLK), F32),
            pltpu.VMEM((n_pat, ATTN_ROWS, LANES), F32),
            pltpu.VMEM((n_pat, ATTN_ROWS, LANES), F32),
        ],
        compiler_params=_params(("arbitrary", "arbitrary", "arbitrary")),
        name="dilated_attn",
    )(qkv, qkv, qkv, qkv, qkv, band)


def _t5_bucket(dist):
    exact = REL_BUCKETS // 2
    d_f = jnp.maximum(dist, 1).astype(F32)
    large = exact + (jnp.log(d_f / exact) / math.log(REL_MAX_DIST / exact)
                     * (REL_BUCKETS - exact)).astype(jnp.int32)
    large = jnp.minimum(large, REL_BUCKETS - 1)
    return jnp.where(dist < exact, dist, large)


def _band_rows(bias_table):
    blk = ATTN_BLK
    rel = blk - jnp.arange(2 * blk)
    band = (rel >= 0) & (rel <= blk)
    rows = []
    for _, dilation in DILATED_PATTERNS:
        bias = bias_table[_t5_bucket(jnp.clip(rel, 0, blk) * dilation)]
        rows.append(jnp.where(band[:, None], bias.astype(F32), MASK_VALUE).T)
    return jnp.concatenate(rows, axis=0)


def _rwkv_kernel(rkv_ref, lora_ref, mix_rkv_ref, mix_lora_ref, w0_ref, w_up_ref, a0_ref, a_up_ref,
                 g_up_ref, kk_ref, ka_ref, rk_ref, lnw_ref, lnb_ref, seg_ref, tri_ref,
                 out_ref,
                 state_ref, carry_rkv_ref, carry_lora_ref,
                 rt_ref, at_ref, bh_ref, kh_ref, bc_ref, kc_ref, v_ref, pc_ref, y_ref, *, width):
    c = pl.program_id(0)
    n_pairs = width // LANES
    n_seq, ch = rkv_ref.shape[:2]
    rows = n_seq * ch

    @pl.when(c == 0)
    def _():
        state_ref[...] = jnp.zeros_like(state_ref)
        carry_rkv_ref[...] = jnp.zeros_like(carry_rkv_ref)
        carry_lora_ref[...] = jnp.zeros_like(carry_lora_ref)

    row = lax.broadcasted_iota(jnp.int32, (rows, 1), 0) % ch

    def per_seq(row_of):
        return jnp.concatenate([jnp.broadcast_to(row_of(b), (ch, row_of(b).shape[-1]))
                                for b in range(n_seq)], axis=0)

    def token_shift(z_ref, carry_ref, mix):
        z = z_ref[...].reshape(rows, z_ref.shape[-1])
        prev = jnp.where(row == 0, per_seq(lambda b: carry_ref[b]), pltpu.roll(z, 1, axis=0))
        for b in range(n_seq):
            carry_ref[b] = z[(b + 1) * ch - 1:(b + 1) * ch, :]
        return z + (prev - z) * mix

    z = token_shift(rkv_ref, carry_rkv_ref, mix_rkv_ref[...])
    zl = token_shift(lora_ref, carry_lora_ref, mix_lora_ref[...])
    r, k, v = z[:, :width], z[:, width:2 * width], z[:, 2 * width:]

    lw = -DECAY_SCALE * jax.nn.sigmoid(w0_ref[...] + _mm(jnp.tanh(zl), w_up_ref[...]))
    a_sig = jax.nn.sigmoid(a0_ref[...] + _mm(zl, a_up_ref[...]))
    gate = _mm(jax.nn.sigmoid(zl), g_up_ref[...])

    seg = seg_ref[...]

    def head_sums(x):
        tiles = [_mm_split(x[:, t:t + SEG_LANES], seg) for t in range(0, width, SEG_LANES)]
        return jnp.concatenate(tiles, axis=1)

    kk = k * kk_ref[...]
    kk = kk / jnp.maximum(jnp.sqrt(head_sums(kk * kk)), 1e-12)
    k = k * (1.0 + (a_sig - 1.0) * ka_ref[...])
    bonus = head_sums(r * k * rk_ref[...]) * v
    a_in = -kk
    b_in = kk * a_sig

    lw_hi = lw.astype(BF16)
    lw_lo = (lw - lw_hi.astype(F32)).astype(BF16)
    tri = tri_ref[...]
    cum = (jnp.dot(tri, lw_hi, preferred_element_type=F32)
           + jnp.dot(tri, lw_lo, preferred_element_type=F32))
    cum_end = per_seq(lambda b: cum[(b + 1) * ch - 1:(b + 1) * ch, :])
    e_neg = jnp.exp(-cum)
    e_end = jnp.exp(cum_end - cum)

    n_units = n_seq * n_pairs

    def put(ref, val):
        for b in range(n_seq):
            for p in range(n_pairs):
                ref[b * n_pairs + p] = val[b * ch:b * ch + ref.shape[1], p * LANES:(p + 1) * LANES]

    put(rt_ref, r * jnp.exp(cum))
    put(at_ref, a_in * jnp.exp(cum - lw))
    put(bh_ref, b_in * e_neg)
    put(kh_ref, k * e_neg)
    put(bc_ref, b_in * e_end)
    put(kc_ref, k * e_end)
    put(v_ref, v)
    put(pc_ref, jnp.exp(cum_end))

    ti = lax.broadcasted_iota(jnp.int32, (ch, 2 * ch), 0)
    si = lax.broadcasted_iota(jnp.int32, (ch, 2 * ch), 1) % ch
    incl = ti >= si
    strict = ti > si
    lane = lax.broadcasted_iota(jnp.int32, (1, 2 * LANES), 1)
    head_a2 = (lane % LANES) < HEAD_DIM
    head_a = head_a2[:, :LANES]
    di = lax.broadcasted_iota(jnp.int32, (LANES, LANES), 0)
    dj = lax.broadcasted_iota(jnp.int32, (LANES, LANES), 1)
    same_head = (di < HEAD_DIM) == (dj < HEAD_DIM)
    diag = di == dj

    def stack(x, mask):
        return jnp.concatenate([jnp.where(mask, x, 0.0), jnp.where(mask, 0.0, x)], axis=0).astype(BF16)

    def block_diag(m):
        return jnp.where(same_head, jnp.concatenate([m, m], axis=0), 0.0).astype(BF16)

    pairs = range(n_units)
    a_rb, a_ab, a_rk, a_ak, vst = [], [], [], [], []
    for p in pairs:
        lhs = jnp.concatenate([rt_ref[p], at_ref[p]], axis=0)
        a_b = _mm_nt(lhs, stack(bh_ref[p], head_a))
        a_k = _mm_nt(lhs, stack(kh_ref[p], head_a))
        a_rb.append(jnp.where(incl, a_b[:ch], 0.0))
        a_ab.append(jnp.where(strict, a_b[ch:], 0.0))
        a_rk.append(jnp.where(incl, a_k[:ch], 0.0))
        a_ak.append(jnp.where(strict, a_k[ch:], 0.0))
        vst.append(stack(v_ref[p], head_a))

    xs = [jnp.concatenate([at_ref[p], _mm(a_ak[p], vst[p])], axis=1) for p in pairs]
    nk = a_ab
    xs = [x + _mm(n, stack(x, head_a2)) for n, x in zip(nk, xs)]
    for _ in range(int(math.log2(ch)) - 1):
        nk = [_mm(n, block_diag(n)) for n in nk]
        xs = [x + _mm(n, stack(x, head_a2)) for n, x in zip(nk, xs)]
    ax = [_mm(a, stack(x, head_a2)) for a, x in zip(a_rb, xs)]
    ykv = [_mm(a, vv) for a, vv in zip(a_rk, vst)]

    for p in pairs:
        w_acc = xs[p][:, :LANES]
        u_acc = xs[p][:, LANES:]
        q_acc = rt_ref[p] + ax[p][:, :LANES]
        y_acc = ax[p][:, LANES:] + ykv[p]
        bt = _mm_tn(bc_ref[p], xs[p])
        kt = _mm_tn(kc_ref[p], v_ref[p])
        m_mat = jnp.where(same_head, bt[:, :LANES], 0.0) + jnp.where(diag, pc_ref[p], 0.0)
        g_mat = jnp.where(same_head, bt[:, LANES:] + kt, 0.0)
        h = state_ref[p]
        y_ref[p] = _mm(q_acc, h) + y_acc
        state_ref[p] = _mm(m_mat, h) + g_mat

    y = jnp.concatenate([jnp.concatenate([y_ref[b * n_pairs + p] for p in range(n_pairs)], axis=1)
                         for b in range(n_seq)], axis=0)
    inv_n = 1.0 / HEAD_DIM
    mu = head_sums(y) * inv_n
    d = y - mu
    var = head_sums(d * d) * inv_n
    yn = d * lax.rsqrt(var + GN_EPS) * lnw_ref[...] + lnb_ref[...]
    out_ref[...] = ((yn + bonus) * gate).reshape(out_ref.shape)


def _rwkv(rkv, lora, prm, *, batch, seq):
    t, w3 = rkv.shape
    width = w3 // 3
    n_lora = lora.shape[1]
    ch = CHUNK
    n_chunks = seq // ch
    n_pairs = width // LANES
    row = lambda n: pl.BlockSpec((1, n), lambda c: (0, 0))
    full = lambda a: pl.BlockSpec(a.shape, lambda c: (0,) * a.ndim)
    head_id = jnp.arange(SEG_LANES) // HEAD_DIM
    seg = (head_id[:, None] == head_id[None, :]).astype(BF16)
    pos = jnp.arange(batch * ch)
    tri = ((pos[:, None] >= pos[None, :]) & (pos[:, None] // ch == pos[None, :] // ch)).astype(BF16)
    n_units = batch * n_pairs
    big = lambda: pltpu.VMEM((n_units, ch, LANES), F32)
    args = (rkv.reshape(batch, seq, w3), lora.reshape(batch, seq, n_lora), prm["mix_rkv"], prm["mix_lora"],
            prm["w0"], prm["w_up"], prm["a0"], prm["a_up"], prm["g_up"], prm["k_k"], prm["k_a"], prm["r_k"],
            prm["ln_w"], prm["ln_b"], seg, tri)
    in_specs = [
        pl.BlockSpec((batch, ch, w3), lambda c: (0, c, 0)),
        pl.BlockSpec((batch, ch, n_lora), lambda c: (0, c, 0)),
        row(w3), row(n_lora), row(width), full(prm["w_up"]), row(width), full(prm["a_up"]),
        full(prm["g_up"]), row(width), row(width), row(width), row(width), row(width),
        full(seg), full(tri),
    ]
    out = pl.pallas_call(
        functools.partial(_rwkv_kernel, width=width),
        grid=(n_chunks,),
        in_specs=in_specs,
        out_specs=pl.BlockSpec((batch, ch, width), lambda c: (0, c, 0)),
        out_shape=jax.ShapeDtypeStruct((batch, seq, width), F32),
        scratch_shapes=[
            pltpu.VMEM((n_units, LANES, LANES), F32),
            pltpu.VMEM((batch, 1, w3), F32),
            pltpu.VMEM((batch, 1, n_lora), F32),
            big(), big(), big(), big(), big(), big(), big(),
            pltpu.VMEM((n_units, 1, LANES), F32),
            big(),
        ],
        compiler_params=_params(("arbitrary",)),
        name="rwkv7",
    )(*args)
    return out.reshape(t, width)


def _out_proj_kernel(attn_ref, rw_ref, w_ref, x_ref, out_ref, lhs_ref, *, width):
    @pl.when(pl.program_id(1) == 0)
    def _():
        lhs_ref[:, :width] = attn_ref[...].astype(BF16)
        lhs_ref[:, width:] = rw_ref[...].astype(BF16)

    out_ref[...] = x_ref[...] + jnp.dot(lhs_ref[...], w_ref[...], preferred_element_type=F32)


def _out_proj(attn, rw, w_out, x2, *, tm, tn):
    t, width = attn.shape
    d = w_out.shape[1]
    return pl.pallas_call(
        functools.partial(_out_proj_kernel, width=width),
        grid=(t // tm, d // tn),
        in_specs=[
            pl.BlockSpec((tm, width), lambda i, j: (i, 0)),
            pl.BlockSpec((tm, width), lambda i, j: (i, 0)),
            pl.BlockSpec((2 * width, tn), lambda i, j: (0, j)),
            pl.BlockSpec((tm, tn), lambda i, j: (i, j)),
        ],
        out_specs=pl.BlockSpec((tm, tn), lambda i, j: (i, j)),
        out_shape=jax.ShapeDtypeStruct((t, d), F32),
        scratch_shapes=[pltpu.VMEM((tm, 2 * width), BF16)],
        compiler_params=_params(("arbitrary", "arbitrary")),
        name="out_proj",
    )(attn, rw, w_out, x2)


def _ffn_up_kernel(h_ref, g_ref, wg_ref, wu_ref, act_ref, hn_ref):
    @pl.when(pl.program_id(1) == 0)
    def _():
        h = h_ref[...]
        ms = jnp.mean(h * h, axis=-1, keepdims=True)
        hn_ref[...] = (h * lax.rsqrt(ms + RMS_EPS) * g_ref[...]).astype(BF16)

    hn = hn_ref[...]
    gate = jnp.dot(hn, wg_ref[...], preferred_element_type=F32)
    up = jnp.dot(hn, wu_ref[...], preferred_element_type=F32)
    act_ref[...] = (gate * jax.nn.sigmoid(gate) * up).astype(BF16)


def _ffn_up(h, g, w_gate_up, *, tm, th):
    t, d = h.shape
    hidden = w_gate_up.shape[1] // 2
    n_h = hidden // th
    return pl.pallas_call(
        _ffn_up_kernel,
        grid=(t // tm, n_h),
        in_specs=[
            pl.BlockSpec((tm, d), lambda i, j: (i, 0)),
            pl.BlockSpec((1, d), lambda i, j: (0, 0)),
            pl.BlockSpec((d, th), lambda i, j: (0, j)),
            pl.BlockSpec((d, th), lambda i, j: (0, n_h + j)),
        ],
        out_specs=pl.BlockSpec((tm, th), lambda i, j: (i, j)),
        out_shape=jax.ShapeDtypeStruct((t, hidden), BF16),
        scratch_shapes=[pltpu.VMEM((tm, d), BF16)],
        compiler_params=_params(("arbitrary", "arbitrary")),
        name="ffn_up",
    )(h, g, w_gate_up, w_gate_up)


def _ffn_down_kernel(act_ref, w_ref, h_ref, g_ref, out_ref, res_ref, ssq_ref, *, d):
    j = pl.program_id(1)
    n_j, _, tn = res_ref.shape
    res = h_ref[...] + jnp.dot(act_ref[...], w_ref[...], preferred_element_type=F32)
    res_ref[j] = res
    part = jnp.sum(res * res, axis=-1, keepdims=True)

    @pl.when(j == 0)
    def _():
        ssq_ref[...] = part

    @pl.when(j > 0)
    def _():
        ssq_ref[...] += part

    @pl.when(j == n_j - 1)
    def _():
        scale = lax.rsqrt(ssq_ref[...] * (1.0 / d) + RMS_EPS)
        for jj in range(n_j):
            cols = slice(jj * tn, (jj + 1) * tn)
            out_ref[:, cols] = res_ref[jj] * scale * g_ref[:, cols]


def _ffn_down(act, w_down, h, g, *, tm, tn):
    t, hidden = act.shape
    d = w_down.shape[1]
    return pl.pallas_call(
        functools.partial(_ffn_down_kernel, d=d),
        grid=(t // tm, d // tn),
        in_specs=[
            pl.BlockSpec((tm, hidden), lambda i, j: (i, 0)),
            pl.BlockSpec((hidden, tn), lambda i, j: (0, j)),
            pl.BlockSpec((tm, tn), lambda i, j: (i, j)),
            pl.BlockSpec((1, d), lambda i, j: (0, 0)),
        ],
        out_specs=pl.BlockSpec((tm, d), lambda i, j: (i, 0)),
        out_shape=jax.ShapeDtypeStruct((t, d), F32),
        scratch_shapes=[pltpu.VMEM((d // tn, tm, tn), F32), pltpu.VMEM((tm, 1), F32)],
        compiler_params=_params(("arbitrary", "arbitrary")),
        name="ffn_down",
    )(act, w_down, h, g)


def _pad_rows(a, rows):
    return jnp.pad(a, ((0, rows - a.shape[0]), (0, 0)))


def _layer(h, norm1_g, w_in, shift_mix, w0, w_up, a0, a_up, g_up, k_k, k_a, r_k, ln_w, ln_b,
           rel_bias_table, w_out, norm2_g, w_gate_up, w_down, out_g, *, batch, seq, tile_m):
    d = h.shape[1]
    width = w_out.shape[0] // 2
    n_heads = width // HEAD_DIM
    n_main = 6 * width
    rank_w, rank_a, rank_g = w_up.shape[0], a_up.shape[0], g_up.shape[0]
    n_lora = rank_w + rank_a + rank_g
    n_lora_pad = -(-n_lora // LANES) * LANES

    w_main = w_in[:, :n_main].astype(BF16)
    w_lora = jnp.pad(w_in[:, n_main:], ((0, 0), (0, n_lora_pad - n_lora))).astype(BF16)
    qkv, rkv, lora = _in_proj(h, norm1_g.reshape(1, d), w_main, w_lora, tm=tile_m, tn=3 * width // 4)

    attn = _attention(qkv, _band_rows(rel_bias_table), batch=batch, seq=seq, n_heads=n_heads)

    mix_rkv = shift_mix[:3 * width].reshape(1, 3 * width)
    mix_lora = jnp.pad(shift_mix[3 * width:], (0, n_lora_pad - n_lora)).reshape(1, n_lora_pad)
    zeros = lambda n: jnp.zeros((n, width), F32)
    prm = dict(
        mix_rkv=mix_rkv, mix_lora=mix_lora,
        w0=w0.reshape(1, width), a0=a0.reshape(1, width),
        w_up=_pad_rows(w_up, n_lora_pad).astype(BF16),
        a_up=_pad_rows(jnp.concatenate([zeros(rank_w), a_up]), n_lora_pad).astype(BF16),
        g_up=_pad_rows(jnp.concatenate([zeros(rank_w + rank_a), g_up]), n_lora_pad).astype(BF16),
        k_k=k_k.reshape(1, width), k_a=k_a.reshape(1, width), r_k=r_k.reshape(1, width),
        ln_w=ln_w.reshape(1, width), ln_b=ln_b.reshape(1, width),
    )
    rw = _rwkv(rkv, lora, prm, batch=batch, seq=seq)

    h1 = _out_proj(attn, rw, w_out.astype(BF16), h, tm=tile_m, tn=d // 2)
    act = _ffn_up(h1, norm2_g.reshape(1, d), w_gate_up.astype(BF16), tm=tile_m, th=512)
    return _ffn_down(act, w_down.astype(BF16), h1, out_g.reshape(1, d), tm=tile_m // 2, tn=512)


def kernel(x, norm1_g, w_in, rwkv_shift_mix, rwkv_w0, rwkv_w_up, rwkv_a0, rwkv_a_up, rwkv_g_up, rwkv_k_k, rwkv_k_a, rwkv_r_k, rwkv_ln_w, rwkv_ln_b, rel_bias_table, w_out, norm2_g, w_gate_up, w_down, final_g):
    batch, seq, d = x.shape
    depth = w_in.shape[0]
    assert depth == 1, "the final RMSNorm is fused into the last layer's FFN kernel"
    assert seq % DILATED_PATTERNS[-1][0] == 0 and seq % CHUNK == 0
    h = x.reshape(batch * seq, d)
    out = _layer(h, norm1_g[0], w_in[0], rwkv_shift_mix[0], rwkv_w0[0], rwkv_w_up[0], rwkv_a0[0],
                 rwkv_a_up[0], rwkv_g_up[0], rwkv_k_k[0], rwkv_k_a[0], rwkv_r_k[0], rwkv_ln_w[0],
                 rwkv_ln_b[0], rel_bias_table, w_out[0], norm2_g[0], w_gate_up[0], w_down[0], final_g,
                 batch=batch, seq=seq, tile_m=min(1024, batch * seq))
    return out.reshape(batch, seq, d)
```

```python
import functools
import math

import jax
import jax.numpy as jnp
from jax import lax
from jax.experimental import pallas as pl
from jax.experimental.pallas import tpu as pltpu

F32 = jnp.float32
BF16 = jnp.bfloat16

HEAD_DIM = 64
LANES = 128
DILATED_PATTERNS = ((128, 1), (512, 4), (2048, 16))
REL_BUCKETS = 32
REL_MAX_DIST = 2048
RMS_EPS = 1e-6
GN_EPS = 64e-5
DECAY_SCALE = math.exp(-0.5)
ATTN_SCALE = HEAD_DIM ** -0.5
MASK_VALUE = -1e30
CHUNK = 64
SEG_LANES = 256
VMEM_LIMIT = 56 * 1024 * 1024


def _mm(a, b):
    return jnp.dot(a.astype(BF16), b.astype(BF16), preferred_element_type=F32)


def _mm_nt(a, b):
    return lax.dot_general(a.astype(BF16), b.astype(BF16), (((1,), (1,)), ((), ())),
                           preferred_element_type=F32)


def _mm_tn(a, b):
    return lax.dot_general(a.astype(BF16), b.astype(BF16), (((0,), (0,)), ((), ())),
                           preferred_element_type=F32)


def _mm_split(a, b_exact):
    hi = a.astype(BF16)
    lo = (a - hi.astype(F32)).astype(BF16)
    return (jnp.dot(hi, b_exact, preferred_element_type=F32)
            + jnp.dot(lo, b_exact, preferred_element_type=F32))


def _params(semantics):
    return pltpu.CompilerParams(dimension_semantics=semantics, vmem_limit_bytes=VMEM_LIMIT)


def _in_proj_kernel(x_ref, g_ref, w_ref, wl_ref, qkv_ref, rkv_ref, lora_ref, xn_ref, *, n_qkv, n_lora):
    j = pl.program_id(1)

    @pl.when(j == 0)
    def _():
        x = x_ref[...]
        ms = jnp.mean(x * x, axis=-1, keepdims=True)
        xn_ref[...] = (x * lax.rsqrt(ms + RMS_EPS) * g_ref[...]).astype(BF16)

    @pl.when(j < n_qkv)
    def _():
        qkv_ref[...] = jnp.dot(xn_ref[...], w_ref[...], preferred_element_type=F32)

    @pl.when((j >= n_qkv) & (j < 2 * n_qkv))
    def _():
        rkv_ref[...] = jnp.dot(xn_ref[...], w_ref[...], preferred_element_type=F32)

    @pl.when(j == 2 * n_qkv)
    def _():
        col = lax.broadcasted_iota(jnp.int32, (1, wl_ref.shape[1]), 1)
        wl = jnp.where(col < n_lora, wl_ref[...], jnp.zeros((), BF16))
        lora_ref[...] = jnp.dot(xn_ref[...], wl, preferred_element_type=F32)


def _in_proj(x2, g, w, *, n_main, n_lora_pad, tm, tn):
    t, d = x2.shape
    n_lora = w.shape[1] - n_main
    n_qkv = (n_main // 2) // tn
    last = 2 * n_qkv - 1
    assert n_main % n_lora_pad == 0 and n_lora <= n_lora_pad
    return pl.pallas_call(
        functools.partial(_in_proj_kernel, n_qkv=n_qkv, n_lora=n_lora),
        grid=(t // tm, 2 * n_qkv + 1),
        in_specs=[
            pl.BlockSpec((tm, d), lambda i, j: (i, 0)),
            pl.BlockSpec((1, d), lambda i, j: (0, 0)),
            pl.BlockSpec((d, tn), lambda i, j: (0, jnp.minimum(j, last))),
            pl.BlockSpec((d, n_lora_pad), lambda i, j: (0, n_main // n_lora_pad)),
        ],
        out_specs=[
            pl.BlockSpec((tm, tn), lambda i, j: (i, jnp.minimum(j, n_qkv - 1))),
            pl.BlockSpec((tm, tn), lambda i, j: (i, jnp.clip(j - n_qkv, 0, n_qkv - 1))),
            pl.BlockSpec((tm, n_lora_pad), lambda i, j: (i, 0)),
        ],
        out_shape=[
            jax.ShapeDtypeStruct((t, n_main // 2), F32),
            jax.ShapeDtypeStruct((t, n_main // 2), F32),
            jax.ShapeDtypeStruct((t, n_lora_pad), F32),
        ],
        scratch_shapes=[pltpu.VMEM((tm, d), BF16)],
        compiler_params=_params(("arbitrary", "arbitrary")),
        name="in_proj",
    )(x2, g, w, w)


ATTN_BLK = 128
ATTN_ROWS = max(w for w, _ in DILATED_PATTERNS)
ATTN_GROUP = 4


def _attn_kernel(q_ref, kp_ref, kc_ref, vp_ref, vc_ref, band_ref, out_ref, bias_ref, o_ref, lse_ref,
                 *, n_heads):
    b, n, hp = pl.program_id(0), pl.program_id(1), pl.program_id(2)
    blk = ATTN_BLK

    @pl.when((b == 0) & (n == 0) & (hp == 0))
    def _():
        for i in range(bias_ref.shape[0]):
            row = jnp.broadcast_to(band_ref[i:i + 1, :], (blk, 2 * blk))
            bias_ref[i] = pltpu.roll(row, 0, 1, stride=1, stride_axis=0)

    lane = lax.broadcasted_iota(jnp.int32, (1, LANES), 1)
    in_head = [(lane >= half * HEAD_DIM) & (lane < (half + 1) * HEAD_DIM) for half in range(2)]
    key_idx = lax.broadcasted_iota(jnp.int32, (1, 2 * blk), 1)
    first_ok = (key_idx >= blk) | (n > 0)

    for bi, (window, dilation) in enumerate(DILATED_PATTERNS):
        units = [(sub, r) for sub in range(ATTN_ROWS // window) for r in range(dilation)]
        for g0 in range(0, len(units), ATTN_GROUP):
            group = units[g0:g0 + ATTN_GROUP]
            rows, q2, k2, v2, key_ok = [], [], [], [], []
            for sub, r in group:
                cur = pl.ds(sub * window + r, blk, stride=dilation)
                if sub == 0:
                    prev_k, prev_v = kp_ref, vp_ref
                    prev = pl.ds(ATTN_ROWS - window + r, blk, stride=dilation)
                else:
                    prev_k, prev_v = kc_ref, vc_ref
                    prev = pl.ds((sub - 1) * window + r, blk, stride=dilation)
                rows.append(cur)
                q2.append(q_ref[cur, :] * ATTN_SCALE)
                k2.append(jnp.concatenate([prev_k[prev, :], kc_ref[cur, :]], axis=0).astype(BF16))
                v2.append(jnp.concatenate([prev_v[prev, :], vc_ref[cur, :]], axis=0).astype(BF16))
                key_ok.append(first_ok if sub == 0 else None)
            s = [[_mm_nt(jnp.where(m, q, 0.0), k) for m in in_head] for q, k in zip(q2, k2)]
            outs = []
            for u in range(len(group)):
                halves = []
                for half in range(2):
                    su = s[u][half] + bias_ref[(bi * n_heads) + 2 * hp + half]
                    if key_ok[u] is not None:
                        su = jnp.where(key_ok[u], su, MASK_VALUE)
                    mx = jnp.max(su, axis=-1, keepdims=True)
                    p = jnp.exp(su - mx)
                    halves.append((p, mx, jnp.sum(p, axis=-1, keepdims=True)))
                outs.append(halves)
            for u in range(len(group)):
                (pa, ma, la), (pb, mb, lb) = outs[u]
                oa = jnp.dot(pa.astype(BF16), v2[u], preferred_element_type=F32) / la
                ob = jnp.dot(pb.astype(BF16), v2[u], preferred_element_type=F32) / lb
                o_ref[bi, rows[u], :] = jnp.where(in_head[0], oa, ob)
                lse_ref[bi, rows[u], :] = jnp.where(in_head[0], ma + jnp.log(la), mb + jnp.log(lb))

    lses = [lse_ref[bi] for bi in range(len(DILATED_PATTERNS))]
    top = functools.reduce(jnp.maximum, lses)
    es = [jnp.exp(l - top) for l in lses]
    num = functools.reduce(jnp.add, [e * o_ref[bi] for bi, e in enumerate(es)])
    out_ref[...] = num / functools.reduce(jnp.add, es)


def _attention(qkv, band, *, batch, seq, n_heads):
    width = n_heads * HEAD_DIM
    n_pairs = width // LANES
    nb = seq // ATTN_ROWS
    n_pat = len(DILATED_PATTERNS)
    spec = lambda f: pl.BlockSpec((ATTN_ROWS, LANES), f)
    prev = lambda b, n: b * nb + jnp.maximum(n - 1, 0)
    return pl.pallas_call(
        functools.partial(_attn_kernel, n_heads=n_heads),
        grid=(batch, nb, n_pairs),
        in_specs=[
            spec(lambda b, n, hp: (b * nb + n, hp)),
            spec(lambda b, n, hp: (prev(b, n), n_pairs + hp)),
            spec(lambda b, n, hp: (b * nb + n, n_pairs + hp)),
            spec(lambda b, n, hp: (prev(b, n), 2 * n_pairs + hp)),
            spec(lambda b, n, hp: (b * nb + n, 2 * n_pairs + hp)),
            pl.BlockSpec(band.shape, lambda b, n, hp: (0, 0)),
        ],
        out_specs=spec(lambda b, n, hp: (b * nb + n, hp)),
        out_shape=jax.ShapeDtypeStruct((batch * seq, width), F32),
        scratch_shapes=[
            pltpu.VMEM((n_pat * n_heads, ATTN_BLK, 2 * ATTN_BLK), F32),
            pltpu.VMEM((n_pat, ATTN_ROWS, LANES), F32),
            pltpu.VMEM((n_pat, ATTN_ROWS, LANES), F32),
        ],
        compiler_params=_params(("arbitrary", "arbitrary", "arbitrary")),
        name="dilated_attn",
    )(qkv, qkv, qkv, qkv, qkv, band)


def _t5_bucket(dist):
    exact = REL_BUCKETS // 2
    d_f = jnp.maximum(dist, 1).astype(F32)
    large = exact + (jnp.log(d_f / exact) / math.log(REL_MAX_DIST / exact)
                     * (REL_BUCKETS - exact)).astype(jnp.int32)
    large = jnp.minimum(large, REL_BUCKETS - 1)
    return jnp.where(dist < exact, dist, large)


def _band_rows(bias_table):
    blk = ATTN_BLK
    rel = blk - jnp.arange(2 * blk)
    band = (rel >= 0) & (rel <= blk)
    rows = []
    for _, dilation in DILATED_PATTERNS:
        bias = bias_table[_t5_bucket(jnp.clip(rel, 0, blk) * dilation)]
        rows.append(jnp.where(band[:, None], bias.astype(F32), MASK_VALUE).T)
    return jnp.concatenate(rows, axis=0)


def _rwkv_kernel(rkv_ref, lora_ref, mix_rkv_ref, mix_lora_ref, w0_ref, w_up_ref, a0_ref, a_up_ref,
                 g_up_ref, kk_ref, ka_ref, rk_ref, lnw_ref, lnb_ref, seg_ref, tri_ref,
                 out_ref,
                 state_ref, carry_rkv_ref, carry_lora_ref,
                 rt_ref, at_ref, bh_ref, kh_ref, bc_ref, kc_ref, v_ref, pc_ref, y_ref, *, width):
    c = pl.program_id(0)
    n_pairs = width // LANES
    n_seq, ch = rkv_ref.shape[:2]
    rows = n_seq * ch

    @pl.when(c == 0)
    def _():
        state_ref[...] = jnp.zeros_like(state_ref)
        carry_rkv_ref[...] = jnp.zeros_like(carry_rkv_ref)
        carry_lora_ref[...] = jnp.zeros_like(carry_lora_ref)

    row = lax.broadcasted_iota(jnp.int32, (rows, 1), 0) % ch

    def per_seq(row_of):
        return jnp.concatenate([jnp.broadcast_to(row_of(b), (ch, row_of(b).shape[-1]))
                                for b in range(n_seq)], axis=0)

    def token_shift(z_ref, carry_ref, mix):
        z = z_ref[...].reshape(rows, z_ref.shape[-1])
        prev = jnp.where(row == 0, per_seq(lambda b: carry_ref[b]), pltpu.roll(z, 1, axis=0))
        for b in range(n_seq):
            carry_ref[b] = z[(b + 1) * ch - 1:(b + 1) * ch, :]
        return z + (prev - z) * mix

    z = token_shift(rkv_ref, carry_rkv_ref, mix_rkv_ref[...])
    zl = token_shift(lora_ref, carry_lora_ref, mix_lora_ref[...])
    r, k, v = z[:, :width], z[:, width:2 * width], z[:, 2 * width:]

    lw = -DECAY_SCALE * jax.nn.sigmoid(w0_ref[...] + _mm(jnp.tanh(zl), w_up_ref[...]))
    a_sig = jax.nn.sigmoid(a0_ref[...] + _mm(zl, a_up_ref[...]))
    gate = _mm(jax.nn.sigmoid(zl), g_up_ref[...])

    seg = seg_ref[...]

    def head_sums(x):
        tiles = [_mm_split(x[:, t:t + SEG_LANES], seg) for t in range(0, width, SEG_LANES)]
        return jnp.concatenate(tiles, axis=1)

    kk = k * kk_ref[...]
    kk = kk / jnp.maximum(jnp.sqrt(head_sums(kk * kk)), 1e-12)
    k = k * (1.0 + (a_sig - 1.0) * ka_ref[...])
    bonus = head_sums(r * k * rk_ref[...]) * v
    a_in = -kk
    b_in = kk * a_sig

    lw_hi = lw.astype(BF16)
    lw_lo = (lw - lw_hi.astype(F32)).astype(BF16)
    tri = tri_ref[...]
    cum = (jnp.dot(tri, lw_hi, preferred_element_type=F32)
           + jnp.dot(tri, lw_lo, preferred_element_type=F32))
    cum_end = per_seq(lambda b: cum[(b + 1) * ch - 1:(b + 1) * ch, :])
    e_neg = jnp.exp(-cum)
    e_end = jnp.exp(cum_end - cum)

    n_units = n_seq * n_pairs

    def put(ref, val):
        for b in range(n_seq):
            for p in range(n_pairs):
                ref[b * n_pairs + p] = val[b * ch:b * ch + ref.shape[1], p * LANES:(p + 1) * LANES]

    put(rt_ref, r * jnp.exp(cum))
    put(at_ref, a_in * jnp.exp(cum - lw))
    put(bh_ref, b_in * e_neg)
    put(kh_ref, k * e_neg)
    put(bc_ref, b_in * e_end)
    put(kc_ref, k * e_end)
    put(v_ref, v)
    put(pc_ref, jnp.exp(cum_end))

    ti = lax.broadcasted_iota(jnp.int32, (ch, 2 * ch), 0)
    si = lax.broadcasted_iota(jnp.int32, (ch, 2 * ch), 1) % ch
    incl = ti >= si
    strict = ti > si
    lane = lax.broadcasted_iota(jnp.int32, (1, 2 * LANES), 1)
    head_a2 = (lane % LANES) < HEAD_DIM
    head_a = head_a2[:, :LANES]
    di = lax.broadcasted_iota(jnp.int32, (LANES, LANES), 0)
    dj = lax.broadcasted_iota(jnp.int32, (LANES, LANES), 1)
    same_head = (di < HEAD_DIM) == (dj < HEAD_DIM)
    diag = di == dj

    def stack(x, mask):
        return jnp.concatenate([jnp.where(mask, x, 0.0), jnp.where(mask, 0.0, x)], axis=0).astype(BF16)

    def block_diag(m):
        return jnp.where(same_head, jnp.concatenate([m, m], axis=0), 0.0).astype(BF16)

    pairs = range(n_units)
    zero_tile = jnp.zeros((2 * ch, LANES), BF16)
    a_rb, a_ab, a_rk, a_ak, vst = [], [], [], [], []
    for p in pairs:
        lhs = jnp.concatenate([rt_ref[p], at_ref[p]], axis=0)
        rhs = jnp.concatenate([stack(bh_ref[p], head_a), stack(kh_ref[p], head_a)], axis=0)
        a_bk = _mm_nt(lhs, rhs)
        a_rb.append(jnp.where(incl, a_bk[:ch, :LANES], 0.0))
        a_ab.append(jnp.where(strict, a_bk[ch:, :LANES], 0.0))
        a_rk.append(jnp.where(incl, a_bk[:ch, LANES:], 0.0))
        a_ak.append(jnp.where(strict, a_bk[ch:, LANES:], 0.0))
        vst.append(stack(v_ref[p], head_a))

    xs = [jnp.concatenate([at_ref[p], _mm(a_ak[p], vst[p])], axis=1) for p in pairs]
    nk = a_ab
    for _ in range(int(math.log2(ch)) - 1):
        nk_next = [_mm(n, block_diag(n)) for n in nk]
        xs = [x + _mm(n, stack(x, head_a2)) for n, x in zip(nk, xs)]
        nk = nk_next
    xs = [x + _mm(n, stack(x, head_a2)) for n, x in zip(nk, xs)]

    ax = [_mm(jnp.concatenate([a_rb[p], a_rk[p]], axis=1),
              jnp.concatenate([stack(xs[p], head_a2), jnp.concatenate([zero_tile, vst[p]], axis=1)], axis=0))
          for p in pairs]
    mg = [_mm_tn(jnp.concatenate([bc_ref[p], kc_ref[p]], axis=0),
                 jnp.concatenate([xs[p], jnp.concatenate([jnp.zeros((ch, LANES), F32), v_ref[p]], axis=1)],
                                 axis=0))
          for p in pairs]
    for p in pairs:
        q_acc = rt_ref[p] + ax[p][:, :LANES]
        m_mat = jnp.where(same_head, mg[p][:, :LANES], 0.0) + jnp.where(diag, pc_ref[p], 0.0)
        g_mat = jnp.where(same_head, mg[p][:, LANES:], 0.0)
        res = _mm(jnp.concatenate([q_acc, m_mat], axis=0), state_ref[p])
        y_ref[p] = res[:ch] + ax[p][:, LANES:]
        state_ref[p] = res[ch:] + g_mat

    y = jnp.concatenate([jnp.concatenate([y_ref[b * n_pairs + p] for p in range(n_pairs)], axis=1)
                         for b in range(n_seq)], axis=0)
    inv_n = 1.0 / HEAD_DIM
    mu = head_sums(y) * inv_n
    d = y - mu
    var = head_sums(d * d) * inv_n
    yn = d * lax.rsqrt(var + GN_EPS) * lnw_ref[...] + lnb_ref[...]
    out_ref[...] = ((yn + bonus) * gate).reshape(out_ref.shape)


def _rwkv(rkv, lora, prm, *, batch, seq):
    t, w3 = rkv.shape
    width = w3 // 3
    n_lora = lora.shape[1]
    ch = CHUNK
    n_chunks = seq // ch
    n_pairs = width // LANES
    row = lambda n: pl.BlockSpec((1, n), lambda c: (0, 0))
    full = lambda a: pl.BlockSpec(a.shape, lambda c: (0,) * a.ndim)
    head_id = jnp.arange(SEG_LANES) // HEAD_DIM
    seg = (head_id[:, None] == head_id[None, :]).astype(BF16)
    pos = jnp.arange(batch * ch)
    tri = ((pos[:, None] >= pos[None, :]) & (pos[:, None] // ch == pos[None, :] // ch)).astype(BF16)
    n_units = batch * n_pairs
    big = lambda: pltpu.VMEM((n_units, ch, LANES), F32)
    args = (rkv.reshape(batch, seq, w3), lora.reshape(batch, seq, n_lora), prm["mix_rkv"], prm["mix_lora"],
            prm["w0"], prm["w_up"], prm["a0"], prm["a_up"], prm["g_up"], prm["k_k"], prm["k_a"], prm["r_k"],
            prm["ln_w"], prm["ln_b"], seg, tri)
    in_specs = [
        pl.BlockSpec((batch, ch, w3), lambda c: (0, c, 0)),
        pl.BlockSpec((batch, ch, n_lora), lambda c: (0, c, 0)),
        row(w3), row(n_lora), row(width), full(prm["w_up"]), row(width), full(prm["a_up"]),
        full(prm["g_up"]), row(width), row(width), row(width), row(width), row(width),
        full(seg), full(tri),
    ]
    out = pl.pallas_call(
        functools.partial(_rwkv_kernel, width=width),
        grid=(n_chunks,),
        in_specs=in_specs,
        out_specs=pl.BlockSpec((batch, ch, width), lambda c: (0, c, 0)),
        out_shape=jax.ShapeDtypeStruct((batch, seq, width), F32),
        scratch_shapes=[
            pltpu.VMEM((n_units, LANES, LANES), F32),
            pltpu.VMEM((batch, 1, w3), F32),
            pltpu.VMEM((batch, 1, n_lora), F32),
            big(), big(), big(), big(), big(), big(), big(),
            pltpu.VMEM((n_units, 1, LANES), F32),
            big(),
        ],
        compiler_params=_params(("arbitrary",)),
        name="rwkv7",
    )(*args)
    return out.reshape(t, width)


def _out_proj_kernel(attn_ref, rw_ref, w_ref, x_ref, out_ref, lhs_ref, *, width):
    @pl.when(pl.program_id(1) == 0)
    def _():
        lhs_ref[:, :width] = attn_ref[...].astype(BF16)
        lhs_ref[:, width:] = rw_ref[...].astype(BF16)

    out_ref[...] = x_ref[...] + jnp.dot(lhs_ref[...], w_ref[...], preferred_element_type=F32)


def _out_proj(attn, rw, w_out, x2, *, tm, tn):
    t, width = attn.shape
    d = w_out.shape[1]
    return pl.pallas_call(
        functools.partial(_out_proj_kernel, width=width),
        grid=(t // tm, d // tn),
        in_specs=[
            pl.BlockSpec((tm, width), lambda i, j: (i, 0)),
            pl.BlockSpec((tm, width), lambda i, j: (i, 0)),
            pl.BlockSpec((2 * width, tn), lambda i, j: (0, j)),
            pl.BlockSpec((tm, tn), lambda i, j: (i, j)),
        ],
        out_specs=pl.BlockSpec((tm, tn), lambda i, j: (i, j)),
        out_shape=jax.ShapeDtypeStruct((t, d), F32),
        scratch_shapes=[pltpu.VMEM((tm, 2 * width), BF16)],
        compiler_params=_params(("arbitrary", "arbitrary")),
        name="out_proj",
    )(attn, rw, w_out, x2)


def _ffn_up_kernel(h_ref, g_ref, wg_ref, wu_ref, act_ref, hn_ref):
    @pl.when(pl.program_id(1) == 0)
    def _():
        h = h_ref[...]
        ms = jnp.mean(h * h, axis=-1, keepdims=True)
        hn_ref[...] = (h * lax.rsqrt(ms + RMS_EPS) * g_ref[...]).astype(BF16)

    hn = hn_ref[...]
    gate = jnp.dot(hn, wg_ref[...], preferred_element_type=F32)
    up = jnp.dot(hn, wu_ref[...], preferred_element_type=F32)
    act_ref[...] = (gate * jax.nn.sigmoid(gate) * up).astype(BF16)


def _ffn_up(h, g, w_gate_up, *, tm, th):
    t, d = h.shape
    hidden = w_gate_up.shape[1] // 2
    n_h = hidden // th
    return pl.pallas_call(
        _ffn_up_kernel,
        grid=(t // tm, n_h),
        in_specs=[
            pl.BlockSpec((tm, d), lambda i, j: (i, 0)),
            pl.BlockSpec((1, d), lambda i, j: (0, 0)),
            pl.BlockSpec((d, th), lambda i, j: (0, j)),
            pl.BlockSpec((d, th), lambda i, j: (0, n_h + j)),
        ],
        out_specs=pl.BlockSpec((tm, th), lambda i, j: (i, j)),
        out_shape=jax.ShapeDtypeStruct((t, hidden), BF16),
        scratch_shapes=[pltpu.VMEM((tm, d), BF16)],
        compiler_params=_params(("arbitrary", "arbitrary")),
        name="ffn_up",
    )(h, g, w_gate_up, w_gate_up)


def _ffn_down_kernel(act_ref, w_ref, h_ref, g_ref, out_ref, res_ref, ssq_ref, *, d):
    j = pl.program_id(1)
    n_j, _, tn = res_ref.shape
    res = h_ref[...] + jnp.dot(act_ref[...], w_ref[...], preferred_element_type=F32)
    res_ref[j] = res
    part = jnp.sum(res * res, axis=-1, keepdims=True)

    @pl.when(j == 0)
    def _():
        ssq_ref[...] = part

    @pl.when(j > 0)
    def _():
        ssq_ref[...] += part

    @pl.when(j == n_j - 1)
    def _():
        scale = lax.rsqrt(ssq_ref[...] * (1.0 / d) + RMS_EPS)
        for jj in range(n_j):
            cols = slice(jj * tn, (jj + 1) * tn)
            out_ref[:, cols] = res_ref[jj] * scale * g_ref[:, cols]


def _ffn_down(act, w_down, h, g, *, tm, tn):
    t, hidden = act.shape
    d = w_down.shape[1]
    return pl.pallas_call(
        functools.partial(_ffn_down_kernel, d=d),
        grid=(t // tm, d // tn),
        in_specs=[
            pl.BlockSpec((tm, hidden), lambda i, j: (i, 0)),
            pl.BlockSpec((hidden, tn), lambda i, j: (0, j)),
            pl.BlockSpec((tm, tn), lambda i, j: (i, j)),
            pl.BlockSpec((1, d), lambda i, j: (0, 0)),
        ],
        out_specs=pl.BlockSpec((tm, d), lambda i, j: (i, 0)),
        out_shape=jax.ShapeDtypeStruct((t, d), F32),
        scratch_shapes=[pltpu.VMEM((d // tn, tm, tn), F32), pltpu.VMEM((tm, 1), F32)],
        compiler_params=_params(("arbitrary", "arbitrary")),
        name="ffn_down",
    )(act, w_down, h, g)


def _pad_rows(a, rows):
    return jnp.pad(a, ((0, rows - a.shape[0]), (0, 0)))


def _layer(h, norm1_g, w_in, shift_mix, w0, w_up, a0, a_up, g_up, k_k, k_a, r_k, ln_w, ln_b,
           rel_bias_table, w_out, norm2_g, w_gate_up, w_down, out_g, *, batch, seq, tile_m):
    d = h.shape[1]
    width = w_out.shape[0] // 2
    n_heads = width // HEAD_DIM
    n_main = 6 * width
    rank_w, rank_a, rank_g = w_up.shape[0], a_up.shape[0], g_up.shape[0]
    n_lora = rank_w + rank_a + rank_g
    n_lora_pad = -(-n_lora // LANES) * LANES

    qkv, rkv, lora = _in_proj(h, norm1_g.reshape(1, d), w_in.astype(BF16), n_main=n_main,
                              n_lora_pad=n_lora_pad, tm=tile_m, tn=3 * width // 4)

    attn = _attention(qkv, _band_rows(rel_bias_table), batch=batch, seq=seq, n_heads=n_heads)

    mix_rkv = shift_mix[:3 * width].reshape(1, 3 * width)
    mix_lora = jnp.pad(shift_mix[3 * width:], (0, n_lora_pad - n_lora)).reshape(1, n_lora_pad)
    zeros = lambda n: jnp.zeros((n, width), F32)
    prm = dict(
        mix_rkv=mix_rkv, mix_lora=mix_lora,
        w0=w0.reshape(1, width), a0=a0.reshape(1, width),
        w_up=_pad_rows(w_up, n_lora_pad).astype(BF16),
        a_up=_pad_rows(jnp.concatenate([zeros(rank_w), a_up]), n_lora_pad).astype(BF16),
        g_up=_pad_rows(jnp.concatenate([zeros(rank_w + rank_a), g_up]), n_lora_pad).astype(BF16),
        k_k=k_k.reshape(1, width), k_a=k_a.reshape(1, width), r_k=r_k.reshape(1, width),
        ln_w=ln_w.reshape(1, width), ln_b=ln_b.reshape(1, width),
    )
    rw = _rwkv(rkv, lora, prm, batch=batch, seq=seq)

    h1 = _out_proj(attn, rw, w_out.astype(BF16), h, tm=tile_m, tn=d // 2)
    act = _ffn_up(h1, norm2_g.reshape(1, d), w_gate_up.astype(BF16), tm=tile_m, th=512)
    return _ffn_down(act, w_down.astype(BF16), h1, out_g.reshape(1, d), tm=tile_m // 2, tn=512)


def kernel(x, norm1_g, w_in, rwkv_shift_mix, rwkv_w0, rwkv_w_up, rwkv_a0, rwkv_a_up, rwkv_g_up, rwkv_k_k, rwkv_k_a, rwkv_r_k, rwkv_ln_w, rwkv_ln_b, rel_bias_table, w_out, norm2_g, w_gate_up, w_down, final_g):
    batch, seq, d = x.shape
    depth = w_in.shape[0]
    assert depth == 1, "the final RMSNorm is fused into the last layer's FFN kernel"
    assert seq % DILATED_PATTERNS[-1][0] == 0 and seq % CHUNK == 0
    h = x.reshape(batch * seq, d)
    out = _layer(h, norm1_g[0], w_in[0], rwkv_shift_mix[0], rwkv_w0[0], rwkv_w_up[0], rwkv_a0[0],
                 rwkv_a_up[0], rwkv_g_up[0], rwkv_k_k[0], rwkv_k_a[0], rwkv_r_k[0], rwkv_ln_w[0],
                 rwkv_ln_b[0], rel_bias_table, w_out[0], norm2_g[0], w_gate_up[0], w_down[0], final_g,
                 batch=batch, seq=seq, tile_m=min(1024, batch * seq))
    return out.reshape(batch, seq, d)
```

```python
import functools
import math

import jax
import jax.numpy as jnp
from jax import lax
from jax.experimental import pallas as pl
from jax.experimental.pallas import tpu as pltpu

F32 = jnp.float32
BF16 = jnp.bfloat16

HEAD_DIM = 64
LANES = 128
DILATED_PATTERNS = ((128, 1), (512, 4), (2048, 16))
REL_BUCKETS = 32
REL_MAX_DIST = 2048
RMS_EPS = 1e-6
GN_EPS = 64e-5
DECAY_SCALE = math.exp(-0.5)
ATTN_SCALE = HEAD_DIM ** -0.5
MASK_VALUE = -1e30
LOG2_E = math.log2(math.e)
LN_2 = math.log(2.0)
CHUNK = 64
SEG_LANES = 256
VMEM_LIMIT = 56 * 1024 * 1024


def _mm(a, b):
    return jnp.dot(a.astype(BF16), b.astype(BF16), preferred_element_type=F32)


def _mm_nt(a, b):
    return lax.dot_general(a.astype(BF16), b.astype(BF16), (((1,), (1,)), ((), ())),
                           preferred_element_type=F32)


def _mm_tn(a, b):
    return lax.dot_general(a.astype(BF16), b.astype(BF16), (((0,), (0,)), ((), ())),
                           preferred_element_type=F32)


def _mm_split(a, b_exact):
    hi = a.astype(BF16)
    lo = (a - hi.astype(F32)).astype(BF16)
    return (jnp.dot(hi, b_exact, preferred_element_type=F32)
            + jnp.dot(lo, b_exact, preferred_element_type=F32))


def _params(semantics):
    return pltpu.CompilerParams(dimension_semantics=semantics, vmem_limit_bytes=VMEM_LIMIT)


def _in_proj_kernel(x_ref, g_ref, w_ref, wl_ref, qkv_ref, rkv_ref, lora_ref, xn_ref, *, n_qkv, n_lora):
    j = pl.program_id(1)

    @pl.when(j == 0)
    def _():
        x = x_ref[...]
        ms = jnp.mean(x * x, axis=-1, keepdims=True)
        xn_ref[...] = (x * lax.rsqrt(ms + RMS_EPS) * g_ref[...]).astype(BF16)

    @pl.when(j < n_qkv)
    def _():
        qkv_ref[...] = jnp.dot(xn_ref[...], w_ref[...], preferred_element_type=F32)

    @pl.when((j >= n_qkv) & (j < 2 * n_qkv))
    def _():
        rkv_ref[...] = jnp.dot(xn_ref[...], w_ref[...], preferred_element_type=F32)

    @pl.when(j == 2 * n_qkv)
    def _():
        col = lax.broadcasted_iota(jnp.int32, (1, wl_ref.shape[1]), 1)
        wl = jnp.where(col < n_lora, wl_ref[...], jnp.zeros((), BF16))
        lora_ref[...] = jnp.dot(xn_ref[...], wl, preferred_element_type=F32)


def _in_proj(x2, g, w, *, n_main, n_lora_pad, tm, tn):
    t, d = x2.shape
    n_lora = w.shape[1] - n_main
    n_qkv = (n_main // 2) // tn
    last = 2 * n_qkv - 1
    assert n_main % n_lora_pad == 0 and n_lora <= n_lora_pad
    return pl.pallas_call(
        functools.partial(_in_proj_kernel, n_qkv=n_qkv, n_lora=n_lora),
        grid=(t // tm, 2 * n_qkv + 1),
        in_specs=[
            pl.BlockSpec((tm, d), lambda i, j: (i, 0)),
            pl.BlockSpec((1, d), lambda i, j: (0, 0)),
            pl.BlockSpec((d, tn), lambda i, j: (0, jnp.minimum(j, last))),
            pl.BlockSpec((d, n_lora_pad), lambda i, j: (0, n_main // n_lora_pad)),
        ],
        out_specs=[
            pl.BlockSpec((tm, tn), lambda i, j: (i, jnp.minimum(j, n_qkv - 1))),
            pl.BlockSpec((tm, tn), lambda i, j: (i, jnp.clip(j - n_qkv, 0, n_qkv - 1))),
            pl.BlockSpec((tm, n_lora_pad), lambda i, j: (i, 0)),
        ],
        out_shape=[
            jax.ShapeDtypeStruct((t, n_main // 2), F32),
            jax.ShapeDtypeStruct((t, n_main // 2), F32),
            jax.ShapeDtypeStruct((t, n_lora_pad), F32),
        ],
        scratch_shapes=[pltpu.VMEM((tm, d), BF16)],
        compiler_params=_params(("arbitrary", "arbitrary")),
        name="in_proj",
    )(x2, g, w, w)


ATTN_BLK = 128
ATTN_ROWS = max(w for w, _ in DILATED_PATTERNS)
ATTN_GROUP = 4


def _attn_kernel(q_ref, kp_ref, kc_ref, vp_ref, vc_ref, band_ref, out_ref, bias_ref, o_ref, lse_ref,
                 *, n_heads):
    b, n, hp = pl.program_id(0), pl.program_id(1), pl.program_id(2)
    blk = ATTN_BLK

    @pl.when((b == 0) & (n == 0) & (hp == 0))
    def _():
        for i in range(bias_ref.shape[0]):
            row = jnp.broadcast_to(band_ref[i:i + 1, :], (blk, 2 * blk))
            bias_ref[i] = pltpu.roll(row, 0, 1, stride=1, stride_axis=0)

    lane = lax.broadcasted_iota(jnp.int32, (1, LANES), 1)
    in_head = [(lane >= half * HEAD_DIM) & (lane < (half + 1) * HEAD_DIM) for half in range(2)]
    key_idx = lax.broadcasted_iota(jnp.int32, (1, 2 * blk), 1)
    first_ok = (key_idx >= blk) | (n > 0)
    ones_tile = jnp.ones((2 * blk, LANES), BF16)
    zero = jnp.zeros((), BF16)

    for bi, (window, dilation) in enumerate(DILATED_PATTERNS):
        units = [(sub, r) for sub in range(ATTN_ROWS // window) for r in range(dilation)]
        for g0 in range(0, len(units), ATTN_GROUP):
            group = units[g0:g0 + ATTN_GROUP]
            rows, q2, k2, v2, key_ok = [], [], [], [], []
            for sub, r in group:
                cur = pl.ds(sub * window + r, blk, stride=dilation)
                if sub == 0:
                    prev_k, prev_v = kp_ref, vp_ref
                    prev = pl.ds(ATTN_ROWS - window + r, blk, stride=dilation)
                else:
                    prev_k, prev_v = kc_ref, vc_ref
                    prev = pl.ds((sub - 1) * window + r, blk, stride=dilation)
                rows.append(cur)
                q2.append((q_ref[cur, :] * (ATTN_SCALE * LOG2_E)).astype(BF16))
                k2.append(jnp.concatenate([prev_k[prev, :], kc_ref[cur, :]], axis=0).astype(BF16))
                v = jnp.concatenate([prev_v[prev, :], vc_ref[cur, :]], axis=0).astype(BF16)
                v2.append(jnp.concatenate([v, ones_tile], axis=1))
                key_ok.append(first_ok if sub == 0 else None)
            s = [[_mm_nt(jnp.where(m, q, zero), k) for m in in_head] for q, k in zip(q2, k2)]
            outs = []
            for u in range(len(group)):
                halves = []
                for half in range(2):
                    su = s[u][half] + bias_ref[(bi * n_heads) + 2 * hp + half]
                    if key_ok[u] is not None:
                        su = jnp.where(key_ok[u], su, MASK_VALUE)
                    mx = jnp.max(su, axis=-1, keepdims=True)
                    halves.append((jnp.exp2(su - mx).astype(BF16), mx))
                outs.append(halves)
            for u in range(len(group)):
                (pa, ma), (pb, mb) = outs[u]
                ra = jnp.dot(pa, v2[u], preferred_element_type=F32)
                rb = jnp.dot(pb, v2[u], preferred_element_type=F32)
                la, lb = ra[:, LANES:], rb[:, LANES:]
                o_ref[bi, rows[u], :] = jnp.where(in_head[0], ra[:, :LANES] / la, rb[:, :LANES] / lb)
                lse_ref[bi, rows[u], :] = jnp.where(in_head[0], ma * LN_2 + jnp.log(la), mb * LN_2 + jnp.log(lb))

    lses = [lse_ref[bi] for bi in range(len(DILATED_PATTERNS))]
    top = functools.reduce(jnp.maximum, lses)
    es = [jnp.exp(l - top) for l in lses]
    num = functools.reduce(jnp.add, [e * o_ref[bi] for bi, e in enumerate(es)])
    out_ref[...] = (num / functools.reduce(jnp.add, es)).astype(out_ref.dtype)


def _attention(qkv, band, *, batch, seq, n_heads):
    width = n_heads * HEAD_DIM
    n_pairs = width // LANES
    nb = seq // ATTN_ROWS
    n_pat = len(DILATED_PATTERNS)
    spec = lambda f: pl.BlockSpec((ATTN_ROWS, LANES), f)
    prev = lambda b, n: b * nb + jnp.maximum(n - 1, 0)
    return pl.pallas_call(
        functools.partial(_attn_kernel, n_heads=n_heads),
        grid=(batch, nb, n_pairs),
        in_specs=[
            spec(lambda b, n, hp: (b * nb + n, hp)),
            spec(lambda b, n, hp: (prev(b, n), n_pairs + hp)),
            spec(lambda b, n, hp: (b * nb + n, n_pairs + hp)),
            spec(lambda b, n, hp: (prev(b, n), 2 * n_pairs + hp)),
            spec(lambda b, n, hp: (b * nb + n, 2 * n_pairs + hp)),
            pl.BlockSpec(band.shape, lambda b, n, hp: (0, 0)),
        ],
        out_specs=spec(lambda b, n, hp: (b * nb + n, hp)),
        out_shape=jax.ShapeDtypeStruct((batch * seq, width), BF16),
        scratch_shapes=[
            pltpu.VMEM((n_pat * n_heads, ATTN_BLK, 2 * ATTN_BLK), F32),
            pltpu.VMEM((n_pat, ATTN_ROWS, LANES), F32),
            pltpu.VMEM((n_pat, ATTN_ROWS, LANES), F32),
        ],
        compiler_params=_params(("arbitrary", "arbitrary", "arbitrary")),
        name="dilated_attn",
    )(qkv, qkv, qkv, qkv, qkv, band)


def _t5_bucket(dist):
    exact = REL_BUCKETS // 2
    d_f = jnp.maximum(dist, 1).astype(F32)
    large = exact + (jnp.log(d_f / exact) / math.log(REL_MAX_DIST / exact)
                     * (REL_BUCKETS - exact)).astype(jnp.int32)
    large = jnp.minimum(large, REL_BUCKETS - 1)
    return jnp.where(dist < exact, dist, large)


def _band_rows(bias_table):
    blk = ATTN_BLK
    rel = blk - jnp.arange(2 * blk)
    band = (rel >= 0) & (rel <= blk)
    rows = []
    for _, dilation in DILATED_PATTERNS:
        bias = bias_table[_t5_bucket(jnp.clip(rel, 0, blk) * dilation)]
        rows.append(jnp.where(band[:, None], bias.astype(F32) * LOG2_E, MASK_VALUE).T)
    return jnp.concatenate(rows, axis=0)


def _rwkv_kernel(rkv_ref, lora_ref, mix_rkv_ref, mix_lora_ref, w0_ref, w_up_ref, a0_ref, a_up_ref,
                 g_up_ref, kk_ref, ka_ref, rk_ref, lnw_ref, lnb_ref, seg_ref, tri_ref,
                 out_ref,
                 state_ref, carry_rkv_ref, carry_lora_ref,
                 rt_ref, at_ref, bh_ref, kh_ref, bc_ref, kc_ref, v_ref, pc_ref, y_ref, *, width):
    c = pl.program_id(0)
    n_pairs = width // LANES
    n_seq, ch = rkv_ref.shape[:2]
    rows = n_seq * ch

    @pl.when(c == 0)
    def _():
        state_ref[...] = jnp.zeros_like(state_ref)
        carry_rkv_ref[...] = jnp.zeros_like(carry_rkv_ref)
        carry_lora_ref[...] = jnp.zeros_like(carry_lora_ref)

    row = lax.broadcasted_iota(jnp.int32, (rows, 1), 0) % ch

    def per_seq(row_of):
        return jnp.concatenate([jnp.broadcast_to(row_of(b), (ch, row_of(b).shape[-1]))
                                for b in range(n_seq)], axis=0)

    def token_shift(z_ref, carry_ref, mix):
        z = z_ref[...].reshape(rows, z_ref.shape[-1])
        prev = jnp.where(row == 0, per_seq(lambda b: carry_ref[b]), pltpu.roll(z, 1, axis=0))
        for b in range(n_seq):
            carry_ref[b] = z[(b + 1) * ch - 1:(b + 1) * ch, :]
        return z + (prev - z) * mix

    z = token_shift(rkv_ref, carry_rkv_ref, mix_rkv_ref[...])
    zl = token_shift(lora_ref, carry_lora_ref, mix_lora_ref[...])
    r, k, v = z[:, :width], z[:, width:2 * width], z[:, 2 * width:]

    lw = -DECAY_SCALE * jax.nn.sigmoid(w0_ref[...] + _mm(jnp.tanh(zl), w_up_ref[...]))
    a_sig = jax.nn.sigmoid(a0_ref[...] + _mm(zl, a_up_ref[...]))
    gate = _mm(jax.nn.sigmoid(zl), g_up_ref[...])

    seg = seg_ref[...]

    def head_sums(x):
        tiles = [_mm_split(x[:, t:t + SEG_LANES], seg) for t in range(0, width, SEG_LANES)]
        return jnp.concatenate(tiles, axis=1)

    kk = k * kk_ref[...]
    kk = kk / jnp.maximum(jnp.sqrt(head_sums(kk * kk)), 1e-12)
    k = k * (1.0 + (a_sig - 1.0) * ka_ref[...])
    bonus = head_sums(r * k * rk_ref[...]) * v
    a_in = -kk
    b_in = kk * a_sig

    lw_hi = lw.astype(BF16)
    lw_lo = (lw - lw_hi.astype(F32)).astype(BF16)
    tri = tri_ref[...]
    cum = (jnp.dot(tri, lw_hi, preferred_element_type=F32)
           + jnp.dot(tri, lw_lo, preferred_element_type=F32))
    cum_end = per_seq(lambda b: cum[(b + 1) * ch - 1:(b + 1) * ch, :])
    e_neg = jnp.exp(-cum)
    e_end = jnp.exp(cum_end - cum)

    n_units = n_seq * n_pairs

    def put(ref, val):
        for b in range(n_seq):
            for p in range(n_pairs):
                ref[b * n_pairs + p] = val[b * ch:b * ch + ref.shape[1], p * LANES:(p + 1) * LANES]

    put(rt_ref, r * jnp.exp(cum))
    put(at_ref, a_in * jnp.exp(cum - lw))
    put(bh_ref, b_in * e_neg)
    put(kh_ref, k * e_neg)
    put(bc_ref, b_in * e_end)
    put(kc_ref, k * e_end)
    put(v_ref, v)
    put(pc_ref, jnp.exp(cum_end))

    ti = lax.broadcasted_iota(jnp.int32, (ch, 2 * ch), 0)
    si = lax.broadcasted_iota(jnp.int32, (ch, 2 * ch), 1) % ch
    incl = ti >= si
    strict = ti > si
    lane = lax.broadcasted_iota(jnp.int32, (1, 2 * LANES), 1)
    head_a2 = (lane % LANES) < HEAD_DIM
    head_a = head_a2[:, :LANES]
    di = lax.broadcasted_iota(jnp.int32, (LANES, LANES), 0)
    dj = lax.broadcasted_iota(jnp.int32, (LANES, LANES), 1)
    same_head = (di < HEAD_DIM) == (dj < HEAD_DIM)
    diag = di == dj

    def stack(x, mask):
        return jnp.concatenate([jnp.where(mask, x, 0.0), jnp.where(mask, 0.0, x)], axis=0).astype(BF16)

    def block_diag(m):
        return jnp.where(same_head, jnp.concatenate([m, m], axis=0), 0.0).astype(BF16)

    pairs = range(n_units)
    zero_tile = jnp.zeros((2 * ch, LANES), BF16)
    a_rb, a_ab, a_rk, a_ak, vst = [], [], [], [], []
    for p in pairs:
        lhs = jnp.concatenate([rt_ref[p], at_ref[p]], axis=0)
        rhs = jnp.concatenate([stack(bh_ref[p], head_a), stack(kh_ref[p], head_a)], axis=0)
        a_bk = _mm_nt(lhs, rhs)
        a_rb.append(jnp.where(incl, a_bk[:ch, :LANES], 0.0))
        a_ab.append(jnp.where(strict, a_bk[ch:, :LANES], 0.0))
        a_rk.append(jnp.where(incl, a_bk[:ch, LANES:], 0.0))
        a_ak.append(jnp.where(strict, a_bk[ch:, LANES:], 0.0))
        vst.append(stack(v_ref[p], head_a))

    xs = [jnp.concatenate([at_ref[p], _mm(a_ak[p], vst[p])], axis=1) for p in pairs]
    nk = a_ab
    for _ in range(int(math.log2(ch)) - 1):
        nk_next = [_mm(n, block_diag(n)) for n in nk]
        xs = [x + _mm(n, stack(x, head_a2)) for n, x in zip(nk, xs)]
        nk = nk_next
    xs = [x + _mm(n, stack(x, head_a2)) for n, x in zip(nk, xs)]

    ax = [_mm(jnp.concatenate([a_rb[p], a_rk[p]], axis=1),
              jnp.concatenate([stack(xs[p], head_a2), jnp.concatenate([zero_tile, vst[p]], axis=1)], axis=0))
          for p in pairs]
    mg = [_mm_tn(jnp.concatenate([bc_ref[p], kc_ref[p]], axis=0),
                 jnp.concatenate([xs[p], jnp.concatenate([jnp.zeros((ch, LANES), F32), v_ref[p]], axis=1)],
                                 axis=0))
          for p in pairs]
    for p in pairs:
        q_acc = rt_ref[p] + ax[p][:, :LANES]
        m_mat = jnp.where(same_head, mg[p][:, :LANES], 0.0) + jnp.where(diag, pc_ref[p], 0.0)
        g_mat = jnp.where(same_head, mg[p][:, LANES:], 0.0)
        res = _mm(jnp.concatenate([q_acc, m_mat], axis=0), state_ref[p])
        y_ref[p] = res[:ch] + ax[p][:, LANES:]
        state_ref[p] = res[ch:] + g_mat

    y = jnp.concatenate([jnp.concatenate([y_ref[b * n_pairs + p] for p in range(n_pairs)], axis=1)
                         for b in range(n_seq)], axis=0)
    inv_n = 1.0 / HEAD_DIM
    mu = head_sums(y) * inv_n
    d = y - mu
    var = head_sums(d * d) * inv_n
    yn = d * lax.rsqrt(var + GN_EPS) * lnw_ref[...] + lnb_ref[...]
    out_ref[...] = ((yn + bonus) * gate).reshape(out_ref.shape).astype(out_ref.dtype)


def _rwkv(rkv, lora, prm, *, batch, seq):
    t, w3 = rkv.shape
    width = w3 // 3
    n_lora = lora.shape[1]
    ch = CHUNK
    n_chunks = seq // ch
    n_pairs = width // LANES
    row = lambda n: pl.BlockSpec((1, n), lambda c: (0, 0))
    full = lambda a: pl.BlockSpec(a.shape, lambda c: (0,) * a.ndim)
    head_id = jnp.arange(SEG_LANES) // HEAD_DIM
    seg = (head_id[:, None] == head_id[None, :]).astype(BF16)
    pos = jnp.arange(batch * ch)
    tri = ((pos[:, None] >= pos[None, :]) & (pos[:, None] // ch == pos[None, :] // ch)).astype(BF16)
    n_units = batch * n_pairs
    big = lambda: pltpu.VMEM((n_units, ch, LANES), F32)
    args = (rkv.reshape(batch, seq, w3), lora.reshape(batch, seq, n_lora), prm["mix_rkv"], prm["mix_lora"],
            prm["w0"], prm["w_up"], prm["a0"], prm["a_up"], prm["g_up"], prm["k_k"], prm["k_a"], prm["r_k"],
            prm["ln_w"], prm["ln_b"], seg, tri)
    in_specs = [
        pl.BlockSpec((batch, ch, w3), lambda c: (0, c, 0)),
        pl.BlockSpec((batch, ch, n_lora), lambda c: (0, c, 0)),
        row(w3), row(n_lora), row(width), full(prm["w_up"]), row(width), full(prm["a_up"]),
        full(prm["g_up"]), row(width), row(width), row(width), row(width), row(width),
        full(seg), full(tri),
    ]
    out = pl.pallas_call(
        functools.partial(_rwkv_kernel, width=width),
        grid=(n_chunks,),
        in_specs=in_specs,
        out_specs=pl.BlockSpec((batch, ch, width), lambda c: (0, c, 0)),
        out_shape=jax.ShapeDtypeStruct((batch, seq, width), BF16),
        scratch_shapes=[
            pltpu.VMEM((n_units, LANES, LANES), F32),
            pltpu.VMEM((batch, 1, w3), F32),
            pltpu.VMEM((batch, 1, n_lora), F32),
            big(), big(), big(), big(), big(), big(), big(),
            pltpu.VMEM((n_units, 1, LANES), F32),
            big(),
        ],
        compiler_params=_params(("arbitrary",)),
        name="rwkv7",
    )(*args)
    return out.reshape(t, width)


def _out_proj_kernel(attn_ref, rw_ref, wa_ref, wr_ref, x_ref, out_ref):
    out_ref[...] = (x_ref[...] + jnp.dot(attn_ref[...], wa_ref[...], preferred_element_type=F32)
                    + jnp.dot(rw_ref[...], wr_ref[...], preferred_element_type=F32))


def _out_proj(attn, rw, w_out, x2, *, tm, tn):
    t, width = attn.shape
    d = w_out.shape[1]
    return pl.pallas_call(
        _out_proj_kernel,
        grid=(t // tm, d // tn),
        in_specs=[
            pl.BlockSpec((tm, width), lambda i, j: (i, 0)),
            pl.BlockSpec((tm, width), lambda i, j: (i, 0)),
            pl.BlockSpec((width, tn), lambda i, j: (0, j)),
            pl.BlockSpec((width, tn), lambda i, j: (1, j)),
            pl.BlockSpec((tm, tn), lambda i, j: (i, j)),
        ],
        out_specs=pl.BlockSpec((tm, tn), lambda i, j: (i, j)),
        out_shape=jax.ShapeDtypeStruct((t, d), F32),
        compiler_params=_params(("arbitrary", "arbitrary")),
        name="out_proj",
    )(attn, rw, w_out, w_out, x2)


def _ffn_up_kernel(h_ref, g_ref, wg_ref, wu_ref, act_ref, hn_ref):
    @pl.when(pl.program_id(1) == 0)
    def _():
        h = h_ref[...]
        ms = jnp.mean(h * h, axis=-1, keepdims=True)
        hn_ref[...] = (h * lax.rsqrt(ms + RMS_EPS) * g_ref[...]).astype(BF16)

    hn = hn_ref[...]
    gate = jnp.dot(hn, wg_ref[...], preferred_element_type=F32)
    up = jnp.dot(hn, wu_ref[...], preferred_element_type=F32)
    act_ref[...] = (gate * jax.nn.sigmoid(gate) * up).astype(BF16)


def _ffn_up(h, g, w_gate_up, *, tm, th):
    t, d = h.shape
    hidden = w_gate_up.shape[1] // 2
    n_h = hidden // th
    return pl.pallas_call(
        _ffn_up_kernel,
        grid=(t // tm, n_h),
        in_specs=[
            pl.BlockSpec((tm, d), lambda i, j: (i, 0)),
            pl.BlockSpec((1, d), lambda i, j: (0, 0)),
            pl.BlockSpec((d, th), lambda i, j: (0, j)),
            pl.BlockSpec((d, th), lambda i, j: (0, n_h + j)),
        ],
        out_specs=pl.BlockSpec((tm, th), lambda i, j: (i, j)),
        out_shape=jax.ShapeDtypeStruct((t, hidden), BF16),
        scratch_shapes=[pltpu.VMEM((tm, d), BF16)],
        compiler_params=_params(("arbitrary", "arbitrary")),
        name="ffn_up",
    )(h, g, w_gate_up, w_gate_up)


def _ffn_down_kernel(act_ref, w_ref, h_ref, g_ref, out_ref, res_ref, ssq_ref, *, d):
    phase, j = pl.program_id(1), pl.program_id(2)

    @pl.when(phase == 0)
    def _():
        res = h_ref[...] + jnp.dot(act_ref[...], w_ref[...], preferred_element_type=F32)
        res_ref[j] = res
        part = jnp.sum(res * res, axis=-1, keepdims=True)

        @pl.when(j == 0)
        def _():
            ssq_ref[...] = part

        @pl.when(j > 0)
        def _():
            ssq_ref[...] += part

    @pl.when(phase == 1)
    def _():
        scale = lax.rsqrt(ssq_ref[...] * (1.0 / d) + RMS_EPS)
        out_ref[...] = res_ref[j] * scale * g_ref[j]


def _ffn_down(act, w_down, h, g, *, tm, tn):
    t, hidden = act.shape
    d = w_down.shape[1]
    n_j = d // tn
    col = lambda phase, j: jnp.where(phase == 0, j, n_j - 1)
    return pl.pallas_call(
        functools.partial(_ffn_down_kernel, d=d),
        grid=(t // tm, 2, n_j),
        in_specs=[
            pl.BlockSpec((tm, hidden), lambda i, phase, j: (i, 0)),
            pl.BlockSpec((hidden, tn), lambda i, phase, j: (0, col(phase, j))),
            pl.BlockSpec((tm, tn), lambda i, phase, j: (i, col(phase, j))),
            pl.BlockSpec((n_j, 1, tn), lambda i, phase, j: (0, 0, 0)),
        ],
        out_specs=pl.BlockSpec((tm, tn), lambda i, phase, j: (i, jnp.where(phase == 0, 0, j))),
        out_shape=jax.ShapeDtypeStruct((t, d), F32),
        scratch_shapes=[pltpu.VMEM((n_j, tm, tn), F32), pltpu.VMEM((tm, 1), F32)],
        compiler_params=_params(("arbitrary", "arbitrary", "arbitrary")),
        name="ffn_down",
    )(act, w_down, h, g.reshape(n_j, 1, tn))


def _pad_rows(a, rows):
    return jnp.pad(a, ((0, rows - a.shape[0]), (0, 0)))


def _layer(h, norm1_g, w_in, shift_mix, w0, w_up, a0, a_up, g_up, k_k, k_a, r_k, ln_w, ln_b,
           rel_bias_table, w_out, norm2_g, w_gate_up, w_down, out_g, *, batch, seq, tile_m):
    d = h.shape[1]
    width = w_out.shape[0] // 2
    n_heads = width // HEAD_DIM
    n_main = 6 * width
    rank_w, rank_a, rank_g = w_up.shape[0], a_up.shape[0], g_up.shape[0]
    n_lora = rank_w + rank_a + rank_g
    n_lora_pad = -(-n_lora // LANES) * LANES

    qkv, rkv, lora = _in_proj(h, norm1_g.reshape(1, d), w_in.astype(BF16), n_main=n_main,
                              n_lora_pad=n_lora_pad, tm=tile_m, tn=3 * width // 4)

    attn = _attention(qkv, _band_rows(rel_bias_table), batch=batch, seq=seq, n_heads=n_heads)

    mix_rkv = shift_mix[:3 * width].reshape(1, 3 * width)
    mix_lora = jnp.pad(shift_mix[3 * width:], (0, n_lora_pad - n_lora)).reshape(1, n_lora_pad)
    zeros = lambda n: jnp.zeros((n, width), F32)
    prm = dict(
        mix_rkv=mix_rkv, mix_lora=mix_lora,
        w0=w0.reshape(1, width), a0=a0.reshape(1, width),
        w_up=_pad_rows(w_up, n_lora_pad).astype(BF16),
        a_up=_pad_rows(jnp.concatenate([zeros(rank_w), a_up]), n_lora_pad).astype(BF16),
        g_up=_pad_rows(jnp.concatenate([zeros(rank_w + rank_a), g_up]), n_lora_pad).astype(BF16),
        k_k=k_k.reshape(1, width), k_a=k_a.reshape(1, width), r_k=r_k.reshape(1, width),
        ln_w=ln_w.reshape(1, width), ln_b=ln_b.reshape(1, width),
    )
    rw = _rwkv(rkv, lora, prm, batch=batch, seq=seq)

    h1 = _out_proj(attn, rw, w_out.astype(BF16), h, tm=tile_m, tn=d // 2)
    act = _ffn_up(h1, norm2_g.reshape(1, d), w_gate_up.astype(BF16), tm=tile_m, th=512)
    return _ffn_down(act, w_down.astype(BF16), h1, out_g, tm=tile_m, tn=512)


def kernel(x, norm1_g, w_in, rwkv_shift_mix, rwkv_w0, rwkv_w_up, rwkv_a0, rwkv_a_up, rwkv_g_up, rwkv_k_k, rwkv_k_a, rwkv_r_k, rwkv_ln_w, rwkv_ln_b, rel_bias_table, w_out, norm2_g, w_gate_up, w_down, final_g):
    batch, seq, d = x.shape
    depth = w_in.shape[0]
    assert depth == 1, "the final RMSNorm is fused into the last layer's FFN kernel"
    assert seq % DILATED_PATTERNS[-1][0] == 0 and seq % CHUNK == 0
    h = x.reshape(batch * seq, d)
    out = _layer(h, norm1_g[0], w_in[0], rwkv_shift_mix[0], rwkv_w0[0], rwkv_w_up[0], rwkv_a0[0],
                 rwkv_a_up[0], rwkv_g_up[0], rwkv_k_k[0], rwkv_k_a[0], rwkv_r_k[0], rwkv_ln_w[0],
                 rwkv_ln_b[0], rel_bias_table, w_out[0], norm2_g[0], w_gate_up[0], w_down[0], final_g,
                 batch=batch, seq=seq, tile_m=min(1024, batch * seq))
    return out.reshape(batch, seq, d)
```

```python
import functools
import math

import jax
import jax.numpy as jnp
from jax import lax
from jax.experimental import pallas as pl
from jax.experimental.pallas import tpu as pltpu

F32 = jnp.float32
BF16 = jnp.bfloat16

HEAD_DIM = 64
LANES = 128
DILATED_PATTERNS = ((128, 1), (512, 4), (2048, 16))
REL_BUCKETS = 32
REL_MAX_DIST = 2048
RMS_EPS = 1e-6
GN_EPS = 64e-5
DECAY_SCALE = math.exp(-0.5)
ATTN_SCALE = HEAD_DIM ** -0.5
MASK_VALUE = -1e30
LOG2_E = math.log2(math.e)
LN_2 = math.log(2.0)
CHUNK = 64
SEG_LANES = 256
VMEM_LIMIT = 56 * 1024 * 1024


def _mm(a, b):
    return jnp.dot(a.astype(BF16), b.astype(BF16), preferred_element_type=F32)


def _mm_nt(a, b):
    return lax.dot_general(a.astype(BF16), b.astype(BF16), (((1,), (1,)), ((), ())),
                           preferred_element_type=F32)


def _mm_tn(a, b):
    return lax.dot_general(a.astype(BF16), b.astype(BF16), (((0,), (0,)), ((), ())),
                           preferred_element_type=F32)


def _mm_split(a, b_exact):
    hi = a.astype(BF16)
    lo = (a - hi.astype(F32)).astype(BF16)
    return (jnp.dot(hi, b_exact, preferred_element_type=F32)
            + jnp.dot(lo, b_exact, preferred_element_type=F32))


def _params(semantics):
    return pltpu.CompilerParams(dimension_semantics=semantics, vmem_limit_bytes=VMEM_LIMIT)


def _in_proj_kernel(x_ref, g_ref, w_ref, wl_ref, qkv_ref, rkv_ref, lora_ref, xn_ref, *, n_qkv, n_lora):
    j = pl.program_id(1)

    @pl.when(j == 0)
    def _():
        x = x_ref[...]
        ms = jnp.mean(x * x, axis=-1, keepdims=True)
        xn_ref[...] = (x * lax.rsqrt(ms + RMS_EPS) * g_ref[...]).astype(BF16)

    @pl.when(j < n_qkv)
    def _():
        qkv_ref[...] = jnp.dot(xn_ref[...], w_ref[...], preferred_element_type=F32)

    @pl.when((j >= n_qkv) & (j < 2 * n_qkv))
    def _():
        rkv_ref[...] = jnp.dot(xn_ref[...], w_ref[...], preferred_element_type=F32)

    @pl.when(j == 2 * n_qkv)
    def _():
        col = lax.broadcasted_iota(jnp.int32, (1, wl_ref.shape[1]), 1)
        wl = jnp.where(col < n_lora, wl_ref[...], jnp.zeros((), BF16))
        lora_ref[...] = jnp.dot(xn_ref[...], wl, preferred_element_type=F32)


def _in_proj(x2, g, w, *, n_main, n_lora_pad, tm, tn):
    t, d = x2.shape
    n_lora = w.shape[1] - n_main
    n_qkv = (n_main // 2) // tn
    last = 2 * n_qkv - 1
    assert n_main % n_lora_pad == 0 and n_lora <= n_lora_pad
    return pl.pallas_call(
        functools.partial(_in_proj_kernel, n_qkv=n_qkv, n_lora=n_lora),
        grid=(t // tm, 2 * n_qkv + 1),
        in_specs=[
            pl.BlockSpec((tm, d), lambda i, j: (i, 0)),
            pl.BlockSpec((1, d), lambda i, j: (0, 0)),
            pl.BlockSpec((d, tn), lambda i, j: (0, jnp.minimum(j, last))),
            pl.BlockSpec((d, n_lora_pad), lambda i, j: (0, n_main // n_lora_pad)),
        ],
        out_specs=[
            pl.BlockSpec((tm, tn), lambda i, j: (i, jnp.minimum(j, n_qkv - 1))),
            pl.BlockSpec((tm, tn), lambda i, j: (i, jnp.clip(j - n_qkv, 0, n_qkv - 1))),
            pl.BlockSpec((tm, n_lora_pad), lambda i, j: (i, 0)),
        ],
        out_shape=[
            jax.ShapeDtypeStruct((t, n_main // 2), F32),
            jax.ShapeDtypeStruct((t, n_main // 2), F32),
            jax.ShapeDtypeStruct((t, n_lora_pad), F32),
        ],
        scratch_shapes=[pltpu.VMEM((tm, d), BF16)],
        compiler_params=_params(("arbitrary", "arbitrary")),
        name="in_proj",
    )(x2, g, w, w)


ATTN_BLK = 128
ATTN_ROWS = max(w for w, _ in DILATED_PATTERNS)
ATTN_GROUP = 4


def _attn_kernel(q_ref, kp_ref, kc_ref, vp_ref, vc_ref, band_ref, out_ref, bias_ref, o_ref, lse_ref,
                 *, n_heads):
    b, n, hp = pl.program_id(0), pl.program_id(1), pl.program_id(2)
    blk = ATTN_BLK

    @pl.when((b == 0) & (n == 0) & (hp == 0))
    def _():
        for i in range(bias_ref.shape[0]):
            row = jnp.broadcast_to(band_ref[i:i + 1, :], (blk, 2 * blk))
            bias_ref[i] = pltpu.roll(row, 0, 1, stride=1, stride_axis=0)

    lane = lax.broadcasted_iota(jnp.int32, (1, LANES), 1)
    in_head = [(lane >= half * HEAD_DIM) & (lane < (half + 1) * HEAD_DIM) for half in range(2)]
    key_idx = lax.broadcasted_iota(jnp.int32, (1, 2 * blk), 1)
    first_ok = (key_idx >= blk) | (n > 0)
    ones_tile = jnp.ones((2 * blk, LANES), BF16)
    zero = jnp.zeros((), BF16)

    for bi, (window, dilation) in enumerate(DILATED_PATTERNS):
        units = [(sub, r) for sub in range(ATTN_ROWS // window) for r in range(dilation)]
        for g0 in range(0, len(units), ATTN_GROUP):
            group = units[g0:g0 + ATTN_GROUP]
            rows, q2, k2, v2, key_ok = [], [], [], [], []
            for sub, r in group:
                cur = pl.ds(sub * window + r, blk, stride=dilation)
                if sub == 0:
                    prev_k, prev_v = kp_ref, vp_ref
                    prev = pl.ds(ATTN_ROWS - window + r, blk, stride=dilation)
                else:
                    prev_k, prev_v = kc_ref, vc_ref
                    prev = pl.ds((sub - 1) * window + r, blk, stride=dilation)
                rows.append(cur)
                q2.append((q_ref[cur, :] * (ATTN_SCALE * LOG2_E)).astype(BF16))
                k2.append(jnp.concatenate([prev_k[prev, :], kc_ref[cur, :]], axis=0).astype(BF16))
                v = jnp.concatenate([prev_v[prev, :], vc_ref[cur, :]], axis=0).astype(BF16)
                v2.append(jnp.concatenate([v, ones_tile], axis=1))
                key_ok.append(first_ok if sub == 0 else None)
            s = [[_mm_nt(jnp.where(m, q, zero), k) for m in in_head] for q, k in zip(q2, k2)]
            outs = []
            for u in range(len(group)):
                halves = []
                for half in range(2):
                    su = s[u][half] + bias_ref[(bi * n_heads) + 2 * hp + half]
                    if key_ok[u] is not None:
                        su = jnp.where(key_ok[u], su, MASK_VALUE)
                    mx = jnp.max(su, axis=-1, keepdims=True)
                    halves.append((jnp.exp2(su - mx).astype(BF16), mx))
                outs.append(halves)
            for u in range(len(group)):
                (pa, ma), (pb, mb) = outs[u]
                ra = jnp.dot(pa, v2[u], preferred_element_type=F32)
                rb = jnp.dot(pb, v2[u], preferred_element_type=F32)
                la, lb = ra[:, LANES:], rb[:, LANES:]
                o_ref[bi, rows[u], :] = jnp.where(in_head[0], ra[:, :LANES] / la, rb[:, :LANES] / lb)
                lse_ref[bi, rows[u], :] = jnp.where(in_head[0], ma * LN_2 + jnp.log(la), mb * LN_2 + jnp.log(lb))

    lses = [lse_ref[bi] for bi in range(len(DILATED_PATTERNS))]
    top = functools.reduce(jnp.maximum, lses)
    es = [jnp.exp(l - top) for l in lses]
    num = functools.reduce(jnp.add, [e * o_ref[bi] for bi, e in enumerate(es)])
    out_ref[...] = (num / functools.reduce(jnp.add, es)).astype(out_ref.dtype)


def _attention(qkv, band, *, batch, seq, n_heads):
    width = n_heads * HEAD_DIM
    n_pairs = width // LANES
    nb = seq // ATTN_ROWS
    n_pat = len(DILATED_PATTERNS)
    spec = lambda f: pl.BlockSpec((ATTN_ROWS, LANES), f)
    prev = lambda b, n: b * nb + jnp.maximum(n - 1, 0)
    return pl.pallas_call(
        functools.partial(_attn_kernel, n_heads=n_heads),
        grid=(batch, nb, n_pairs),
        in_specs=[
            spec(lambda b, n, hp: (b * nb + n, hp)),
            spec(lambda b, n, hp: (prev(b, n), n_pairs + hp)),
            spec(lambda b, n, hp: (b * nb + n, n_pairs + hp)),
            spec(lambda b, n, hp: (prev(b, n), 2 * n_pairs + hp)),
            spec(lambda b, n, hp: (b * nb + n, 2 * n_pairs + hp)),
            pl.BlockSpec(band.shape, lambda b, n, hp: (0, 0)),
        ],
        out_specs=spec(lambda b, n, hp: (b * nb + n, hp)),
        out_shape=jax.ShapeDtypeStruct((batch * seq, width), BF16),
        scratch_shapes=[
            pltpu.VMEM((n_pat * n_heads, ATTN_BLK, 2 * ATTN_BLK), F32),
            pltpu.VMEM((n_pat, ATTN_ROWS, LANES), F32),
            pltpu.VMEM((n_pat, ATTN_ROWS, LANES), F32),
        ],
        compiler_params=_params(("arbitrary", "arbitrary", "arbitrary")),
        name="dilated_attn",
    )(qkv, qkv, qkv, qkv, qkv, band)


def _t5_bucket(dist):
    exact = REL_BUCKETS // 2
    d_f = jnp.maximum(dist, 1).astype(F32)
    large = exact + (jnp.log(d_f / exact) / math.log(REL_MAX_DIST / exact)
                     * (REL_BUCKETS - exact)).astype(jnp.int32)
    large = jnp.minimum(large, REL_BUCKETS - 1)
    return jnp.where(dist < exact, dist, large)


def _band_rows(bias_table):
    blk = ATTN_BLK
    rel = blk - jnp.arange(2 * blk)
    band = (rel >= 0) & (rel <= blk)
    rows = []
    for _, dilation in DILATED_PATTERNS:
        bias = bias_table[_t5_bucket(jnp.clip(rel, 0, blk) * dilation)]
        rows.append(jnp.where(band[:, None], bias.astype(F32) * LOG2_E, MASK_VALUE).T)
    return jnp.concatenate(rows, axis=0)


def _rwkv_kernel(rkv_ref, lora_ref, mix_rkv_ref, mix_lora_ref, w0_ref, w_up_ref, a0_ref, a_up_ref,
                 g_up_ref, kk_ref, ka_ref, rk_ref, lnw_ref, lnb_ref, seg_ref, tri_ref,
                 out_ref,
                 state_ref, carry_rkv_ref, carry_lora_ref,
                 rt_ref, at_ref, bh_ref, kh_ref, bc_ref, kc_ref, v_ref, pc_ref, y_ref, *, width):
    c = pl.program_id(0)
    n_pairs = width // LANES
    n_seq, ch = rkv_ref.shape[:2]
    rows = n_seq * ch

    @pl.when(c == 0)
    def _():
        state_ref[...] = jnp.zeros_like(state_ref)
        carry_rkv_ref[...] = jnp.zeros_like(carry_rkv_ref)
        carry_lora_ref[...] = jnp.zeros_like(carry_lora_ref)

    row = lax.broadcasted_iota(jnp.int32, (rows, 1), 0) % ch

    def per_seq(row_of):
        return jnp.concatenate([jnp.broadcast_to(row_of(b), (ch, row_of(b).shape[-1]))
                                for b in range(n_seq)], axis=0)

    def token_shift(z_ref, carry_ref, mix):
        z = z_ref[...].reshape(rows, z_ref.shape[-1])
        prev = jnp.where(row == 0, per_seq(lambda b: carry_ref[b]), pltpu.roll(z, 1, axis=0))
        for b in range(n_seq):
            carry_ref[b] = z[(b + 1) * ch - 1:(b + 1) * ch, :]
        return z + (prev - z) * mix

    z = token_shift(rkv_ref, carry_rkv_ref, mix_rkv_ref[...])
    zl = token_shift(lora_ref, carry_lora_ref, mix_lora_ref[...])
    r, k, v = z[:, :width], z[:, width:2 * width], z[:, 2 * width:]

    lw = -DECAY_SCALE * jax.nn.sigmoid(w0_ref[...] + _mm(jnp.tanh(zl), w_up_ref[...]))
    a_sig = jax.nn.sigmoid(a0_ref[...] + _mm(zl, a_up_ref[...]))
    gate = _mm(jax.nn.sigmoid(zl), g_up_ref[...])

    seg = seg_ref[...]

    def head_sums(x):
        tiles = [_mm_split(x[:, t:t + SEG_LANES], seg) for t in range(0, width, SEG_LANES)]
        return jnp.concatenate(tiles, axis=1)

    kk = k * kk_ref[...]
    kk = kk / jnp.maximum(jnp.sqrt(head_sums(kk * kk)), 1e-12)
    k = k * (1.0 + (a_sig - 1.0) * ka_ref[...])
    bonus = head_sums(r * k * rk_ref[...]) * v
    a_in = -kk
    b_in = kk * a_sig

    lw_hi = lw.astype(BF16)
    lw_lo = (lw - lw_hi.astype(F32)).astype(BF16)
    tri = tri_ref[...]
    cum = (jnp.dot(tri, lw_hi, preferred_element_type=F32)
           + jnp.dot(tri, lw_lo, preferred_element_type=F32))
    cum_end = per_seq(lambda b: cum[(b + 1) * ch - 1:(b + 1) * ch, :])
    e_neg = jnp.exp(-cum)
    e_end = jnp.exp(cum_end - cum)

    n_units = n_seq * n_pairs

    def put(ref, val):
        for b in range(n_seq):
            for p in range(n_pairs):
                ref[b * n_pairs + p] = val[b * ch:b * ch + ref.shape[1], p * LANES:(p + 1) * LANES]

    put(rt_ref, r * jnp.exp(cum))
    put(at_ref, a_in * jnp.exp(cum - lw))
    put(bh_ref, b_in * e_neg)
    put(kh_ref, k * e_neg)
    put(bc_ref, b_in * e_end)
    put(kc_ref, k * e_end)
    put(v_ref, v)
    put(pc_ref, jnp.exp(cum_end))

    ti = lax.broadcasted_iota(jnp.int32, (ch, 2 * ch), 0)
    si = lax.broadcasted_iota(jnp.int32, (ch, 2 * ch), 1) % ch
    incl = ti >= si
    strict = ti > si
    lane = lax.broadcasted_iota(jnp.int32, (1, 2 * LANES), 1)
    head_a2 = (lane % LANES) < HEAD_DIM
    head_a = head_a2[:, :LANES]
    di = lax.broadcasted_iota(jnp.int32, (LANES, LANES), 0)
    dj = lax.broadcasted_iota(jnp.int32, (LANES, LANES), 1)
    same_head = (di < HEAD_DIM) == (dj < HEAD_DIM)
    diag = di == dj

    zero = jnp.zeros((), BF16)

    def stack(x, mask):
        x = x.astype(BF16)
        return jnp.concatenate([jnp.where(mask, x, zero), jnp.where(mask, zero, x)], axis=0)

    def block_diag(m):
        m = m.astype(BF16)
        return jnp.where(same_head, jnp.concatenate([m, m], axis=0), zero)

    pairs = range(n_units)
    zero_tile = jnp.zeros((2 * ch, LANES), BF16)
    a_rb, a_ab, a_rk, a_ak, vst = [], [], [], [], []
    for p in pairs:
        lhs = jnp.concatenate([rt_ref[p], at_ref[p]], axis=0)
        rhs = jnp.concatenate([stack(bh_ref[p], head_a), stack(kh_ref[p], head_a)], axis=0)
        a_bk = _mm_nt(lhs, rhs).astype(BF16)
        a_rb.append(jnp.where(incl, a_bk[:ch, :LANES], zero))
        a_ab.append(jnp.where(strict, a_bk[ch:, :LANES], zero))
        a_rk.append(jnp.where(incl, a_bk[:ch, LANES:], zero))
        a_ak.append(jnp.where(strict, a_bk[ch:, LANES:], zero))
        vst.append(stack(v_ref[p], head_a))

    xs = [jnp.concatenate([at_ref[p], _mm(a_ak[p], vst[p])], axis=1) for p in pairs]
    nk = a_ab
    for _ in range(int(math.log2(ch)) - 1):
        nk_next = [_mm(n, block_diag(n)) for n in nk]
        xs = [x + _mm(n, stack(x, head_a2)) for n, x in zip(nk, xs)]
        nk = nk_next
    xs = [x + _mm(n, stack(x, head_a2)) for n, x in zip(nk, xs)]

    ax = [_mm(jnp.concatenate([a_rb[p], a_rk[p]], axis=1),
              jnp.concatenate([stack(xs[p], head_a2), jnp.concatenate([zero_tile, vst[p]], axis=1)], axis=0))
          for p in pairs]
    mg = [_mm_tn(jnp.concatenate([bc_ref[p], kc_ref[p]], axis=0),
                 jnp.concatenate([xs[p], jnp.concatenate([jnp.zeros((ch, LANES), F32), v_ref[p]], axis=1)],
                                 axis=0))
          for p in pairs]
    for p in pairs:
        q_acc = rt_ref[p] + ax[p][:, :LANES]
        m_mat = jnp.where(same_head, mg[p][:, :LANES], 0.0) + jnp.where(diag, pc_ref[p], 0.0)
        g_mat = jnp.where(same_head, mg[p][:, LANES:], 0.0)
        res = _mm(jnp.concatenate([q_acc, m_mat], axis=0), state_ref[p])
        y_ref[p] = res[:ch] + ax[p][:, LANES:]
        state_ref[p] = res[ch:] + g_mat

    y = jnp.concatenate([jnp.concatenate([y_ref[b * n_pairs + p] for p in range(n_pairs)], axis=1)
                         for b in range(n_seq)], axis=0)
    inv_n = 1.0 / HEAD_DIM
    mu = head_sums(y) * inv_n
    d = y - mu
    var = head_sums(d * d) * inv_n
    yn = d * lax.rsqrt(var + GN_EPS) * lnw_ref[...] + lnb_ref[...]
    out_ref[...] = ((yn + bonus) * gate).reshape(out_ref.shape).astype(out_ref.dtype)


def _rwkv(rkv, lora, prm, *, batch, seq):
    t, w3 = rkv.shape
    width = w3 // 3
    n_lora = lora.shape[1]
    ch = CHUNK
    n_chunks = seq // ch
    n_pairs = width // LANES
    row = lambda n: pl.BlockSpec((1, n), lambda c: (0, 0))
    full = lambda a: pl.BlockSpec(a.shape, lambda c: (0,) * a.ndim)
    head_id = jnp.arange(SEG_LANES) // HEAD_DIM
    seg = (head_id[:, None] == head_id[None, :]).astype(BF16)
    pos = jnp.arange(batch * ch)
    tri = ((pos[:, None] >= pos[None, :]) & (pos[:, None] // ch == pos[None, :] // ch)).astype(BF16)
    n_units = batch * n_pairs
    big = lambda: pltpu.VMEM((n_units, ch, LANES), F32)
    args = (rkv.reshape(batch, seq, w3), lora.reshape(batch, seq, n_lora), prm["mix_rkv"], prm["mix_lora"],
            prm["w0"], prm["w_up"], prm["a0"], prm["a_up"], prm["g_up"], prm["k_k"], prm["k_a"], prm["r_k"],
            prm["ln_w"], prm["ln_b"], seg, tri)
    in_specs = [
        pl.BlockSpec((batch, ch, w3), lambda c: (0, c, 0)),
        pl.BlockSpec((batch, ch, n_lora), lambda c: (0, c, 0)),
        row(w3), row(n_lora), row(width), full(prm["w_up"]), row(width), full(prm["a_up"]),
        full(prm["g_up"]), row(width), row(width), row(width), row(width), row(width),
        full(seg), full(tri),
    ]
    out = pl.pallas_call(
        functools.partial(_rwkv_kernel, width=width),
        grid=(n_chunks,),
        in_specs=in_specs,
        out_specs=pl.BlockSpec((batch, ch, width), lambda c: (0, c, 0)),
        out_shape=jax.ShapeDtypeStruct((batch, seq, width), BF16),
        scratch_shapes=[
            pltpu.VMEM((n_units, LANES, LANES), F32),
            pltpu.VMEM((batch, 1, w3), F32),
            pltpu.VMEM((batch, 1, n_lora), F32),
            big(), big(), big(), big(), big(), big(), big(),
            pltpu.VMEM((n_units, 1, LANES), F32),
            big(),
        ],
        compiler_params=_params(("arbitrary",)),
        name="rwkv7",
    )(*args)
    return out.reshape(t, width)


def _out_proj_kernel(attn_ref, rw_ref, wa_ref, wr_ref, x_ref, out_ref):
    out_ref[...] = (x_ref[...] + jnp.dot(attn_ref[...], wa_ref[...], preferred_element_type=F32)
                    + jnp.dot(rw_ref[...], wr_ref[...], preferred_element_type=F32))


def _out_proj(attn, rw, w_out, x2, *, tm, tn):
    t, width = attn.shape
    d = w_out.shape[1]
    return pl.pallas_call(
        _out_proj_kernel,
        grid=(t // tm, d // tn),
        in_specs=[
            pl.BlockSpec((tm, width), lambda i, j: (i, 0)),
            pl.BlockSpec((tm, width), lambda i, j: (i, 0)),
            pl.BlockSpec((width, tn), lambda i, j: (0, j)),
            pl.BlockSpec((width, tn), lambda i, j: (1, j)),
            pl.BlockSpec((tm, tn), lambda i, j: (i, j)),
        ],
        out_specs=pl.BlockSpec((tm, tn), lambda i, j: (i, j)),
        out_shape=jax.ShapeDtypeStruct((t, d), F32),
        compiler_params=_params(("arbitrary", "arbitrary")),
        name="out_proj",
    )(attn, rw, w_out, w_out, x2)


def _ffn_up_kernel(h_ref, g_ref, wg_ref, wu_ref, act_ref, hn_ref):
    @pl.when(pl.program_id(1) == 0)
    def _():
        h = h_ref[...]
        ms = jnp.mean(h * h, axis=-1, keepdims=True)
        hn_ref[...] = (h * lax.rsqrt(ms + RMS_EPS) * g_ref[...]).astype(BF16)

    hn = hn_ref[...]
    gate = jnp.dot(hn, wg_ref[...], preferred_element_type=F32)
    up = jnp.dot(hn, wu_ref[...], preferred_element_type=F32)
    act_ref[...] = (gate * jax.nn.sigmoid(gate) * up).astype(BF16)


def _ffn_up(h, g, w_gate_up, *, tm, th):
    t, d = h.shape
    hidden = w_gate_up.shape[1] // 2
    n_h = hidden // th
    return pl.pallas_call(
        _ffn_up_kernel,
        grid=(t // tm, n_h),
        in_specs=[
            pl.BlockSpec((tm, d), lambda i, j: (i, 0)),
            pl.BlockSpec((1, d), lambda i, j: (0, 0)),
            pl.BlockSpec((d, th), lambda i, j: (0, j)),
            pl.BlockSpec((d, th), lambda i, j: (0, n_h + j)),
        ],
        out_specs=pl.BlockSpec((tm, th), lambda i, j: (i, j)),
        out_shape=jax.ShapeDtypeStruct((t, hidden), BF16),
        scratch_shapes=[pltpu.VMEM((tm, d), BF16)],
        compiler_params=_params(("arbitrary", "arbitrary")),
        name="ffn_up",
    )(h, g, w_gate_up, w_gate_up)


FFN_DOWN_COLS = 512


def _ffn_down_kernel(act_ref, w_ref, h_ref, g_ref, out_ref):
    k = pl.program_id(1)

    @pl.when(k == 0)
    def _():
        out_ref[...] = h_ref[...]

    act = act_ref[...]
    for c0 in range(0, out_ref.shape[1], FFN_DOWN_COLS):
        cols = slice(c0, c0 + FFN_DOWN_COLS)
        out_ref[:, cols] += jnp.dot(act, w_ref[:, cols], preferred_element_type=F32)

    @pl.when(k == pl.num_programs(1) - 1)
    def _():
        h = out_ref[...]
        ms = jnp.mean(h * h, axis=-1, keepdims=True)
        out_ref[...] = h * lax.rsqrt(ms + RMS_EPS) * g_ref[...]


def _ffn_down(act, w_down, h, g, *, tm, tk):
    t, hidden = act.shape
    d = w_down.shape[1]
    return pl.pallas_call(
        _ffn_down_kernel,
        grid=(t // tm, hidden // tk),
        in_specs=[
            pl.BlockSpec((tm, tk), lambda i, k: (i, k)),
            pl.BlockSpec((tk, d), lambda i, k: (k, 0)),
            pl.BlockSpec((tm, d), lambda i, k: (i, 0)),
            pl.BlockSpec((1, d), lambda i, k: (0, 0)),
        ],
        out_specs=pl.BlockSpec((tm, d), lambda i, k: (i, 0)),
        out_shape=jax.ShapeDtypeStruct((t, d), F32),
        compiler_params=_params(("arbitrary", "arbitrary")),
        name="ffn_down",
    )(act, w_down, h, g)


def _pad_rows(a, rows):
    return jnp.pad(a, ((0, rows - a.shape[0]), (0, 0)))


def _layer(h, norm1_g, w_in, shift_mix, w0, w_up, a0, a_up, g_up, k_k, k_a, r_k, ln_w, ln_b,
           rel_bias_table, w_out, norm2_g, w_gate_up, w_down, out_g, *, batch, seq, tile_m):
    d = h.shape[1]
    width = w_out.shape[0] // 2
    n_heads = width // HEAD_DIM
    n_main = 6 * width
    rank_w, rank_a, rank_g = w_up.shape[0], a_up.shape[0], g_up.shape[0]
    n_lora = rank_w + rank_a + rank_g
    n_lora_pad = -(-n_lora // LANES) * LANES

    qkv, rkv, lora = _in_proj(h, norm1_g.reshape(1, d), w_in.astype(BF16), n_main=n_main,
                              n_lora_pad=n_lora_pad, tm=tile_m, tn=3 * width // 4)

    attn = _attention(qkv, _band_rows(rel_bias_table), batch=batch, seq=seq, n_heads=n_heads)

    mix_rkv = shift_mix[:3 * width].reshape(1, 3 * width)
    mix_lora = jnp.pad(shift_mix[3 * width:], (0, n_lora_pad - n_lora)).reshape(1, n_lora_pad)
    zeros = lambda n: jnp.zeros((n, width), F32)
    prm = dict(
        mix_rkv=mix_rkv, mix_lora=mix_lora,
        w0=w0.reshape(1, width), a0=a0.reshape(1, width),
        w_up=_pad_rows(w_up, n_lora_pad).astype(BF16),
        a_up=_pad_rows(jnp.concatenate([zeros(rank_w), a_up]), n_lora_pad).astype(BF16),
        g_up=_pad_rows(jnp.concatenate([zeros(rank_w + rank_a), g_up]), n_lora_pad).astype(BF16),
        k_k=k_k.reshape(1, width), k_a=k_a.reshape(1, width), r_k=r_k.reshape(1, width),
        ln_w=ln_w.reshape(1, width), ln_b=ln_b.reshape(1, width),
    )
    rw = _rwkv(rkv, lora, prm, batch=batch, seq=seq)

    h1 = _out_proj(attn, rw, w_out.astype(BF16), h, tm=tile_m, tn=d // 2)
    act = _ffn_up(h1, norm2_g.reshape(1, d), w_gate_up.astype(BF16), tm=tile_m, th=512)
    hidden = w_down.shape[0]
    return _ffn_down(act, w_down.astype(BF16), h1, out_g.reshape(1, d), tm=tile_m, tk=hidden // 4)


def kernel(x, norm1_g, w_in, rwkv_shift_mix, rwkv_w0, rwkv_w_up, rwkv_a0, rwkv_a_up, rwkv_g_up, rwkv_k_k, rwkv_k_a, rwkv_r_k, rwkv_ln_w, rwkv_ln_b, rel_bias_table, w_out, norm2_g, w_gate_up, w_down, final_g):
    batch, seq, d = x.shape
    depth = w_in.shape[0]
    assert depth == 1, "the final RMSNorm is fused into the last layer's FFN kernel"
    assert seq % DILATED_PATTERNS[-1][0] == 0 and seq % CHUNK == 0
    h = x.reshape(batch * seq, d)
    out = _layer(h, norm1_g[0], w_in[0], rwkv_shift_mix[0], rwkv_w0[0], rwkv_w_up[0], rwkv_a0[0],
                 rwkv_a_up[0], rwkv_g_up[0], rwkv_k_k[0], rwkv_k_a[0], rwkv_r_k[0], rwkv_ln_w[0],
                 rwkv_ln_b[0], rel_bias_table, w_out[0], norm2_g[0], w_gate_up[0], w_down[0], final_g,
                 batch=batch, seq=seq, tile_m=min(1024, batch * seq))
    return out.reshape(batch, seq, d)
```

```python
import functools
import math

import jax
import jax.numpy as jnp
from jax import lax
from jax.experimental import pallas as pl
from jax.experimental.pallas import tpu as pltpu

F32 = jnp.float32
BF16 = jnp.bfloat16

HEAD_DIM = 64
LANES = 128
DILATED_PATTERNS = ((128, 1), (512, 4), (2048, 16))
REL_BUCKETS = 32
REL_MAX_DIST = 2048
RMS_EPS = 1e-6
GN_EPS = 64e-5
DECAY_SCALE = math.exp(-0.5)
ATTN_SCALE = HEAD_DIM ** -0.5
MASK_VALUE = -1e30
LOG2_E = math.log2(math.e)
LN_2 = math.log(2.0)
CHUNK = 64
SEG_LANES = 256
VMEM_LIMIT = 56 * 1024 * 1024
FFN_VMEM_LIMIT = 60 * 1024 * 1024


def _mm(a, b):
    return jnp.dot(a.astype(BF16), b.astype(BF16), preferred_element_type=F32)


def _mm_nt(a, b):
    return lax.dot_general(a.astype(BF16), b.astype(BF16), (((1,), (1,)), ((), ())),
                           preferred_element_type=F32)


def _mm_tn(a, b):
    return lax.dot_general(a.astype(BF16), b.astype(BF16), (((0,), (0,)), ((), ())),
                           preferred_element_type=F32)


def _mm_split(a, b_exact):
    hi = a.astype(BF16)
    lo = (a - hi.astype(F32)).astype(BF16)
    return (jnp.dot(hi, b_exact, preferred_element_type=F32)
            + jnp.dot(lo, b_exact, preferred_element_type=F32))


def _params(semantics):
    return pltpu.CompilerParams(dimension_semantics=semantics, vmem_limit_bytes=VMEM_LIMIT)


def _in_proj_kernel(x_ref, g_ref, w_ref, wl_ref, qkv_ref, rkv_ref, lora_ref, xn_ref, *, n_qkv, n_lora):
    j = pl.program_id(1)

    @pl.when(j == 0)
    def _():
        x = x_ref[...]
        ms = jnp.mean(x * x, axis=-1, keepdims=True)
        xn_ref[...] = (x * lax.rsqrt(ms + RMS_EPS) * g_ref[...]).astype(BF16)

    @pl.when(j < n_qkv)
    def _():
        qkv_ref[...] = jnp.dot(xn_ref[...], w_ref[...], preferred_element_type=F32)

    @pl.when((j >= n_qkv) & (j < 2 * n_qkv))
    def _():
        rkv_ref[...] = jnp.dot(xn_ref[...], w_ref[...], preferred_element_type=F32)

    @pl.when(j == 2 * n_qkv)
    def _():
        col = lax.broadcasted_iota(jnp.int32, (1, wl_ref.shape[1]), 1)
        wl = jnp.where(col < n_lora, wl_ref[...], jnp.zeros((), BF16))
        lora_ref[...] = jnp.dot(xn_ref[...], wl, preferred_element_type=F32)


def _in_proj(x2, g, w, *, n_main, n_lora_pad, tm, tn):
    t, d = x2.shape
    n_lora = w.shape[1] - n_main
    n_qkv = (n_main // 2) // tn
    last = 2 * n_qkv - 1
    assert n_main % n_lora_pad == 0 and n_lora <= n_lora_pad
    return pl.pallas_call(
        functools.partial(_in_proj_kernel, n_qkv=n_qkv, n_lora=n_lora),
        grid=(t // tm, 2 * n_qkv + 1),
        in_specs=[
            pl.BlockSpec((tm, d), lambda i, j: (i, 0)),
            pl.BlockSpec((1, d), lambda i, j: (0, 0)),
            pl.BlockSpec((d, tn), lambda i, j: (0, jnp.minimum(j, last))),
            pl.BlockSpec((d, n_lora_pad), lambda i, j: (0, n_main // n_lora_pad)),
        ],
        out_specs=[
            pl.BlockSpec((tm, tn), lambda i, j: (i, jnp.minimum(j, n_qkv - 1))),
            pl.BlockSpec((tm, tn), lambda i, j: (i, jnp.clip(j - n_qkv, 0, n_qkv - 1))),
            pl.BlockSpec((tm, n_lora_pad), lambda i, j: (i, 0)),
        ],
        out_shape=[
            jax.ShapeDtypeStruct((t, n_main // 2), F32),
            jax.ShapeDtypeStruct((t, n_main // 2), F32),
            jax.ShapeDtypeStruct((t, n_lora_pad), F32),
        ],
        scratch_shapes=[pltpu.VMEM((tm, d), BF16)],
        compiler_params=_params(("arbitrary", "arbitrary")),
        name="in_proj",
    )(x2, g, w, w)


ATTN_BLK = 128
ATTN_ROWS = max(w for w, _ in DILATED_PATTERNS)
ATTN_GROUP = 4


def _attn_kernel(q_ref, kp_ref, kc_ref, vp_ref, vc_ref, band_ref, out_ref, bias_ref, o_ref, lse_ref,
                 *, n_heads):
    b, n, hp = pl.program_id(0), pl.program_id(1), pl.program_id(2)
    blk = ATTN_BLK

    @pl.when((b == 0) & (n == 0) & (hp == 0))
    def _():
        for i in range(bias_ref.shape[0]):
            row = jnp.broadcast_to(band_ref[i:i + 1, :], (blk, 2 * blk))
            bias_ref[i] = pltpu.roll(row, 0, 1, stride=1, stride_axis=0)

    lane = lax.broadcasted_iota(jnp.int32, (1, LANES), 1)
    in_head = [(lane >= half * HEAD_DIM) & (lane < (half + 1) * HEAD_DIM) for half in range(2)]
    key_idx = lax.broadcasted_iota(jnp.int32, (1, 2 * blk), 1)
    first_ok = (key_idx >= blk) | (n > 0)
    ones_tile = jnp.ones((2 * blk, LANES), BF16)
    zero = jnp.zeros((), BF16)

    for bi, (window, dilation) in enumerate(DILATED_PATTERNS):
        units = [(sub, r) for sub in range(ATTN_ROWS // window) for r in range(dilation)]
        for g0 in range(0, len(units), ATTN_GROUP):
            group = units[g0:g0 + ATTN_GROUP]
            rows, q2, k2, v2, key_ok = [], [], [], [], []
            for sub, r in group:
                cur = pl.ds(sub * window + r, blk, stride=dilation)
                if sub == 0:
                    prev_k, prev_v = kp_ref, vp_ref
                    prev = pl.ds(ATTN_ROWS - window + r, blk, stride=dilation)
                else:
                    prev_k, prev_v = kc_ref, vc_ref
                    prev = pl.ds((sub - 1) * window + r, blk, stride=dilation)
                rows.append(cur)
                q2.append((q_ref[cur, :] * (ATTN_SCALE * LOG2_E)).astype(BF16))
                k2.append(jnp.concatenate([prev_k[prev, :], kc_ref[cur, :]], axis=0).astype(BF16))
                v = jnp.concatenate([prev_v[prev, :], vc_ref[cur, :]], axis=0).astype(BF16)
                v2.append(jnp.concatenate([v, ones_tile], axis=1))
                key_ok.append(first_ok if sub == 0 else None)
            s = [[_mm_nt(jnp.where(m, q, zero), k) for m in in_head] for q, k in zip(q2, k2)]
            outs = []
            for u in range(len(group)):
                halves = []
                for half in range(2):
                    su = s[u][half] + bias_ref[(bi * n_heads) + 2 * hp + half]
                    if key_ok[u] is not None:
                        su = jnp.where(key_ok[u], su, MASK_VALUE)
                    mx = jnp.max(su, axis=-1, keepdims=True)
                    halves.append((jnp.exp2(su - mx).astype(BF16), mx))
                outs.append(halves)
            for u in range(len(group)):
                (pa, ma), (pb, mb) = outs[u]
                ra = jnp.dot(pa, v2[u], preferred_element_type=F32)
                rb = jnp.dot(pb, v2[u], preferred_element_type=F32)
                la, lb = ra[:, LANES:], rb[:, LANES:]
                o_ref[bi, rows[u], :] = jnp.where(in_head[0], ra[:, :LANES] / la, rb[:, :LANES] / lb)
                lse_ref[bi, rows[u], :] = jnp.where(in_head[0], ma * LN_2 + jnp.log(la), mb * LN_2 + jnp.log(lb))

    lses = [lse_ref[bi] for bi in range(len(DILATED_PATTERNS))]
    top = functools.reduce(jnp.maximum, lses)
    es = [jnp.exp(l - top) for l in lses]
    num = functools.reduce(jnp.add, [e * o_ref[bi] for bi, e in enumerate(es)])
    out_ref[...] = (num / functools.reduce(jnp.add, es)).astype(out_ref.dtype)


def _attention(qkv, band, *, batch, seq, n_heads):
    width = n_heads * HEAD_DIM
    n_pairs = width // LANES
    nb = seq // ATTN_ROWS
    n_pat = len(DILATED_PATTERNS)
    spec = lambda f: pl.BlockSpec((ATTN_ROWS, LANES), f)
    prev = lambda b, n: b * nb + jnp.maximum(n - 1, 0)
    return pl.pallas_call(
        functools.partial(_attn_kernel, n_heads=n_heads),
        grid=(batch, nb, n_pairs),
        in_specs=[
            spec(lambda b, n, hp: (b * nb + n, hp)),
            spec(lambda b, n, hp: (prev(b, n), n_pairs + hp)),
            spec(lambda b, n, hp: (b * nb + n, n_pairs + hp)),
            spec(lambda b, n, hp: (prev(b, n), 2 * n_pairs + hp)),
            spec(lambda b, n, hp: (b * nb + n, 2 * n_pairs + hp)),
            pl.BlockSpec(band.shape, lambda b, n, hp: (0, 0)),
        ],
        out_specs=spec(lambda b, n, hp: (b * nb + n, hp)),
        out_shape=jax.ShapeDtypeStruct((batch * seq, width), BF16),
        scratch_shapes=[
            pltpu.VMEM((n_pat * n_heads, ATTN_BLK, 2 * ATTN_BLK), F32),
            pltpu.VMEM((n_pat, ATTN_ROWS, LANES), F32),
            pltpu.VMEM((n_pat, ATTN_ROWS, LANES), F32),
        ],
        compiler_params=_params(("arbitrary", "arbitrary", "arbitrary")),
        name="dilated_attn",
    )(qkv, qkv, qkv, qkv, qkv, band)


def _t5_bucket(dist):
    exact = REL_BUCKETS // 2
    d_f = jnp.maximum(dist, 1).astype(F32)
    large = exact + (jnp.log(d_f / exact) / math.log(REL_MAX_DIST / exact)
                     * (REL_BUCKETS - exact)).astype(jnp.int32)
    large = jnp.minimum(large, REL_BUCKETS - 1)
    return jnp.where(dist < exact, dist, large)


def _band_rows(bias_table):
    blk = ATTN_BLK
    rel = blk - jnp.arange(2 * blk)
    band = (rel >= 0) & (rel <= blk)
    rows = []
    for _, dilation in DILATED_PATTERNS:
        bias = bias_table[_t5_bucket(jnp.clip(rel, 0, blk) * dilation)]
        rows.append(jnp.where(band[:, None], bias.astype(F32) * LOG2_E, MASK_VALUE).T)
    return jnp.concatenate(rows, axis=0)


def _rwkv_kernel(rkv_ref, lora_ref, mix_rkv_ref, mix_lora_ref, w0_ref, w_up_ref, a0_ref, a_up_ref,
                 g_up_ref, kk_ref, ka_ref, rk_ref, lnw_ref, lnb_ref, seg_ref, tri_ref,
                 out_ref,
                 state_ref, carry_rkv_ref, carry_lora_ref,
                 rt_ref, at_ref, bh_ref, kh_ref, bc_ref, kc_ref, v_ref, pc_ref, y_ref, *, width):
    c = pl.program_id(0)
    n_pairs = width // LANES
    n_seq, ch = rkv_ref.shape[:2]
    rows = n_seq * ch

    @pl.when(c == 0)
    def _():
        state_ref[...] = jnp.zeros_like(state_ref)
        carry_rkv_ref[...] = jnp.zeros_like(carry_rkv_ref)
        carry_lora_ref[...] = jnp.zeros_like(carry_lora_ref)

    row = lax.broadcasted_iota(jnp.int32, (rows, 1), 0) % ch

    def per_seq(row_of):
        return jnp.concatenate([jnp.broadcast_to(row_of(b), (ch, row_of(b).shape[-1]))
                                for b in range(n_seq)], axis=0)

    def token_shift(z_ref, carry_ref, mix):
        z = z_ref[...].reshape(rows, z_ref.shape[-1])
        prev = jnp.where(row == 0, per_seq(lambda b: carry_ref[b]), pltpu.roll(z, 1, axis=0))
        for b in range(n_seq):
            carry_ref[b] = z[(b + 1) * ch - 1:(b + 1) * ch, :]
        return z + (prev - z) * mix

    z = token_shift(rkv_ref, carry_rkv_ref, mix_rkv_ref[...])
    zl = token_shift(lora_ref, carry_lora_ref, mix_lora_ref[...])
    r, k, v = z[:, :width], z[:, width:2 * width], z[:, 2 * width:]

    lw = -DECAY_SCALE * jax.nn.sigmoid(w0_ref[...] + _mm(jnp.tanh(zl), w_up_ref[...]))
    a_sig = jax.nn.sigmoid(a0_ref[...] + _mm(zl, a_up_ref[...]))
    gate = _mm(jax.nn.sigmoid(zl), g_up_ref[...])

    seg = seg_ref[...]

    def head_sums(x):
        tiles = [_mm_split(x[:, t:t + SEG_LANES], seg) for t in range(0, width, SEG_LANES)]
        return jnp.concatenate(tiles, axis=1)

    kk = k * kk_ref[...]
    kk = kk / jnp.maximum(jnp.sqrt(head_sums(kk * kk)), 1e-12)
    k = k * (1.0 + (a_sig - 1.0) * ka_ref[...])
    bonus = head_sums(r * k * rk_ref[...]) * v
    a_in = -kk
    b_in = kk * a_sig

    lw_hi = lw.astype(BF16)
    lw_lo = (lw - lw_hi.astype(F32)).astype(BF16)
    tri = tri_ref[...]
    cum = (jnp.dot(tri, lw_hi, preferred_element_type=F32)
           + jnp.dot(tri, lw_lo, preferred_element_type=F32))
    cum_end = per_seq(lambda b: cum[(b + 1) * ch - 1:(b + 1) * ch, :])
    e_neg = jnp.exp(-cum)
    e_end = jnp.exp(cum_end - cum)

    n_units = n_seq * n_pairs

    def put(ref, val):
        for b in range(n_seq):
            for p in range(n_pairs):
                ref[b * n_pairs + p] = val[b * ch:b * ch + ref.shape[1], p * LANES:(p + 1) * LANES]

    put(rt_ref, r * jnp.exp(cum))
    put(at_ref, a_in * jnp.exp(cum - lw))
    put(bh_ref, b_in * e_neg)
    put(kh_ref, k * e_neg)
    put(bc_ref, b_in * e_end)
    put(kc_ref, k * e_end)
    put(v_ref, v)
    put(pc_ref, jnp.exp(cum_end))

    ti = lax.broadcasted_iota(jnp.int32, (ch, 2 * ch), 0)
    si = lax.broadcasted_iota(jnp.int32, (ch, 2 * ch), 1) % ch
    incl = ti >= si
    strict = ti > si
    lane = lax.broadcasted_iota(jnp.int32, (1, 2 * LANES), 1)
    head_a2 = (lane % LANES) < HEAD_DIM
    head_a = head_a2[:, :LANES]
    di = lax.broadcasted_iota(jnp.int32, (LANES, LANES), 0)
    dj = lax.broadcasted_iota(jnp.int32, (LANES, LANES), 1)
    same_head = (di < HEAD_DIM) == (dj < HEAD_DIM)
    diag = di == dj

    zero = jnp.zeros((), BF16)

    def stack(x, mask):
        x = x.astype(BF16)
        return jnp.concatenate([jnp.where(mask, x, zero), jnp.where(mask, zero, x)], axis=0)

    def block_diag(m):
        m = m.astype(BF16)
        return jnp.where(same_head, jnp.concatenate([m, m], axis=0), zero)

    pairs = range(n_units)
    zero_tile = jnp.zeros((2 * ch, LANES), BF16)
    a_rb, a_ab, a_rk, a_ak, vst = [], [], [], [], []
    for p in pairs:
        lhs = jnp.concatenate([rt_ref[p], at_ref[p]], axis=0)
        rhs = jnp.concatenate([stack(bh_ref[p], head_a), stack(kh_ref[p], head_a)], axis=0)
        a_bk = _mm_nt(lhs, rhs).astype(BF16)
        a_rb.append(jnp.where(incl, a_bk[:ch, :LANES], zero))
        a_ab.append(jnp.where(strict, a_bk[ch:, :LANES], zero))
        a_rk.append(jnp.where(incl, a_bk[:ch, LANES:], zero))
        a_ak.append(jnp.where(strict, a_bk[ch:, LANES:], zero))
        vst.append(stack(v_ref[p], head_a))

    xs = [jnp.concatenate([at_ref[p], _mm(a_ak[p], vst[p])], axis=1) for p in pairs]
    nk = a_ab
    for _ in range(int(math.log2(ch)) - 1):
        nk_next = [_mm(n, block_diag(n)) for n in nk]
        xs = [x + _mm(n, stack(x, head_a2)) for n, x in zip(nk, xs)]
        nk = nk_next
    xs = [x + _mm(n, stack(x, head_a2)) for n, x in zip(nk, xs)]

    ax = [_mm(jnp.concatenate([a_rb[p], a_rk[p]], axis=1),
              jnp.concatenate([stack(xs[p], head_a2), jnp.concatenate([zero_tile, vst[p]], axis=1)], axis=0))
          for p in pairs]
    mg = [_mm_tn(jnp.concatenate([bc_ref[p], kc_ref[p]], axis=0),
                 jnp.concatenate([xs[p], jnp.concatenate([jnp.zeros((ch, LANES), F32), v_ref[p]], axis=1)],
                                 axis=0))
          for p in pairs]
    for p in pairs:
        q_acc = rt_ref[p] + ax[p][:, :LANES]
        m_mat = jnp.where(same_head, mg[p][:, :LANES], 0.0) + jnp.where(diag, pc_ref[p], 0.0)
        g_mat = jnp.where(same_head, mg[p][:, LANES:], 0.0)
        res = _mm(jnp.concatenate([q_acc, m_mat], axis=0), state_ref[p])
        y_ref[p] = res[:ch] + ax[p][:, LANES:]
        state_ref[p] = res[ch:] + g_mat

    y = jnp.concatenate([jnp.concatenate([y_ref[b * n_pairs + p] for p in range(n_pairs)], axis=1)
                         for b in range(n_seq)], axis=0)
    inv_n = 1.0 / HEAD_DIM
    mu = head_sums(y) * inv_n
    d = y - mu
    var = head_sums(d * d) * inv_n
    yn = d * lax.rsqrt(var + GN_EPS) * lnw_ref[...] + lnb_ref[...]
    out_ref[...] = ((yn + bonus) * gate).reshape(out_ref.shape).astype(out_ref.dtype)


def _rwkv(rkv, lora, prm, *, batch, seq):
    t, w3 = rkv.shape
    width = w3 // 3
    n_lora = lora.shape[1]
    ch = CHUNK
    n_chunks = seq // ch
    n_pairs = width // LANES
    row = lambda n: pl.BlockSpec((1, n), lambda c: (0, 0))
    full = lambda a: pl.BlockSpec(a.shape, lambda c: (0,) * a.ndim)
    head_id = jnp.arange(SEG_LANES) // HEAD_DIM
    seg = (head_id[:, None] == head_id[None, :]).astype(BF16)
    pos = jnp.arange(batch * ch)
    tri = ((pos[:, None] >= pos[None, :]) & (pos[:, None] // ch == pos[None, :] // ch)).astype(BF16)
    n_units = batch * n_pairs
    big = lambda: pltpu.VMEM((n_units, ch, LANES), F32)
    args = (rkv.reshape(batch, seq, w3), lora.reshape(batch, seq, n_lora), prm["mix_rkv"], prm["mix_lora"],
            prm["w0"], prm["w_up"], prm["a0"], prm["a_up"], prm["g_up"], prm["k_k"], prm["k_a"], prm["r_k"],
            prm["ln_w"], prm["ln_b"], seg, tri)
    in_specs = [
        pl.BlockSpec((batch, ch, w3), lambda c: (0, c, 0)),
        pl.BlockSpec((batch, ch, n_lora), lambda c: (0, c, 0)),
        row(w3), row(n_lora), row(width), full(prm["w_up"]), row(width), full(prm["a_up"]),
        full(prm["g_up"]), row(width), row(width), row(width), row(width), row(width),
        full(seg), full(tri),
    ]
    out = pl.pallas_call(
        functools.partial(_rwkv_kernel, width=width),
        grid=(n_chunks,),
        in_specs=in_specs,
        out_specs=pl.BlockSpec((batch, ch, width), lambda c: (0, c, 0)),
        out_shape=jax.ShapeDtypeStruct((batch, seq, width), BF16),
        scratch_shapes=[
            pltpu.VMEM((n_units, LANES, LANES), F32),
            pltpu.VMEM((batch, 1, w3), F32),
            pltpu.VMEM((batch, 1, n_lora), F32),
            big(), big(), big(), big(), big(), big(), big(),
            pltpu.VMEM((n_units, 1, LANES), F32),
            big(),
        ],
        compiler_params=_params(("arbitrary",)),
        name="rwkv7",
    )(*args)
    return out.reshape(t, width)


def _out_proj_kernel(attn_ref, rw_ref, wa_ref, wr_ref, x_ref, out_ref):
    out_ref[...] = (x_ref[...] + jnp.dot(attn_ref[...], wa_ref[...], preferred_element_type=F32)
                    + jnp.dot(rw_ref[...], wr_ref[...], preferred_element_type=F32))


def _out_proj(attn, rw, w_out, x2, *, tm, tn):
    t, width = attn.shape
    d = w_out.shape[1]
    return pl.pallas_call(
        _out_proj_kernel,
        grid=(t // tm, d // tn),
        in_specs=[
            pl.BlockSpec((tm, width), lambda i, j: (i, 0)),
            pl.BlockSpec((tm, width), lambda i, j: (i, 0)),
            pl.BlockSpec((width, tn), lambda i, j: (0, j)),
            pl.BlockSpec((width, tn), lambda i, j: (1, j)),
            pl.BlockSpec((tm, tn), lambda i, j: (i, j)),
        ],
        out_specs=pl.BlockSpec((tm, tn), lambda i, j: (i, j)),
        out_shape=jax.ShapeDtypeStruct((t, d), F32),
        compiler_params=_params(("arbitrary", "arbitrary")),
        name="out_proj",
    )(attn, rw, w_out, w_out, x2)


def _ffn_up_kernel(h_ref, g_ref, wg_ref, wu_ref, act_ref, hn_ref):
    @pl.when(pl.program_id(1) == 0)
    def _():
        h = h_ref[...]
        ms = jnp.mean(h * h, axis=-1, keepdims=True)
        hn_ref[...] = (h * lax.rsqrt(ms + RMS_EPS) * g_ref[...]).astype(BF16)

    hn = hn_ref[...]
    gate = jnp.dot(hn, wg_ref[...], preferred_element_type=F32)
    up = jnp.dot(hn, wu_ref[...], preferred_element_type=F32)
    act_ref[...] = (gate * jax.nn.sigmoid(gate) * up).astype(BF16)


def _ffn_up(h, g, w_gate_up, *, tm, th):
    t, d = h.shape
    hidden = w_gate_up.shape[1] // 2
    n_h = hidden // th
    return pl.pallas_call(
        _ffn_up_kernel,
        grid=(t // tm, n_h),
        in_specs=[
            pl.BlockSpec((tm, d), lambda i, j: (i, 0)),
            pl.BlockSpec((1, d), lambda i, j: (0, 0)),
            pl.BlockSpec((d, th), lambda i, j: (0, j)),
            pl.BlockSpec((d, th), lambda i, j: (0, n_h + j)),
        ],
        out_specs=pl.BlockSpec((tm, th), lambda i, j: (i, j)),
        out_shape=jax.ShapeDtypeStruct((t, hidden), BF16),
        scratch_shapes=[pltpu.VMEM((tm, d), BF16)],
        compiler_params=_params(("arbitrary", "arbitrary")),
        name="ffn_up",
    )(h, g, w_gate_up, w_gate_up)


FFN_DOWN_COLS = 512


def _ffn_down_kernel(act_ref, w_ref, h_ref, g_ref, out_ref):
    k = pl.program_id(1)

    @pl.when(k == 0)
    def _():
        out_ref[...] = h_ref[...]

    act = act_ref[...]
    for c0 in range(0, out_ref.shape[1], FFN_DOWN_COLS):
        cols = slice(c0, c0 + FFN_DOWN_COLS)
        out_ref[:, cols] += jnp.dot(act, w_ref[:, cols], preferred_element_type=F32)

    @pl.when(k == pl.num_programs(1) - 1)
    def _():
        h = out_ref[...]
        ms = jnp.mean(h * h, axis=-1, keepdims=True)
        out_ref[...] = h * lax.rsqrt(ms + RMS_EPS) * g_ref[...]


def _ffn_down(act, w_down, h, g, *, tm, tk):
    t, hidden = act.shape
    d = w_down.shape[1]
    return pl.pallas_call(
        _ffn_down_kernel,
        grid=(t // tm, hidden // tk),
        in_specs=[
            pl.BlockSpec((tm, tk), lambda i, k: (i, k)),
            pl.BlockSpec((tk, d), lambda i, k: (k, 0)),
            pl.BlockSpec((tm, d), lambda i, k: (i, 0)),
            pl.BlockSpec((1, d), lambda i, k: (0, 0)),
        ],
        out_specs=pl.BlockSpec((tm, d), lambda i, k: (i, 0)),
        out_shape=jax.ShapeDtypeStruct((t, d), F32),
        compiler_params=_params(("arbitrary", "arbitrary")),
        name="ffn_down",
    )(act, w_down, h, g)


def _ffn_kernel(h_ref, g_ref, wg_ref, wu_ref, wd_ref, og_ref, out_ref, hn_ref):
    j = pl.program_id(1)

    @pl.when(j == 0)
    def _():
        h = h_ref[...]
        ms = jnp.mean(h * h, axis=-1, keepdims=True)
        hn_ref[...] = (h * lax.rsqrt(ms + RMS_EPS) * g_ref[...]).astype(BF16)
        out_ref[...] = h

    hn = hn_ref[...]
    gate = jnp.dot(hn, wg_ref[...], preferred_element_type=F32)
    up = jnp.dot(hn, wu_ref[...], preferred_element_type=F32)
    act = (gate * jax.nn.sigmoid(gate) * up).astype(BF16)
    for c0 in range(0, out_ref.shape[1], FFN_DOWN_COLS):
        cols = slice(c0, c0 + FFN_DOWN_COLS)
        out_ref[:, cols] += jnp.dot(act, wd_ref[:, cols], preferred_element_type=F32)

    @pl.when(j == pl.num_programs(1) - 1)
    def _():
        h = out_ref[...]
        ms = jnp.mean(h * h, axis=-1, keepdims=True)
        out_ref[...] = h * lax.rsqrt(ms + RMS_EPS) * og_ref[...]


def _ffn(h, g, w_gate_up, w_down, out_g, *, tm, th):
    t, d = h.shape
    hidden = w_down.shape[0]
    n_h = hidden // th
    return pl.pallas_call(
        _ffn_kernel,
        grid=(t // tm, n_h),
        in_specs=[
            pl.BlockSpec((tm, d), lambda i, j: (i, 0)),
            pl.BlockSpec((1, d), lambda i, j: (0, 0)),
            pl.BlockSpec((d, th), lambda i, j: (0, j)),
            pl.BlockSpec((d, th), lambda i, j: (0, n_h + j)),
            pl.BlockSpec((th, d), lambda i, j: (j, 0)),
            pl.BlockSpec((1, d), lambda i, j: (0, 0)),
        ],
        out_specs=pl.BlockSpec((tm, d), lambda i, j: (i, 0)),
        out_shape=jax.ShapeDtypeStruct((t, d), F32),
        scratch_shapes=[pltpu.VMEM((tm, d), BF16)],
        compiler_params=pltpu.CompilerParams(dimension_semantics=("arbitrary", "arbitrary"),
                                             vmem_limit_bytes=FFN_VMEM_LIMIT),
        name="ffn",
    )(h, g, w_gate_up, w_gate_up, w_down, out_g)


def _pad_rows(a, rows):
    return jnp.pad(a, ((0, rows - a.shape[0]), (0, 0)))


def _layer(h, norm1_g, w_in, shift_mix, w0, w_up, a0, a_up, g_up, k_k, k_a, r_k, ln_w, ln_b,
           rel_bias_table, w_out, norm2_g, w_gate_up, w_down, out_g, *, batch, seq, tile_m):
    d = h.shape[1]
    width = w_out.shape[0] // 2
    n_heads = width // HEAD_DIM
    n_main = 6 * width
    rank_w, rank_a, rank_g = w_up.shape[0], a_up.shape[0], g_up.shape[0]
    n_lora = rank_w + rank_a + rank_g
    n_lora_pad = -(-n_lora // LANES) * LANES

    qkv, rkv, lora = _in_proj(h, norm1_g.reshape(1, d), w_in.astype(BF16), n_main=n_main,
                              n_lora_pad=n_lora_pad, tm=tile_m, tn=width)

    attn = _attention(qkv, _band_rows(rel_bias_table), batch=batch, seq=seq, n_heads=n_heads)

    mix_rkv = shift_mix[:3 * width].reshape(1, 3 * width)
    mix_lora = jnp.pad(shift_mix[3 * width:], (0, n_lora_pad - n_lora)).reshape(1, n_lora_pad)
    zeros = lambda n: jnp.zeros((n, width), F32)
    prm = dict(
        mix_rkv=mix_rkv, mix_lora=mix_lora,
        w0=w0.reshape(1, width), a0=a0.reshape(1, width),
        w_up=_pad_rows(w_up, n_lora_pad).astype(BF16),
        a_up=_pad_rows(jnp.concatenate([zeros(rank_w), a_up]), n_lora_pad).astype(BF16),
        g_up=_pad_rows(jnp.concatenate([zeros(rank_w + rank_a), g_up]), n_lora_pad).astype(BF16),
        k_k=k_k.reshape(1, width), k_a=k_a.reshape(1, width), r_k=r_k.reshape(1, width),
        ln_w=ln_w.reshape(1, width), ln_b=ln_b.reshape(1, width),
    )
    rw = _rwkv(rkv, lora, prm, batch=batch, seq=seq)

    h1 = _out_proj(attn, rw, w_out.astype(BF16), h, tm=tile_m, tn=d // 2)
    return _ffn(h1, norm2_g.reshape(1, d), w_gate_up.astype(BF16), w_down.astype(BF16), out_g.reshape(1, d),
                tm=tile_m, th=512)


def kernel(x, norm1_g, w_in, rwkv_shift_mix, rwkv_w0, rwkv_w_up, rwkv_a0, rwkv_a_up, rwkv_g_up, rwkv_k_k, rwkv_k_a, rwkv_r_k, rwkv_ln_w, rwkv_ln_b, rel_bias_table, w_out, norm2_g, w_gate_up, w_down, final_g):
    batch, seq, d = x.shape
    depth = w_in.shape[0]
    assert depth == 1, "the final RMSNorm is fused into the last layer's FFN kernel"
    assert seq % DILATED_PATTERNS[-1][0] == 0 and seq % CHUNK == 0
    h = x.reshape(batch * seq, d)
    out = _layer(h, norm1_g[0], w_in[0], rwkv_shift_mix[0], rwkv_w0[0], rwkv_w_up[0], rwkv_a0[0],
                 rwkv_a_up[0], rwkv_g_up[0], rwkv_k_k[0], rwkv_k_a[0], rwkv_r_k[0], rwkv_ln_w[0],
                 rwkv_ln_b[0], rel_bias_table, w_out[0], norm2_g[0], w_gate_up[0], w_down[0], final_g,
                 batch=batch, seq=seq, tile_m=min(1024, batch * seq))
    return out.reshape(batch, seq, d)
```

```python
import functools
import math

import jax
import jax.numpy as jnp
from jax import lax
from jax.experimental import pallas as pl
from jax.experimental.pallas import tpu as pltpu

F32 = jnp.float32
BF16 = jnp.bfloat16

HEAD_DIM = 64
LANES = 128
DILATED_PATTERNS = ((128, 1), (512, 4), (2048, 16))
REL_BUCKETS = 32
REL_MAX_DIST = 2048
RMS_EPS = 1e-6
GN_EPS = 64e-5
DECAY_SCALE = math.exp(-0.5)
ATTN_SCALE = HEAD_DIM ** -0.5
MASK_VALUE = -1e30
LOG2_E = math.log2(math.e)
CHUNK = 64
SEG_LANES = 256
VMEM_LIMIT = 56 * 1024 * 1024
FFN_VMEM_LIMIT = 60 * 1024 * 1024


def _mm(a, b):
    return jnp.dot(a.astype(BF16), b.astype(BF16), preferred_element_type=F32)


def _mm_nt(a, b):
    return lax.dot_general(a.astype(BF16), b.astype(BF16), (((1,), (1,)), ((), ())),
                           preferred_element_type=F32)


def _mm_tn(a, b):
    return lax.dot_general(a.astype(BF16), b.astype(BF16), (((0,), (0,)), ((), ())),
                           preferred_element_type=F32)


def _mm_split(a, b_exact):
    hi = a.astype(BF16)
    lo = (a - hi.astype(F32)).astype(BF16)
    return (jnp.dot(hi, b_exact, preferred_element_type=F32)
            + jnp.dot(lo, b_exact, preferred_element_type=F32))


def _params(semantics):
    return pltpu.CompilerParams(dimension_semantics=semantics, vmem_limit_bytes=VMEM_LIMIT)


def _in_proj_kernel(x_ref, g_ref, w_ref, wl_ref, qkv_ref, rkv_ref, lora_ref, xn_ref, *, n_qkv, n_lora):
    j = pl.program_id(1)

    @pl.when(j == 0)
    def _():
        x = x_ref[...]
        ms = jnp.mean(x * x, axis=-1, keepdims=True)
        xn_ref[...] = (x * lax.rsqrt(ms + RMS_EPS) * g_ref[...]).astype(BF16)

    @pl.when(j < n_qkv)
    def _():
        qkv_ref[...] = jnp.dot(xn_ref[...], w_ref[...], preferred_element_type=F32)

    @pl.when((j >= n_qkv) & (j < 2 * n_qkv))
    def _():
        rkv_ref[...] = jnp.dot(xn_ref[...], w_ref[...], preferred_element_type=F32)

    @pl.when(j == 2 * n_qkv)
    def _():
        col = lax.broadcasted_iota(jnp.int32, (1, wl_ref.shape[1]), 1)
        wl = jnp.where(col < n_lora, wl_ref[...], jnp.zeros((), BF16))
        lora_ref[...] = jnp.dot(xn_ref[...], wl, preferred_element_type=F32)


def _in_proj(x2, g, w, *, n_main, n_lora_pad, tm, tn):
    t, d = x2.shape
    n_lora = w.shape[1] - n_main
    n_qkv = (n_main // 2) // tn
    last = 2 * n_qkv - 1
    assert n_main % n_lora_pad == 0 and n_lora <= n_lora_pad
    return pl.pallas_call(
        functools.partial(_in_proj_kernel, n_qkv=n_qkv, n_lora=n_lora),
        grid=(t // tm, 2 * n_qkv + 1),
        in_specs=[
            pl.BlockSpec((tm, d), lambda i, j: (i, 0)),
            pl.BlockSpec((1, d), lambda i, j: (0, 0)),
            pl.BlockSpec((d, tn), lambda i, j: (0, jnp.minimum(j, last))),
            pl.BlockSpec((d, n_lora_pad), lambda i, j: (0, n_main // n_lora_pad)),
        ],
        out_specs=[
            pl.BlockSpec((tm, tn), lambda i, j: (i, jnp.minimum(j, n_qkv - 1))),
            pl.BlockSpec((tm, tn), lambda i, j: (i, jnp.clip(j - n_qkv, 0, n_qkv - 1))),
            pl.BlockSpec((tm, n_lora_pad), lambda i, j: (i, 0)),
        ],
        out_shape=[
            jax.ShapeDtypeStruct((t, n_main // 2), F32),
            jax.ShapeDtypeStruct((t, n_main // 2), F32),
            jax.ShapeDtypeStruct((t, n_lora_pad), F32),
        ],
        scratch_shapes=[pltpu.VMEM((tm, d), BF16)],
        compiler_params=_params(("arbitrary", "arbitrary")),
        name="in_proj",
    )(x2, g, w, w)


ATTN_BLK = 128
ATTN_ROWS = max(w for w, _ in DILATED_PATTERNS)
ATTN_GROUP = 4


def _attn_kernel(q_ref, kp_ref, kc_ref, vp_ref, vc_ref, band_ref, out_ref, bias_ref, o_ref, lse_ref,
                 *, n_heads):
    b, n, hp = pl.program_id(0), pl.program_id(1), pl.program_id(2)
    blk = ATTN_BLK

    @pl.when((b == 0) & (n == 0) & (hp == 0))
    def _():
        for i in range(bias_ref.shape[0]):
            row = jnp.broadcast_to(band_ref[i:i + 1, :], (blk, 2 * blk))
            bias_ref[i] = pltpu.roll(row, 0, 1, stride=1, stride_axis=0)

    lane = lax.broadcasted_iota(jnp.int32, (1, LANES), 1)
    in_head = [(lane >= half * HEAD_DIM) & (lane < (half + 1) * HEAD_DIM) for half in range(2)]
    key_idx = lax.broadcasted_iota(jnp.int32, (1, 2 * blk), 1)
    first_ok = (key_idx >= blk) | (n > 0)
    ones_tile = jnp.ones((2 * blk, LANES), BF16)
    zero = jnp.zeros((), BF16)

    units = [(bi, window, dilation, sub, r)
             for bi, (window, dilation) in enumerate(DILATED_PATTERNS)
             for sub in range(ATTN_ROWS // window) for r in range(dilation)]
    groups = [units[g0:g0 + ATTN_GROUP] for g0 in range(0, len(units), ATTN_GROUP)]

    def scores(group):
        work = []
        for bi, window, dilation, sub, r in group:
            cur = pl.ds(sub * window + r, blk, stride=dilation)
            if sub == 0:
                prev_k, prev_v = kp_ref, vp_ref
                prev = pl.ds(ATTN_ROWS - window + r, blk, stride=dilation)
            else:
                prev_k, prev_v = kc_ref, vc_ref
                prev = pl.ds((sub - 1) * window + r, blk, stride=dilation)
            q = (q_ref[cur, :] * (ATTN_SCALE * LOG2_E)).astype(BF16)
            k = jnp.concatenate([prev_k[prev, :], kc_ref[cur, :]], axis=0).astype(BF16)
            v = jnp.concatenate([prev_v[prev, :], vc_ref[cur, :]], axis=0).astype(BF16)
            q_both = jnp.concatenate([jnp.where(m, q, zero) for m in in_head], axis=0)
            work.append((bi, cur, sub == 0, _mm_nt(q_both, k), jnp.concatenate([v, ones_tile], axis=1)))
        return work

    def softmax(work):
        done = []
        for bi, cur, at_start, s, v in work:
            bias = bias_ref[pl.ds(bi * n_heads + 2 * hp, 2)]
            su = s + bias.reshape(2 * blk, 2 * blk)
            if at_start:
                su = jnp.where(first_ok, su, MASK_VALUE)
            mx = jnp.max(su, axis=-1, keepdims=True)
            done.append((bi, cur, jnp.exp2(su - mx).astype(BF16), mx, v))
        return done

    def values(done):
        for bi, cur, p, mx, v in done:
            res = jnp.dot(p, v, preferred_element_type=F32)
            ra, rb = res[:blk], res[blk:]
            la, lb = ra[:, LANES:], rb[:, LANES:]
            o_ref[bi, cur, :] = jnp.where(in_head[0], ra[:, :LANES] / la, rb[:, :LANES] / lb)
            lse_ref[bi, cur, :] = jnp.where(in_head[0], mx[:blk] + jnp.log2(la), mx[blk:] + jnp.log2(lb))

    pending = None
    for group in groups:
        work = scores(group)
        if pending is not None:
            values(pending)
        pending = softmax(work)
    values(pending)

    lses = [lse_ref[bi] for bi in range(len(DILATED_PATTERNS))]
    top = functools.reduce(jnp.maximum, lses)
    es = [jnp.exp2(l - top) for l in lses]
    num = functools.reduce(jnp.add, [e * o_ref[bi] for bi, e in enumerate(es)])
    out_ref[...] = (num / functools.reduce(jnp.add, es)).astype(out_ref.dtype)


def _attention(qkv, band, *, batch, seq, n_heads):
    width = n_heads * HEAD_DIM
    n_pairs = width // LANES
    nb = seq // ATTN_ROWS
    n_pat = len(DILATED_PATTERNS)
    spec = lambda f: pl.BlockSpec((ATTN_ROWS, LANES), f)
    prev = lambda b, n: b * nb + jnp.maximum(n - 1, 0)
    return pl.pallas_call(
        functools.partial(_attn_kernel, n_heads=n_heads),
        grid=(batch, nb, n_pairs),
        in_specs=[
            spec(lambda b, n, hp: (b * nb + n, hp)),
            spec(lambda b, n, hp: (prev(b, n), n_pairs + hp)),
            spec(lambda b, n, hp: (b * nb + n, n_pairs + hp)),
            spec(lambda b, n, hp: (prev(b, n), 2 * n_pairs + hp)),
            spec(lambda b, n, hp: (b * nb + n, 2 * n_pairs + hp)),
            pl.BlockSpec(band.shape, lambda b, n, hp: (0, 0)),
        ],
        out_specs=spec(lambda b, n, hp: (b * nb + n, hp)),
        out_shape=jax.ShapeDtypeStruct((batch * seq, width), BF16),
        scratch_shapes=[
            pltpu.VMEM((n_pat * n_heads, ATTN_BLK, 2 * ATTN_BLK), F32),
            pltpu.VMEM((n_pat, ATTN_ROWS, LANES), F32),
            pltpu.VMEM((n_pat, ATTN_ROWS, LANES), F32),
        ],
        compiler_params=_params(("arbitrary", "arbitrary", "arbitrary")),
        name="dilated_attn",
    )(qkv, qkv, qkv, qkv, qkv, band)


def _t5_bucket(dist):
    exact = REL_BUCKETS // 2
    d_f = jnp.maximum(dist, 1).astype(F32)
    large = exact + (jnp.log(d_f / exact) / math.log(REL_MAX_DIST / exact)
                     * (REL_BUCKETS - exact)).astype(jnp.int32)
    large = jnp.minimum(large, REL_BUCKETS - 1)
    return jnp.where(dist < exact, dist, large)


def _band_rows(bias_table):
    blk = ATTN_BLK
    rel = blk - jnp.arange(2 * blk)
    band = (rel >= 0) & (rel <= blk)
    rows = []
    for _, dilation in DILATED_PATTERNS:
        bias = bias_table[_t5_bucket(jnp.clip(rel, 0, blk) * dilation)]
        rows.append(jnp.where(band[:, None], bias.astype(F32) * LOG2_E, MASK_VALUE).T)
    return jnp.concatenate(rows, axis=0)


def _rwkv_kernel(rkv_ref, lora_ref, mix_rkv_ref, mix_lora_ref, w0_ref, w_up_ref, a0_ref, a_up_ref,
                 g_up_ref, kk_ref, ka_ref, rk_ref, lnw_ref, lnb_ref, seg_ref, tri_ref,
                 out_ref,
                 state_ref, carry_rkv_ref, carry_lora_ref,
                 rt_ref, at_ref, bh_ref, kh_ref, bc_ref, kc_ref, v_ref, pc_ref, y_ref, *, width):
    c = pl.program_id(0)
    n_pairs = width // LANES
    n_seq, ch = rkv_ref.shape[:2]
    rows = n_seq * ch

    @pl.when(c == 0)
    def _():
        state_ref[...] = jnp.zeros_like(state_ref)
        carry_rkv_ref[...] = jnp.zeros_like(carry_rkv_ref)
        carry_lora_ref[...] = jnp.zeros_like(carry_lora_ref)

    row = lax.broadcasted_iota(jnp.int32, (rows, 1), 0) % ch

    def per_seq(row_of):
        return jnp.concatenate([jnp.broadcast_to(row_of(b), (ch, row_of(b).shape[-1]))
                                for b in range(n_seq)], axis=0)

    def token_shift(z_ref, carry_ref, mix):
        z = z_ref[...].reshape(rows, z_ref.shape[-1])
        prev = jnp.where(row == 0, per_seq(lambda b: carry_ref[b]), pltpu.roll(z, 1, axis=0))
        for b in range(n_seq):
            carry_ref[b] = z[(b + 1) * ch - 1:(b + 1) * ch, :]
        return z + (prev - z) * mix

    z = token_shift(rkv_ref, carry_rkv_ref, mix_rkv_ref[...])
    zl = token_shift(lora_ref, carry_lora_ref, mix_lora_ref[...])
    r, k, v = z[:, :width], z[:, width:2 * width], z[:, 2 * width:]

    lw = -DECAY_SCALE * jax.nn.sigmoid(w0_ref[...] + _mm(jnp.tanh(zl), w_up_ref[...]))
    a_sig = jax.nn.sigmoid(a0_ref[...] + _mm(zl, a_up_ref[...]))
    gate = _mm(jax.nn.sigmoid(zl), g_up_ref[...])

    seg = seg_ref[...]

    def head_sums(x):
        tiles = [_mm_split(x[:, t:t + SEG_LANES], seg) for t in range(0, width, SEG_LANES)]
        return jnp.concatenate(tiles, axis=1)

    kk = k * kk_ref[...]
    kk = kk / jnp.maximum(jnp.sqrt(head_sums(kk * kk)), 1e-12)
    k = k * (1.0 + (a_sig - 1.0) * ka_ref[...])
    bonus = head_sums(r * k * rk_ref[...]) * v
    a_in = -kk
    b_in = kk * a_sig

    lw_hi = lw.astype(BF16)
    lw_lo = (lw - lw_hi.astype(F32)).astype(BF16)
    tri = tri_ref[...]
    cum = (jnp.dot(tri, lw_hi, preferred_element_type=F32)
           + jnp.dot(tri, lw_lo, preferred_element_type=F32))
    cum_end = per_seq(lambda b: cum[(b + 1) * ch - 1:(b + 1) * ch, :])
    e_neg = jnp.exp(-cum)
    e_end = jnp.exp(cum_end - cum)

    n_units = n_seq * n_pairs

    def put(ref, val):
        for b in range(n_seq):
            for p in range(n_pairs):
                ref[b * n_pairs + p] = val[b * ch:b * ch + ref.shape[1], p * LANES:(p + 1) * LANES]

    put(rt_ref, r * jnp.exp(cum))
    put(at_ref, a_in * jnp.exp(cum - lw))
    put(bh_ref, b_in * e_neg)
    put(kh_ref, k * e_neg)
    put(bc_ref, b_in * e_end)
    put(kc_ref, k * e_end)
    put(v_ref, v)
    put(pc_ref, jnp.exp(cum_end))

    ti = lax.broadcasted_iota(jnp.int32, (ch, 2 * ch), 0)
    si = lax.broadcasted_iota(jnp.int32, (ch, 2 * ch), 1) % ch
    incl = ti >= si
    strict = ti > si
    lane = lax.broadcasted_iota(jnp.int32, (1, 2 * LANES), 1)
    head_a2 = (lane % LANES) < HEAD_DIM
    head_a = head_a2[:, :LANES]
    di = lax.broadcasted_iota(jnp.int32, (LANES, LANES), 0)
    dj = lax.broadcasted_iota(jnp.int32, (LANES, LANES), 1)
    same_head = (di < HEAD_DIM) == (dj < HEAD_DIM)
    diag = di == dj

    zero = jnp.zeros((), BF16)

    def stack(x, mask):
        x = x.astype(BF16)
        return jnp.concatenate([jnp.where(mask, x, zero), jnp.where(mask, zero, x)], axis=0)

    def block_diag(m):
        m = m.astype(BF16)
        return jnp.where(same_head, jnp.concatenate([m, m], axis=0), zero)

    pairs = range(n_units)
    zero_tile = jnp.zeros((2 * ch, LANES), BF16)
    a_rb, a_ab, a_rk, a_ak, vst = [], [], [], [], []
    for p in pairs:
        lhs = jnp.concatenate([rt_ref[p], at_ref[p]], axis=0)
        rhs = jnp.concatenate([stack(bh_ref[p], head_a), stack(kh_ref[p], head_a)], axis=0)
        a_bk = _mm_nt(lhs, rhs).astype(BF16)
        a_rb.append(jnp.where(incl, a_bk[:ch, :LANES], zero))
        a_ab.append(jnp.where(strict, a_bk[ch:, :LANES], zero))
        a_rk.append(jnp.where(incl, a_bk[:ch, LANES:], zero))
        a_ak.append(jnp.where(strict, a_bk[ch:, LANES:], zero))
        vst.append(stack(v_ref[p], head_a))

    xs = [jnp.concatenate([at_ref[p], _mm(a_ak[p], vst[p])], axis=1) for p in pairs]
    nk = a_ab
    for _ in range(int(math.log2(ch)) - 1):
        nk_next = [_mm(n, block_diag(n)) for n in nk]
        xs = [x + _mm(n, stack(x, head_a2)) for n, x in zip(nk, xs)]
        nk = nk_next
    xs = [x + _mm(n, stack(x, head_a2)) for n, x in zip(nk, xs)]

    ax = [_mm(jnp.concatenate([a_rb[p], a_rk[p]], axis=1),
              jnp.concatenate([stack(xs[p], head_a2), jnp.concatenate([zero_tile, vst[p]], axis=1)], axis=0))
          for p in pairs]
    mg = [_mm_tn(jnp.concatenate([bc_ref[p], kc_ref[p]], axis=0),
                 jnp.concatenate([xs[p], jnp.concatenate([jnp.zeros((ch, LANES), F32), v_ref[p]], axis=1)],
                                 axis=0))
          for p in pairs]
    for p in pairs:
        q_acc = rt_ref[p] + ax[p][:, :LANES]
        m_mat = jnp.where(same_head, mg[p][:, :LANES], 0.0) + jnp.where(diag, pc_ref[p], 0.0)
        g_mat = jnp.where(same_head, mg[p][:, LANES:], 0.0)
        res = _mm(jnp.concatenate([q_acc, m_mat], axis=0), state_ref[p])
        y_ref[p] = res[:ch] + ax[p][:, LANES:]
        state_ref[p] = res[ch:] + g_mat

    y = jnp.concatenate([jnp.concatenate([y_ref[b * n_pairs + p] for p in range(n_pairs)], axis=1)
                         for b in range(n_seq)], axis=0)
    inv_n = 1.0 / HEAD_DIM
    mu = head_sums(y) * inv_n
    d = y - mu
    var = head_sums(d * d) * inv_n
    yn = d * lax.rsqrt(var + GN_EPS) * lnw_ref[...] + lnb_ref[...]
    out_ref[...] = ((yn + bonus) * gate).reshape(out_ref.shape).astype(out_ref.dtype)


def _rwkv(rkv, lora, prm, *, batch, seq):
    t, w3 = rkv.shape
    width = w3 // 3
    n_lora = lora.shape[1]
    ch = CHUNK
    n_chunks = seq // ch
    n_pairs = width // LANES
    row = lambda n: pl.BlockSpec((1, n), lambda c: (0, 0))
    full = lambda a: pl.BlockSpec(a.shape, lambda c: (0,) * a.ndim)
    head_id = jnp.arange(SEG_LANES) // HEAD_DIM
    seg = (head_id[:, None] == head_id[None, :]).astype(BF16)
    pos = jnp.arange(batch * ch)
    tri = ((pos[:, None] >= pos[None, :]) & (pos[:, None] // ch == pos[None, :] // ch)).astype(BF16)
    n_units = batch * n_pairs
    big = lambda: pltpu.VMEM((n_units, ch, LANES), F32)
    args = (rkv.reshape(batch, seq, w3), lora.reshape(batch, seq, n_lora), prm["mix_rkv"], prm["mix_lora"],
            prm["w0"], prm["w_up"], prm["a0"], prm["a_up"], prm["g_up"], prm["k_k"], prm["k_a"], prm["r_k"],
            prm["ln_w"], prm["ln_b"], seg, tri)
    in_specs = [
        pl.BlockSpec((batch, ch, w3), lambda c: (0, c, 0)),
        pl.BlockSpec((batch, ch, n_lora), lambda c: (0, c, 0)),
        row(w3), row(n_lora), row(width), full(prm["w_up"]), row(width), full(prm["a_up"]),
        full(prm["g_up"]), row(width), row(width), row(width), row(width), row(width),
        full(seg), full(tri),
    ]
    out = pl.pallas_call(
        functools.partial(_rwkv_kernel, width=width),
        grid=(n_chunks,),
        in_specs=in_specs,
        out_specs=pl.BlockSpec((batch, ch, width), lambda c: (0, c, 0)),
        out_shape=jax.ShapeDtypeStruct((batch, seq, width), BF16),
        scratch_shapes=[
            pltpu.VMEM((n_units, LANES, LANES), F32),
            pltpu.VMEM((batch, 1, w3), F32),
            pltpu.VMEM((batch, 1, n_lora), F32),
            big(), big(), big(), big(), big(), big(), big(),
            pltpu.VMEM((n_units, 1, LANES), F32),
            big(),
        ],
        compiler_params=_params(("arbitrary",)),
        name="rwkv7",
    )(*args)
    return out.reshape(t, width)


def _out_proj_kernel(attn_ref, rw_ref, wa_ref, wr_ref, x_ref, out_ref):
    out_ref[...] = (x_ref[...] + jnp.dot(attn_ref[...], wa_ref[...], preferred_element_type=F32)
                    + jnp.dot(rw_ref[...], wr_ref[...], preferred_element_type=F32))


def _out_proj(attn, rw, w_out, x2, *, tm, tn):
    t, width = attn.shape
    d = w_out.shape[1]
    return pl.pallas_call(
        _out_proj_kernel,
        grid=(t // tm, d // tn),
        in_specs=[
            pl.BlockSpec((tm, width), lambda i, j: (i, 0)),
            pl.BlockSpec((tm, width), lambda i, j: (i, 0)),
            pl.BlockSpec((width, tn), lambda i, j: (0, j)),
            pl.BlockSpec((width, tn), lambda i, j: (1, j)),
            pl.BlockSpec((tm, tn), lambda i, j: (i, j)),
        ],
        out_specs=pl.BlockSpec((tm, tn), lambda i, j: (i, j)),
        out_shape=jax.ShapeDtypeStruct((t, d), F32),
        compiler_params=_params(("arbitrary", "arbitrary")),
        name="out_proj",
    )(attn, rw, w_out, w_out, x2)


FFN_DOWN_COLS = 512


def _ffn_kernel(h_ref, g_ref, wg_ref, wu_ref, wd_ref, og_ref, out_ref, hn_ref):
    j = pl.program_id(1)

    @pl.when(j == 0)
    def _():
        h = h_ref[...]
        ms = jnp.mean(h * h, axis=-1, keepdims=True)
        hn_ref[...] = (h * lax.rsqrt(ms + RMS_EPS) * g_ref[...]).astype(BF16)
        out_ref[...] = h

    hn = hn_ref[...]
    gate = jnp.dot(hn, wg_ref[...], preferred_element_type=F32)
    up = jnp.dot(hn, wu_ref[...], preferred_element_type=F32)
    act = (gate * jax.nn.sigmoid(gate) * up).astype(BF16)
    for c0 in range(0, out_ref.shape[1], FFN_DOWN_COLS):
        cols = slice(c0, c0 + FFN_DOWN_COLS)
        out_ref[:, cols] += jnp.dot(act, wd_ref[:, cols], preferred_element_type=F32)

    @pl.when(j == pl.num_programs(1) - 1)
    def _():
        h = out_ref[...]
        ms = jnp.mean(h * h, axis=-1, keepdims=True)
        out_ref[...] = h * lax.rsqrt(ms + RMS_EPS) * og_ref[...]


def _ffn(h, g, w_gate_up, w_down, out_g, *, tm, th):
    t, d = h.shape
    hidden = w_down.shape[0]
    n_h = hidden // th
    return pl.pallas_call(
        _ffn_kernel,
        grid=(t // tm, n_h),
        in_specs=[
            pl.BlockSpec((tm, d), lambda i, j: (i, 0)),
            pl.BlockSpec((1, d), lambda i, j: (0, 0)),
            pl.BlockSpec((d, th), lambda i, j: (0, j)),
            pl.BlockSpec((d, th), lambda i, j: (0, n_h + j)),
            pl.BlockSpec((th, d), lambda i, j: (j, 0)),
            pl.BlockSpec((1, d), lambda i, j: (0, 0)),
        ],
        out_specs=pl.BlockSpec((tm, d), lambda i, j: (i, 0)),
        out_shape=jax.ShapeDtypeStruct((t, d), F32),
        scratch_shapes=[pltpu.VMEM((tm, d), BF16)],
        compiler_params=pltpu.CompilerParams(dimension_semantics=("arbitrary", "arbitrary"),
                                             vmem_limit_bytes=FFN_VMEM_LIMIT),
        name="ffn",
    )(h, g, w_gate_up, w_gate_up, w_down, out_g)


def _pad_rows(a, rows):
    return jnp.pad(a, ((0, rows - a.shape[0]), (0, 0)))


def _layer(h, norm1_g, w_in, shift_mix, w0, w_up, a0, a_up, g_up, k_k, k_a, r_k, ln_w, ln_b,
           rel_bias_table, w_out, norm2_g, w_gate_up, w_down, out_g, *, batch, seq, tile_m):
    d = h.shape[1]
    width = w_out.shape[0] // 2
    n_heads = width // HEAD_DIM
    n_main = 6 * width
    rank_w, rank_a, rank_g = w_up.shape[0], a_up.shape[0], g_up.shape[0]
    n_lora = rank_w + rank_a + rank_g
    n_lora_pad = -(-n_lora // LANES) * LANES

    qkv, rkv, lora = _in_proj(h, norm1_g.reshape(1, d), w_in.astype(BF16), n_main=n_main,
                              n_lora_pad=n_lora_pad, tm=tile_m, tn=width)

    attn = _attention(qkv, _band_rows(rel_bias_table), batch=batch, seq=seq, n_heads=n_heads)

    mix_rkv = shift_mix[:3 * width].reshape(1, 3 * width)
    mix_lora = jnp.pad(shift_mix[3 * width:], (0, n_lora_pad - n_lora)).reshape(1, n_lora_pad)
    zeros = lambda n: jnp.zeros((n, width), F32)
    prm = dict(
        mix_rkv=mix_rkv, mix_lora=mix_lora,
        w0=w0.reshape(1, width), a0=a0.reshape(1, width),
        w_up=_pad_rows(w_up, n_lora_pad).astype(BF16),
        a_up=_pad_rows(jnp.concatenate([zeros(rank_w), a_up]), n_lora_pad).astype(BF16),
        g_up=_pad_rows(jnp.concatenate([zeros(rank_w + rank_a), g_up]), n_lora_pad).astype(BF16),
        k_k=k_k.reshape(1, width), k_a=k_a.reshape(1, width), r_k=r_k.reshape(1, width),
        ln_w=ln_w.reshape(1, width), ln_b=ln_b.reshape(1, width),
    )
    rw = _rwkv(rkv, lora, prm, batch=batch, seq=seq)

    h1 = _out_proj(attn, rw, w_out.astype(BF16), h, tm=tile_m // 2, tn=d)
    return _ffn(h1, norm2_g.reshape(1, d), w_gate_up.astype(BF16), w_down.astype(BF16), out_g.reshape(1, d),
                tm=tile_m, th=512)


def kernel(x, norm1_g, w_in, rwkv_shift_mix, rwkv_w0, rwkv_w_up, rwkv_a0, rwkv_a_up, rwkv_g_up, rwkv_k_k, rwkv_k_a, rwkv_r_k, rwkv_ln_w, rwkv_ln_b, rel_bias_table, w_out, norm2_g, w_gate_up, w_down, final_g):
    batch, seq, d = x.shape
    depth = w_in.shape[0]
    assert depth == 1, "the final RMSNorm is fused into the last layer's FFN kernel"
    assert seq % DILATED_PATTERNS[-1][0] == 0 and seq % CHUNK == 0
    h = x.reshape(batch * seq, d)
    out = _layer(h, norm1_g[0], w_in[0], rwkv_shift_mix[0], rwkv_w0[0], rwkv_w_up[0], rwkv_a0[0],
                 rwkv_a_up[0], rwkv_g_up[0], rwkv_k_k[0], rwkv_k_a[0], rwkv_r_k[0], rwkv_ln_w[0],
                 rwkv_ln_b[0], rel_bias_table, w_out[0], norm2_g[0], w_gate_up[0], w_down[0], final_g,
                 batch=batch, seq=seq, tile_m=min(1024, batch * seq))
    return out.reshape(batch, seq, d)
```

```python
import functools
import math

import jax
import jax.numpy as jnp
from jax import lax
from jax.experimental import pallas as pl
from jax.experimental.pallas import tpu as pltpu

F32 = jnp.float32
BF16 = jnp.bfloat16

HEAD_DIM = 64
LANES = 128
DILATED_PATTERNS = ((128, 1), (512, 4), (2048, 16))
REL_BUCKETS = 32
REL_MAX_DIST = 2048
RMS_EPS = 1e-6
GN_EPS = 64e-5
DECAY_SCALE = math.exp(-0.5)
ATTN_SCALE = HEAD_DIM ** -0.5
MASK_VALUE = -1e30
LOG2_E = math.log2(math.e)
CHUNK = 64
SEG_LANES = 256
VMEM_LIMIT = 56 * 1024 * 1024
FFN_VMEM_LIMIT = 60 * 1024 * 1024


def _mm(a, b):
    return jnp.dot(a.astype(BF16), b.astype(BF16), preferred_element_type=F32)


def _mm_nt(a, b):
    return lax.dot_general(a.astype(BF16), b.astype(BF16), (((1,), (1,)), ((), ())),
                           preferred_element_type=F32)


def _mm_tn(a, b):
    return lax.dot_general(a.astype(BF16), b.astype(BF16), (((0,), (0,)), ((), ())),
                           preferred_element_type=F32)


def _mm_split(a, b_exact):
    hi = a.astype(BF16)
    lo = (a - hi.astype(F32)).astype(BF16)
    return (jnp.dot(hi, b_exact, preferred_element_type=F32)
            + jnp.dot(lo, b_exact, preferred_element_type=F32))


def _params(semantics):
    return pltpu.CompilerParams(dimension_semantics=semantics, vmem_limit_bytes=VMEM_LIMIT)


def _in_proj_kernel(x_ref, g_ref, w_ref, wl_ref, qkv_ref, rkv_ref, lora_ref, xn_ref, *, n_qkv, n_lora):
    j = pl.program_id(1)

    @pl.when(j == 0)
    def _():
        x = x_ref[...]
        ms = jnp.mean(x * x, axis=-1, keepdims=True)
        xn_ref[...] = (x * lax.rsqrt(ms + RMS_EPS) * g_ref[...]).astype(BF16)

    @pl.when(j < n_qkv)
    def _():
        qkv_ref[...] = jnp.dot(xn_ref[...], w_ref[...], preferred_element_type=F32)

    @pl.when((j >= n_qkv) & (j < 2 * n_qkv))
    def _():
        rkv_ref[...] = jnp.dot(xn_ref[...], w_ref[...], preferred_element_type=F32)

    @pl.when(j == 2 * n_qkv)
    def _():
        col = lax.broadcasted_iota(jnp.int32, (1, wl_ref.shape[1]), 1)
        wl = jnp.where(col < n_lora, wl_ref[...], jnp.zeros((), BF16))
        lora_ref[...] = jnp.dot(xn_ref[...], wl, preferred_element_type=F32)


def _in_proj(x2, g, w, *, n_main, n_lora_pad, tm, tn):
    t, d = x2.shape
    n_lora = w.shape[1] - n_main
    n_qkv = (n_main // 2) // tn
    last = 2 * n_qkv - 1
    assert n_main % n_lora_pad == 0 and n_lora <= n_lora_pad
    return pl.pallas_call(
        functools.partial(_in_proj_kernel, n_qkv=n_qkv, n_lora=n_lora),
        grid=(t // tm, 2 * n_qkv + 1),
        in_specs=[
            pl.BlockSpec((tm, d), lambda i, j: (i, 0)),
            pl.BlockSpec((1, d), lambda i, j: (0, 0)),
            pl.BlockSpec((d, tn), lambda i, j: (0, jnp.minimum(j, last))),
            pl.BlockSpec((d, n_lora_pad), lambda i, j: (0, n_main // n_lora_pad)),
        ],
        out_specs=[
            pl.BlockSpec((tm, tn), lambda i, j: (i, jnp.minimum(j, n_qkv - 1))),
            pl.BlockSpec((tm, tn), lambda i, j: (i, jnp.clip(j - n_qkv, 0, n_qkv - 1))),
            pl.BlockSpec((tm, n_lora_pad), lambda i, j: (i, 0)),
        ],
        out_shape=[
            jax.ShapeDtypeStruct((t, n_main // 2), F32),
            jax.ShapeDtypeStruct((t, n_main // 2), F32),
            jax.ShapeDtypeStruct((t, n_lora_pad), F32),
        ],
        scratch_shapes=[pltpu.VMEM((tm, d), BF16)],
        compiler_params=_params(("arbitrary", "arbitrary")),
        name="in_proj",
    )(x2, g, w, w)


ATTN_BLK = 128
ATTN_ROWS = max(w for w, _ in DILATED_PATTERNS)
ATTN_GROUP = 2


def _attn_stages(q_ref, kp_ref, kc_ref, vp_ref, vc_ref, band_ref, out_ref, bias_ref, o_ref, lse_ref,
                 *, n_heads, first, n, hp):
    blk = ATTN_BLK

    @pl.when(first)
    def _():
        for i in range(bias_ref.shape[0]):
            row = jnp.broadcast_to(band_ref[i:i + 1, :], (blk, 2 * blk))
            bias_ref[i] = pltpu.roll(row, 0, 1, stride=1, stride_axis=0)

    yield

    lane = lax.broadcasted_iota(jnp.int32, (1, LANES), 1)
    in_head = [(lane >= half * HEAD_DIM) & (lane < (half + 1) * HEAD_DIM) for half in range(2)]
    key_idx = lax.broadcasted_iota(jnp.int32, (1, 2 * blk), 1)
    first_ok = (key_idx >= blk) | (n > 0)
    ones_tile = jnp.ones((2 * blk, LANES), BF16)
    zero = jnp.zeros((), BF16)

    units = [(bi, window, dilation, sub, r)
             for bi, (window, dilation) in enumerate(DILATED_PATTERNS)
             for sub in range(ATTN_ROWS // window) for r in range(dilation)]
    groups = [units[g0:g0 + ATTN_GROUP] for g0 in range(0, len(units), ATTN_GROUP)]

    def scores(group):
        work = []
        for bi, window, dilation, sub, r in group:
            cur = pl.ds(sub * window + r, blk, stride=dilation)
            if sub == 0:
                prev_k, prev_v = kp_ref, vp_ref
                prev = pl.ds(ATTN_ROWS - window + r, blk, stride=dilation)
            else:
                prev_k, prev_v = kc_ref, vc_ref
                prev = pl.ds((sub - 1) * window + r, blk, stride=dilation)
            q = (q_ref[cur, :] * (ATTN_SCALE * LOG2_E)).astype(BF16)
            k = jnp.concatenate([prev_k[prev, :], kc_ref[cur, :]], axis=0).astype(BF16)
            v = jnp.concatenate([prev_v[prev, :], vc_ref[cur, :]], axis=0).astype(BF16)
            q_both = jnp.concatenate([jnp.where(m, q, zero) for m in in_head], axis=0)
            work.append((bi, cur, sub == 0, _mm_nt(q_both, k), jnp.concatenate([v, ones_tile], axis=1)))
        return work

    def softmax(work):
        done = []
        for bi, cur, at_start, s, v in work:
            bias = bias_ref[pl.ds(bi * n_heads + 2 * hp, 2)]
            su = s + bias.reshape(2 * blk, 2 * blk)
            if at_start:
                su = jnp.where(first_ok, su, MASK_VALUE)
            mx = jnp.max(su, axis=-1, keepdims=True)
            done.append((bi, cur, jnp.exp2(su - mx).astype(BF16), mx, v))
        return done

    def values(done):
        for bi, cur, p, mx, v in done:
            res = jnp.dot(p, v, preferred_element_type=F32)
            ra, rb = res[:blk], res[blk:]
            la, lb = ra[:, LANES:], rb[:, LANES:]
            o_ref[bi, cur, :] = jnp.where(in_head[0], ra[:, :LANES] / la, rb[:, :LANES] / lb)
            lse_ref[bi, cur, :] = jnp.where(in_head[0], mx[:blk] + jnp.log2(la), mx[blk:] + jnp.log2(lb))

    pending = None
    for group in groups:
        work = scores(group)
        if pending is not None:
            values(pending)
        pending = softmax(work)
        yield
    values(pending)
    yield

    lses = [lse_ref[bi] for bi in range(len(DILATED_PATTERNS))]
    top = functools.reduce(jnp.maximum, lses)
    es = [jnp.exp2(l - top) for l in lses]
    num = functools.reduce(jnp.add, [e * o_ref[bi] for bi, e in enumerate(es)])
    out_ref[...] = (num / functools.reduce(jnp.add, es)).astype(out_ref.dtype)


def _t5_bucket(dist):
    exact = REL_BUCKETS // 2
    d_f = jnp.maximum(dist, 1).astype(F32)
    large = exact + (jnp.log(d_f / exact) / math.log(REL_MAX_DIST / exact)
                     * (REL_BUCKETS - exact)).astype(jnp.int32)
    large = jnp.minimum(large, REL_BUCKETS - 1)
    return jnp.where(dist < exact, dist, large)


def _band_rows(bias_table):
    blk = ATTN_BLK
    rel = blk - jnp.arange(2 * blk)
    band = (rel >= 0) & (rel <= blk)
    rows = []
    for _, dilation in DILATED_PATTERNS:
        bias = bias_table[_t5_bucket(jnp.clip(rel, 0, blk) * dilation)]
        rows.append(jnp.where(band[:, None], bias.astype(F32) * LOG2_E, MASK_VALUE).T)
    return jnp.concatenate(rows, axis=0)


def _rwkv_stages(rkv_ref, lora_ref, mix_rkv_ref, mix_lora_ref, w0_ref, w_up_ref, a0_ref, a_up_ref,
                 g_up_ref, kk_ref, ka_ref, rk_ref, lnw_ref, lnb_ref, seg_ref, tri_ref,
                 out_ref,
                 state_ref, carry_rkv_ref, carry_lora_ref,
                 rt_ref, at_ref, bh_ref, kh_ref, bc_ref, kc_ref, v_ref, pc_ref, y_ref, *, width, lo):
    n_pairs = width // LANES
    n_seq, ch = rkv_ref.shape[0], CHUNK
    rows = n_seq * ch

    row = lax.broadcasted_iota(jnp.int32, (rows, 1), 0) % ch

    def per_seq(row_of):
        return jnp.concatenate([jnp.broadcast_to(row_of(b), (ch, row_of(b).shape[-1]))
                                for b in range(n_seq)], axis=0)

    def token_shift(z_ref, carry_ref, mix):
        z = z_ref[:, lo:lo + ch, :].reshape(rows, z_ref.shape[-1])
        prev = jnp.where(row == 0, per_seq(lambda b: carry_ref[b]), pltpu.roll(z, 1, axis=0))
        for b in range(n_seq):
            carry_ref[b] = z[(b + 1) * ch - 1:(b + 1) * ch, :]
        return z + (prev - z) * mix

    z = token_shift(rkv_ref, carry_rkv_ref, mix_rkv_ref[...])
    zl = token_shift(lora_ref, carry_lora_ref, mix_lora_ref[...])
    r, k, v = z[:, :width], z[:, width:2 * width], z[:, 2 * width:]

    lw = -DECAY_SCALE * jax.nn.sigmoid(w0_ref[...] + _mm(jnp.tanh(zl), w_up_ref[...]))
    a_sig = jax.nn.sigmoid(a0_ref[...] + _mm(zl, a_up_ref[...]))
    gate = _mm(jax.nn.sigmoid(zl), g_up_ref[...])

    seg = seg_ref[...]

    def head_sums(x):
        tiles = [_mm_split(x[:, t:t + SEG_LANES], seg) for t in range(0, width, SEG_LANES)]
        return jnp.concatenate(tiles, axis=1)

    kk = k * kk_ref[...]
    kk = kk / jnp.maximum(jnp.sqrt(head_sums(kk * kk)), 1e-12)
    k = k * (1.0 + (a_sig - 1.0) * ka_ref[...])
    bonus = head_sums(r * k * rk_ref[...]) * v
    a_in = -kk
    b_in = kk * a_sig

    lw_hi = lw.astype(BF16)
    lw_lo = (lw - lw_hi.astype(F32)).astype(BF16)
    tri = tri_ref[...]
    cum = (jnp.dot(tri, lw_hi, preferred_element_type=F32)
           + jnp.dot(tri, lw_lo, preferred_element_type=F32))
    cum_end = per_seq(lambda b: cum[(b + 1) * ch - 1:(b + 1) * ch, :])
    e_neg = jnp.exp(-cum)
    e_end = jnp.exp(cum_end - cum)

    n_units = n_seq * n_pairs

    def put(ref, val):
        for b in range(n_seq):
            for p in range(n_pairs):
                ref[b * n_pairs + p] = val[b * ch:b * ch + ref.shape[1], p * LANES:(p + 1) * LANES]

    put(rt_ref, r * jnp.exp(cum))
    put(at_ref, a_in * jnp.exp(cum - lw))
    put(bh_ref, b_in * e_neg)
    put(kh_ref, k * e_neg)
    put(bc_ref, b_in * e_end)
    put(kc_ref, k * e_end)
    put(v_ref, v)
    put(pc_ref, jnp.exp(cum_end))
    yield

    ti = lax.broadcasted_iota(jnp.int32, (ch, 2 * ch), 0)
    si = lax.broadcasted_iota(jnp.int32, (ch, 2 * ch), 1) % ch
    incl = ti >= si
    strict = ti > si
    lane = lax.broadcasted_iota(jnp.int32, (1, 2 * LANES), 1)
    head_a2 = (lane % LANES) < HEAD_DIM
    head_a = head_a2[:, :LANES]
    di = lax.broadcasted_iota(jnp.int32, (LANES, LANES), 0)
    dj = lax.broadcasted_iota(jnp.int32, (LANES, LANES), 1)
    same_head = (di < HEAD_DIM) == (dj < HEAD_DIM)
    diag = di == dj

    zero = jnp.zeros((), BF16)

    def stack(x, mask):
        x = x.astype(BF16)
        return jnp.concatenate([jnp.where(mask, x, zero), jnp.where(mask, zero, x)], axis=0)

    def block_diag(m):
        m = m.astype(BF16)
        return jnp.where(same_head, jnp.concatenate([m, m], axis=0), zero)

    pairs = range(n_units)
    zero_tile = jnp.zeros((2 * ch, LANES), BF16)
    a_rb, a_ab, a_rk, a_ak, vst = [], [], [], [], []
    for p in pairs:
        lhs = jnp.concatenate([rt_ref[p], at_ref[p]], axis=0)
        rhs = jnp.concatenate([stack(bh_ref[p], head_a), stack(kh_ref[p], head_a)], axis=0)
        a_bk = _mm_nt(lhs, rhs).astype(BF16)
        a_rb.append(jnp.where(incl, a_bk[:ch, :LANES], zero))
        a_ab.append(jnp.where(strict, a_bk[ch:, :LANES], zero))
        a_rk.append(jnp.where(incl, a_bk[:ch, LANES:], zero))
        a_ak.append(jnp.where(strict, a_bk[ch:, LANES:], zero))
        vst.append(stack(v_ref[p], head_a))
    yield

    xs = [jnp.concatenate([at_ref[p], _mm(a_ak[p], vst[p])], axis=1) for p in pairs]
    yield
    nk = a_ab
    for _ in range(int(math.log2(ch)) - 1):
        nk_next = [_mm(n, block_diag(n)) for n in nk]
        xs = [x + _mm(n, stack(x, head_a2)) for n, x in zip(nk, xs)]
        nk = nk_next
        yield
    xs = [x + _mm(n, stack(x, head_a2)) for n, x in zip(nk, xs)]
    yield

    ax = [_mm(jnp.concatenate([a_rb[p], a_rk[p]], axis=1),
              jnp.concatenate([stack(xs[p], head_a2), jnp.concatenate([zero_tile, vst[p]], axis=1)], axis=0))
          for p in pairs]
    mg = [_mm_tn(jnp.concatenate([bc_ref[p], kc_ref[p]], axis=0),
                 jnp.concatenate([xs[p], jnp.concatenate([jnp.zeros((ch, LANES), F32), v_ref[p]], axis=1)],
                                 axis=0))
          for p in pairs]
    yield
    for p in pairs:
        q_acc = rt_ref[p] + ax[p][:, :LANES]
        m_mat = jnp.where(same_head, mg[p][:, :LANES], 0.0) + jnp.where(diag, pc_ref[p], 0.0)
        g_mat = jnp.where(same_head, mg[p][:, LANES:], 0.0)
        res = _mm(jnp.concatenate([q_acc, m_mat], axis=0), state_ref[p])
        y_ref[p] = res[:ch] + ax[p][:, LANES:]
        state_ref[p] = res[ch:] + g_mat
    yield

    y = jnp.concatenate([jnp.concatenate([y_ref[b * n_pairs + p] for p in range(n_pairs)], axis=1)
                         for b in range(n_seq)], axis=0)
    inv_n = 1.0 / HEAD_DIM
    mu = head_sums(y) * inv_n
    d = y - mu
    var = head_sums(d * d) * inv_n
    yn = d * lax.rsqrt(var + GN_EPS) * lnw_ref[...] + lnb_ref[...]
    out_ref[:, lo:lo + ch, :] = ((yn + bonus) * gate).reshape(n_seq, ch, width).astype(out_ref.dtype)


N_ATTN_IN, N_ATTN_SCRATCH = 6, 3
N_RWKV_IN, N_RWKV_SCRATCH = 16, 12
RWKV_STAGES_PER_ATTN_STAGE = 1


def _mixers_kernel(*refs, n_heads, width, nb, n_pairs, chunks_per_step):
    attn_in, refs = refs[:N_ATTN_IN], refs[N_ATTN_IN:]
    rwkv_in, refs = refs[:N_RWKV_IN], refs[N_RWKV_IN:]
    attn_out, rw_out, refs = refs[0], refs[1], refs[2:]
    attn_scr, rwkv_scr = refs[:N_ATTN_SCRATCH], refs[N_ATTN_SCRATCH:]
    c = pl.program_id(0)
    state_ref, carry_rkv_ref, carry_lora_ref = rwkv_scr[:3]

    @pl.when(c == 0)
    def _():
        state_ref[...] = jnp.zeros_like(state_ref)
        carry_rkv_ref[...] = jnp.zeros_like(carry_rkv_ref)
        carry_lora_ref[...] = jnp.zeros_like(carry_lora_ref)

    attn = _attn_stages(*attn_in, attn_out, *attn_scr, n_heads=n_heads, first=c == 0,
                        n=(c // n_pairs) % nb, hp=c % n_pairs)
    next(attn)

    def rwkv_chunks():
        for i in range(chunks_per_step):
            yield from _rwkv_stages(*rwkv_in, rw_out, *rwkv_scr, width=width, lo=i * CHUNK)

    rwkv = rwkv_chunks()
    running = [True, True]
    while any(running):
        for _ in range(RWKV_STAGES_PER_ATTN_STAGE):
            if running[0] and next(rwkv, "done") == "done":
                running[0] = False
        if running[1] and next(attn, "done") == "done":
            running[1] = False


def _mixers(qkv, band, rkv, lora, prm, *, batch, seq, n_heads):
    t, w3 = rkv.shape
    width = w3 // 3
    n_lora = lora.shape[1]
    ch = CHUNK
    n_pairs = width // LANES
    nb = seq // ATTN_ROWS
    n_pat = len(DILATED_PATTERNS)
    n_steps = batch * nb * n_pairs
    assert (seq // ch) % n_steps == 0, "RWKV chunks must divide evenly over the attention steps"
    chunks_per_step = (seq // ch) // n_steps
    rows_per_step = chunks_per_step * ch

    a_blk = lambda c: (c // n_pairs) // nb * nb + (c // n_pairs) % nb
    a_prev = lambda c: (c // n_pairs) // nb * nb + jnp.maximum((c // n_pairs) % nb - 1, 0)
    aspec = lambda f: pl.BlockSpec((ATTN_ROWS, LANES), f)
    attn_specs = [
        aspec(lambda c: (a_blk(c), c % n_pairs)),
        aspec(lambda c: (a_prev(c), n_pairs + c % n_pairs)),
        aspec(lambda c: (a_blk(c), n_pairs + c % n_pairs)),
        aspec(lambda c: (a_prev(c), 2 * n_pairs + c % n_pairs)),
        aspec(lambda c: (a_blk(c), 2 * n_pairs + c % n_pairs)),
        pl.BlockSpec(band.shape, lambda c: (0, 0)),
    ]

    row = lambda n: pl.BlockSpec((1, n), lambda c: (0, 0))
    full = lambda a: pl.BlockSpec(a.shape, lambda c: (0,) * a.ndim)
    head_id = jnp.arange(SEG_LANES) // HEAD_DIM
    seg = (head_id[:, None] == head_id[None, :]).astype(BF16)
    pos = jnp.arange(batch * ch)
    tri = ((pos[:, None] >= pos[None, :]) & (pos[:, None] // ch == pos[None, :] // ch)).astype(BF16)
    n_units = batch * n_pairs
    big = lambda: pltpu.VMEM((n_units, ch, LANES), F32)
    rwkv_args = (rkv.reshape(batch, seq, w3), lora.reshape(batch, seq, n_lora), prm["mix_rkv"],
                 prm["mix_lora"], prm["w0"], prm["w_up"], prm["a0"], prm["a_up"], prm["g_up"], prm["k_k"],
                 prm["k_a"], prm["r_k"], prm["ln_w"], prm["ln_b"], seg, tri)
    rwkv_specs = [
        pl.BlockSpec((batch, rows_per_step, w3), lambda c: (0, c, 0)),
        pl.BlockSpec((batch, rows_per_step, n_lora), lambda c: (0, c, 0)),
        row(w3), row(n_lora), row(width), full(prm["w_up"]), row(width), full(prm["a_up"]),
        full(prm["g_up"]), row(width), row(width), row(width), row(width), row(width),
        full(seg), full(tri),
    ]
    assert len(attn_specs) == N_ATTN_IN and len(rwkv_specs) == N_RWKV_IN
    attn, rw = pl.pallas_call(
        functools.partial(_mixers_kernel, n_heads=n_heads, width=width, nb=nb, n_pairs=n_pairs,
                          chunks_per_step=chunks_per_step),
        grid=(n_steps,),
        in_specs=attn_specs + rwkv_specs,
        out_specs=[aspec(lambda c: (a_blk(c), c % n_pairs)),
                   pl.BlockSpec((batch, rows_per_step, width), lambda c: (0, c, 0))],
        out_shape=[jax.ShapeDtypeStruct((t, width), BF16),
                   jax.ShapeDtypeStruct((batch, seq, width), BF16)],
        scratch_shapes=[
            pltpu.VMEM((n_pat * n_heads, ATTN_BLK, 2 * ATTN_BLK), F32),
            pltpu.VMEM((n_pat, ATTN_ROWS, LANES), F32),
            pltpu.VMEM((n_pat, ATTN_ROWS, LANES), F32),
            pltpu.VMEM((n_units, LANES, LANES), F32),
            pltpu.VMEM((batch, 1, w3), F32),
            pltpu.VMEM((batch, 1, n_lora), F32),
            big(), big(), big(), big(), big(), big(), big(),
            pltpu.VMEM((n_units, 1, LANES), F32),
            big(),
        ],
        compiler_params=_params(("arbitrary",)),
        name="mixers",
    )(qkv, qkv, qkv, qkv, qkv, band, *rwkv_args)
    return attn, rw.reshape(t, width)


def _out_proj_kernel(attn_ref, rw_ref, wa_ref, wr_ref, x_ref, out_ref):
    out_ref[...] = (x_ref[...] + jnp.dot(attn_ref[...], wa_ref[...], preferred_element_type=F32)
                    + jnp.dot(rw_ref[...], wr_ref[...], preferred_element_type=F32))


def _out_proj(attn, rw, w_out, x2, *, tm, tn):
    t, width = attn.shape
    d = w_out.shape[1]
    return pl.pallas_call(
        _out_proj_kernel,
        grid=(t // tm, d // tn),
        in_specs=[
            pl.BlockSpec((tm, width), lambda i, j: (i, 0)),
            pl.BlockSpec((tm, width), lambda i, j: (i, 0)),
            pl.BlockSpec((width, tn), lambda i, j: (0, j)),
            pl.BlockSpec((width, tn), lambda i, j: (1, j)),
            pl.BlockSpec((tm, tn), lambda i, j: (i, j)),
        ],
        out_specs=pl.BlockSpec((tm, tn), lambda i, j: (i, j)),
        out_shape=jax.ShapeDtypeStruct((t, d), F32),
        compiler_params=_params(("arbitrary", "arbitrary")),
        name="out_proj",
    )(attn, rw, w_out, w_out, x2)


FFN_DOWN_COLS = 512


def _ffn_kernel(h_ref, g_ref, wg_ref, wu_ref, wd_ref, og_ref, out_ref, hn_ref):
    j = pl.program_id(1)

    @pl.when(j == 0)
    def _():
        h = h_ref[...]
        ms = jnp.mean(h * h, axis=-1, keepdims=True)
        hn_ref[...] = (h * lax.rsqrt(ms + RMS_EPS) * g_ref[...]).astype(BF16)
        out_ref[...] = h

    hn = hn_ref[...]
    gate = jnp.dot(hn, wg_ref[...], preferred_element_type=F32)
    up = jnp.dot(hn, wu_ref[...], preferred_element_type=F32)
    act = (gate * jax.nn.sigmoid(gate) * up).astype(BF16)
    for c0 in range(0, out_ref.shape[1], FFN_DOWN_COLS):
        cols = slice(c0, c0 + FFN_DOWN_COLS)
        out_ref[:, cols] += jnp.dot(act, wd_ref[:, cols], preferred_element_type=F32)

    @pl.when(j == pl.num_programs(1) - 1)
    def _():
        h = out_ref[...]
        ms = jnp.mean(h * h, axis=-1, keepdims=True)
        out_ref[...] = h * lax.rsqrt(ms + RMS_EPS) * og_ref[...]


def _ffn(h, g, w_gate_up, w_down, out_g, *, tm, th):
    t, d = h.shape
    hidden = w_down.shape[0]
    n_h = hidden // th
    return pl.pallas_call(
        _ffn_kernel,
        grid=(t // tm, n_h),
        in_specs=[
            pl.BlockSpec((tm, d), lambda i, j: (i, 0)),
            pl.BlockSpec((1, d), lambda i, j: (0, 0)),
            pl.BlockSpec((d, th), lambda i, j: (0, j)),
            pl.BlockSpec((d, th), lambda i, j: (0, n_h + j)),
            pl.BlockSpec((th, d), lambda i, j: (j, 0)),
            pl.BlockSpec((1, d), lambda i, j: (0, 0)),
        ],
        out_specs=pl.BlockSpec((tm, d), lambda i, j: (i, 0)),
        out_shape=jax.ShapeDtypeStruct((t, d), F32),
        scratch_shapes=[pltpu.VMEM((tm, d), BF16)],
        compiler_params=pltpu.CompilerParams(dimension_semantics=("arbitrary", "arbitrary"),
                                             vmem_limit_bytes=FFN_VMEM_LIMIT),
        name="ffn",
    )(h, g, w_gate_up, w_gate_up, w_down, out_g)


def _pad_rows(a, rows):
    return jnp.pad(a, ((0, rows - a.shape[0]), (0, 0)))


def _layer(h, norm1_g, w_in, shift_mix, w0, w_up, a0, a_up, g_up, k_k, k_a, r_k, ln_w, ln_b,
           rel_bias_table, w_out, norm2_g, w_gate_up, w_down, out_g, *, batch, seq, tile_m):
    d = h.shape[1]
    width = w_out.shape[0] // 2
    n_heads = width // HEAD_DIM
    n_main = 6 * width
    rank_w, rank_a, rank_g = w_up.shape[0], a_up.shape[0], g_up.shape[0]
    n_lora = rank_w + rank_a + rank_g
    n_lora_pad = -(-n_lora // LANES) * LANES

    qkv, rkv, lora = _in_proj(h, norm1_g.reshape(1, d), w_in.astype(BF16), n_main=n_main,
                              n_lora_pad=n_lora_pad, tm=tile_m, tn=width)

    mix_rkv = shift_mix[:3 * width].reshape(1, 3 * width)
    mix_lora = jnp.pad(shift_mix[3 * width:], (0, n_lora_pad - n_lora)).reshape(1, n_lora_pad)
    zeros = lambda n: jnp.zeros((n, width), F32)
    prm = dict(
        mix_rkv=mix_rkv, mix_lora=mix_lora,
        w0=w0.reshape(1, width), a0=a0.reshape(1, width),
        w_up=_pad_rows(w_up, n_lora_pad).astype(BF16),
        a_up=_pad_rows(jnp.concatenate([zeros(rank_w), a_up]), n_lora_pad).astype(BF16),
        g_up=_pad_rows(jnp.concatenate([zeros(rank_w + rank_a), g_up]), n_lora_pad).astype(BF16),
        k_k=k_k.reshape(1, width), k_a=k_a.reshape(1, width), r_k=r_k.reshape(1, width),
        ln_w=ln_w.reshape(1, width), ln_b=ln_b.reshape(1, width),
    )
    attn, rw = _mixers(qkv, _band_rows(rel_bias_table), rkv, lora, prm, batch=batch, seq=seq, n_heads=n_heads)

    h1 = _out_proj(attn, rw, w_out.astype(BF16), h, tm=tile_m // 2, tn=d)
    return _ffn(h1, norm2_g.reshape(1, d), w_gate_up.astype(BF16), w_down.astype(BF16), out_g.reshape(1, d),
                tm=tile_m, th=512)


def kernel(x, norm1_g, w_in, rwkv_shift_mix, rwkv_w0, rwkv_w_up, rwkv_a0, rwkv_a_up, rwkv_g_up, rwkv_k_k, rwkv_k_a, rwkv_r_k, rwkv_ln_w, rwkv_ln_b, rel_bias_table, w_out, norm2_g, w_gate_up, w_down, final_g):
    batch, seq, d = x.shape
    depth = w_in.shape[0]
    assert depth == 1, "the final RMSNorm is fused into the last layer's FFN kernel"
    assert seq % DILATED_PATTERNS[-1][0] == 0 and seq % CHUNK == 0
    h = x.reshape(batch * seq, d)
    out = _layer(h, norm1_g[0], w_in[0], rwkv_shift_mix[0], rwkv_w0[0], rwkv_w_up[0], rwkv_a0[0],
                 rwkv_a_up[0], rwkv_g_up[0], rwkv_k_k[0], rwkv_k_a[0], rwkv_r_k[0], rwkv_ln_w[0],
                 rwkv_ln_b[0], rel_bias_table, w_out[0], norm2_g[0], w_gate_up[0], w_down[0], final_g,
                 batch=batch, seq=seq, tile_m=min(1024, batch * seq))
    return out.reshape(batch, seq, d)
```

```python
import functools
import math

import jax
import jax.numpy as jnp
from jax import lax
from jax.experimental import pallas as pl
from jax.experimental.pallas import tpu as pltpu

F32 = jnp.float32
BF16 = jnp.bfloat16

HEAD_DIM = 64
LANES = 128
DILATED_PATTERNS = ((128, 1), (512, 4), (2048, 16))
REL_BUCKETS = 32
REL_MAX_DIST = 2048
RMS_EPS = 1e-6
GN_EPS = 64e-5
DECAY_SCALE = math.exp(-0.5)
ATTN_SCALE = HEAD_DIM ** -0.5
MASK_VALUE = -1e30
LOG2_E = math.log2(math.e)
CHUNK = 64
SEG_LANES = 256
VMEM_LIMIT = 56 * 1024 * 1024
FFN_VMEM_LIMIT = 60 * 1024 * 1024


def _mm(a, b):
    return jnp.dot(a.astype(BF16), b.astype(BF16), preferred_element_type=F32)


def _mm_nt(a, b):
    return lax.dot_general(a.astype(BF16), b.astype(BF16), (((1,), (1,)), ((), ())),
                           preferred_element_type=F32)


def _mm_tn(a, b):
    return lax.dot_general(a.astype(BF16), b.astype(BF16), (((0,), (0,)), ((), ())),
                           preferred_element_type=F32)


def _mm_split(a, b_exact):
    hi = a.astype(BF16)
    lo = (a - hi.astype(F32)).astype(BF16)
    return (jnp.dot(hi, b_exact, preferred_element_type=F32)
            + jnp.dot(lo, b_exact, preferred_element_type=F32))


def _params(semantics):
    return pltpu.CompilerParams(dimension_semantics=semantics, vmem_limit_bytes=VMEM_LIMIT)


def _in_proj_kernel(x_ref, g_ref, w_ref, wl_ref, qkv_ref, rkv_ref, lora_ref, xn_ref, *, n_qkv, n_lora):
    j = pl.program_id(1)

    @pl.when(j == 0)
    def _():
        x = x_ref[...]
        ms = jnp.mean(x * x, axis=-1, keepdims=True)
        xn_ref[...] = (x * lax.rsqrt(ms + RMS_EPS) * g_ref[...]).astype(BF16)

    @pl.when(j < n_qkv)
    def _():
        qkv_ref[...] = jnp.dot(xn_ref[...], w_ref[...], preferred_element_type=F32)

    @pl.when((j >= n_qkv) & (j < 2 * n_qkv))
    def _():
        rkv_ref[...] = jnp.dot(xn_ref[...], w_ref[...], preferred_element_type=F32).astype(rkv_ref.dtype)

    @pl.when(j == 2 * n_qkv)
    def _():
        col = lax.broadcasted_iota(jnp.int32, (1, wl_ref.shape[1]), 1)
        wl = jnp.where(col < n_lora, wl_ref[...], jnp.zeros((), BF16))
        lora_ref[...] = jnp.dot(xn_ref[...], wl, preferred_element_type=F32).astype(lora_ref.dtype)


def _in_proj(x2, g, w, *, n_main, n_lora_pad, tm, tn):
    t, d = x2.shape
    n_lora = w.shape[1] - n_main
    n_qkv = (n_main // 2) // tn
    last = 2 * n_qkv - 1
    assert n_main % n_lora_pad == 0 and n_lora <= n_lora_pad
    return pl.pallas_call(
        functools.partial(_in_proj_kernel, n_qkv=n_qkv, n_lora=n_lora),
        grid=(t // tm, 2 * n_qkv + 1),
        in_specs=[
            pl.BlockSpec((tm, d), lambda i, j: (i, 0)),
            pl.BlockSpec((1, d), lambda i, j: (0, 0)),
            pl.BlockSpec((d, tn), lambda i, j: (0, jnp.minimum(j, last))),
            pl.BlockSpec((d, n_lora_pad), lambda i, j: (0, n_main // n_lora_pad)),
        ],
        out_specs=[
            pl.BlockSpec((tm, tn), lambda i, j: (i, jnp.minimum(j, n_qkv - 1))),
            pl.BlockSpec((tm, tn), lambda i, j: (i, jnp.clip(j - n_qkv, 0, n_qkv - 1))),
            pl.BlockSpec((tm, n_lora_pad), lambda i, j: (i, 0)),
        ],
        out_shape=[
            jax.ShapeDtypeStruct((t, n_main // 2), F32),
            jax.ShapeDtypeStruct((t, n_main // 2), BF16),
            jax.ShapeDtypeStruct((t, n_lora_pad), BF16),
        ],
        scratch_shapes=[pltpu.VMEM((tm, d), BF16)],
        compiler_params=_params(("arbitrary", "arbitrary")),
        name="in_proj",
    )(x2, g, w, w)


ATTN_BLK = 128
ATTN_ROWS = max(w for w, _ in DILATED_PATTERNS)
ATTN_GROUP = 2


def _attn_stages(q_ref, kp_ref, kc_ref, vp_ref, vc_ref, band_ref, out_ref, bias_ref, o_ref, lse_ref,
                 *, n_heads, first, n, hp):
    blk = ATTN_BLK

    @pl.when(first)
    def _():
        for i in range(bias_ref.shape[0]):
            row = jnp.broadcast_to(band_ref[i:i + 1, :], (blk, 2 * blk))
            bias_ref[i] = pltpu.roll(row, 0, 1, stride=1, stride_axis=0)

    yield

    lane = lax.broadcasted_iota(jnp.int32, (1, LANES), 1)
    in_head = [(lane >= half * HEAD_DIM) & (lane < (half + 1) * HEAD_DIM) for half in range(2)]
    key_idx = lax.broadcasted_iota(jnp.int32, (1, 2 * blk), 1)
    first_ok = (key_idx >= blk) | (n > 0)
    ones_tile = jnp.ones((2 * blk, LANES), BF16)
    zero = jnp.zeros((), BF16)

    units = [(bi, window, dilation, sub, r)
             for bi, (window, dilation) in enumerate(DILATED_PATTERNS)
             for sub in range(ATTN_ROWS // window) for r in range(dilation)]
    groups = [units[g0:g0 + ATTN_GROUP] for g0 in range(0, len(units), ATTN_GROUP)]

    def scores(group):
        work = []
        for bi, window, dilation, sub, r in group:
            cur = pl.ds(sub * window + r, blk, stride=dilation)
            if sub == 0:
                prev_k, prev_v = kp_ref, vp_ref
                prev = pl.ds(ATTN_ROWS - window + r, blk, stride=dilation)
            else:
                prev_k, prev_v = kc_ref, vc_ref
                prev = pl.ds((sub - 1) * window + r, blk, stride=dilation)
            q = (q_ref[cur, :] * (ATTN_SCALE * LOG2_E)).astype(BF16)
            k = jnp.concatenate([prev_k[prev, :], kc_ref[cur, :]], axis=0).astype(BF16)
            v = jnp.concatenate([prev_v[prev, :], vc_ref[cur, :]], axis=0).astype(BF16)
            q_both = jnp.concatenate([jnp.where(m, q, zero) for m in in_head], axis=0)
            work.append((bi, cur, sub == 0, _mm_nt(q_both, k), jnp.concatenate([v, ones_tile], axis=1)))
        return work

    def softmax(work):
        done = []
        for bi, cur, at_start, s, v in work:
            bias = bias_ref[pl.ds(bi * n_heads + 2 * hp, 2)]
            su = s + bias.reshape(2 * blk, 2 * blk)
            if at_start:
                su = jnp.where(first_ok, su, MASK_VALUE)
            mx = jnp.max(su, axis=-1, keepdims=True)
            done.append((bi, cur, jnp.exp2(su - mx).astype(BF16), mx, v))
        return done

    def values(done):
        for bi, cur, p, mx, v in done:
            res = jnp.dot(p, v, preferred_element_type=F32)
            ra, rb = res[:blk], res[blk:]
            la, lb = ra[:, LANES:], rb[:, LANES:]
            o_ref[bi, cur, :] = jnp.where(in_head[0], ra[:, :LANES] / la, rb[:, :LANES] / lb)
            lse_ref[bi, cur, :] = jnp.where(in_head[0], mx[:blk] + jnp.log2(la), mx[blk:] + jnp.log2(lb))

    pending = None
    for group in groups:
        work = scores(group)
        if pending is not None:
            values(pending)
        pending = softmax(work)
        yield
    values(pending)
    yield

    lses = [lse_ref[bi] for bi in range(len(DILATED_PATTERNS))]
    top = functools.reduce(jnp.maximum, lses)
    es = [jnp.exp2(l - top) for l in lses]
    num = functools.reduce(jnp.add, [e * o_ref[bi] for bi, e in enumerate(es)])
    out_ref[...] = (num / functools.reduce(jnp.add, es)).astype(out_ref.dtype)


def _t5_bucket(dist):
    exact = REL_BUCKETS // 2
    d_f = jnp.maximum(dist, 1).astype(F32)
    large = exact + (jnp.log(d_f / exact) / math.log(REL_MAX_DIST / exact)
                     * (REL_BUCKETS - exact)).astype(jnp.int32)
    large = jnp.minimum(large, REL_BUCKETS - 1)
    return jnp.where(dist < exact, dist, large)


def _band_rows(bias_table):
    blk = ATTN_BLK
    rel = blk - jnp.arange(2 * blk)
    band = (rel >= 0) & (rel <= blk)
    rows = []
    for _, dilation in DILATED_PATTERNS:
        bias = bias_table[_t5_bucket(jnp.clip(rel, 0, blk) * dilation)]
        rows.append(jnp.where(band[:, None], bias.astype(F32) * LOG2_E, MASK_VALUE).T)
    return jnp.concatenate(rows, axis=0)


def _rwkv_stages(rkv_ref, lora_ref, mix_rkv_ref, mix_lora_ref, w0_ref, w_up_ref, a0_ref, a_up_ref,
                 g_up_ref, kk_ref, ka_ref, rk_ref, lnw_ref, lnb_ref, seg_ref, tri_ref,
                 out_ref,
                 state_ref, carry_rkv_ref, carry_lora_ref,
                 rt_ref, at_ref, bh_ref, kh_ref, bc_ref, kc_ref, v_ref, pc_ref, y_ref, *, width, lo):
    n_pairs = width // LANES
    n_seq, ch = rkv_ref.shape[0], CHUNK
    rows = n_seq * ch

    row = lax.broadcasted_iota(jnp.int32, (rows, 1), 0) % ch

    def per_seq(row_of):
        return jnp.concatenate([jnp.broadcast_to(row_of(b), (ch, row_of(b).shape[-1]))
                                for b in range(n_seq)], axis=0)

    def token_shift(z_ref, carry_ref, mix):
        z = z_ref[:, lo:lo + ch, :].astype(F32).reshape(rows, z_ref.shape[-1])
        prev = jnp.where(row == 0, per_seq(lambda b: carry_ref[b]), pltpu.roll(z, 1, axis=0))
        for b in range(n_seq):
            carry_ref[b] = z[(b + 1) * ch - 1:(b + 1) * ch, :]
        return z + (prev - z) * mix

    z = token_shift(rkv_ref, carry_rkv_ref, mix_rkv_ref[...])
    zl = token_shift(lora_ref, carry_lora_ref, mix_lora_ref[...])
    r, k, v = z[:, :width], z[:, width:2 * width], z[:, 2 * width:]

    lw = -DECAY_SCALE * jax.nn.sigmoid(w0_ref[...] + _mm(jnp.tanh(zl), w_up_ref[...]))
    a_sig = jax.nn.sigmoid(a0_ref[...] + _mm(zl, a_up_ref[...]))
    gate = _mm(jax.nn.sigmoid(zl), g_up_ref[...])

    seg = seg_ref[...]

    def head_sums(x):
        tiles = [_mm_split(x[:, t:t + SEG_LANES], seg) for t in range(0, width, SEG_LANES)]
        return jnp.concatenate(tiles, axis=1)

    kk = k * kk_ref[...]
    kk = kk / jnp.maximum(jnp.sqrt(head_sums(kk * kk)), 1e-12)
    k = k * (1.0 + (a_sig - 1.0) * ka_ref[...])
    bonus = head_sums(r * k * rk_ref[...]) * v
    a_in = -kk
    b_in = kk * a_sig

    lw_hi = lw.astype(BF16)
    lw_lo = (lw - lw_hi.astype(F32)).astype(BF16)
    tri = tri_ref[...]
    cum = (jnp.dot(tri, lw_hi, preferred_element_type=F32)
           + jnp.dot(tri, lw_lo, preferred_element_type=F32))
    cum_end = per_seq(lambda b: cum[(b + 1) * ch - 1:(b + 1) * ch, :])
    e_neg = jnp.exp(-cum)
    e_end = jnp.exp(cum_end - cum)

    n_units = n_seq * n_pairs

    def put(ref, val):
        for b in range(n_seq):
            for p in range(n_pairs):
                ref[b * n_pairs + p] = val[b * ch:b * ch + ref.shape[1], p * LANES:(p + 1) * LANES]

    put(rt_ref, r * jnp.exp(cum))
    put(at_ref, a_in * jnp.exp(cum - lw))
    put(bh_ref, b_in * e_neg)
    put(kh_ref, k * e_neg)
    put(bc_ref, b_in * e_end)
    put(kc_ref, k * e_end)
    put(v_ref, v)
    put(pc_ref, jnp.exp(cum_end))
    yield

    ti = lax.broadcasted_iota(jnp.int32, (ch, 2 * ch), 0)
    si = lax.broadcasted_iota(jnp.int32, (ch, 2 * ch), 1) % ch
    incl = ti >= si
    strict = ti > si
    lane = lax.broadcasted_iota(jnp.int32, (1, 2 * LANES), 1)
    head_a2 = (lane % LANES) < HEAD_DIM
    head_a = head_a2[:, :LANES]
    di = lax.broadcasted_iota(jnp.int32, (LANES, LANES), 0)
    dj = lax.broadcasted_iota(jnp.int32, (LANES, LANES), 1)
    same_head = (di < HEAD_DIM) == (dj < HEAD_DIM)
    diag = di == dj

    zero = jnp.zeros((), BF16)

    def stack(x, mask):
        x = x.astype(BF16)
        return jnp.concatenate([jnp.where(mask, x, zero), jnp.where(mask, zero, x)], axis=0)

    def block_diag(m):
        m = m.astype(BF16)
        return jnp.where(same_head, jnp.concatenate([m, m], axis=0), zero)

    pairs = range(n_units)
    zero_tile = jnp.zeros((2 * ch, LANES), BF16)
    a_rb, a_ab, a_rk, a_ak, vst = [], [], [], [], []
    for p in pairs:
        lhs = jnp.concatenate([rt_ref[p], at_ref[p]], axis=0)
        rhs = jnp.concatenate([stack(bh_ref[p], head_a), stack(kh_ref[p], head_a)], axis=0)
        a_bk = _mm_nt(lhs, rhs).astype(BF16)
        a_rb.append(jnp.where(incl, a_bk[:ch, :LANES], zero))
        a_ab.append(jnp.where(strict, a_bk[ch:, :LANES], zero))
        a_rk.append(jnp.where(incl, a_bk[:ch, LANES:], zero))
        a_ak.append(jnp.where(strict, a_bk[ch:, LANES:], zero))
        vst.append(stack(v_ref[p], head_a))
    yield

    xs = [jnp.concatenate([at_ref[p], _mm(a_ak[p], vst[p])], axis=1) for p in pairs]
    yield
    nk = a_ab
    for _ in range(int(math.log2(ch)) - 1):
        nk_next = [_mm(n, block_diag(n)) for n in nk]
        xs = [x + _mm(n, stack(x, head_a2)) for n, x in zip(nk, xs)]
        nk = nk_next
        yield
    xs = [x + _mm(n, stack(x, head_a2)) for n, x in zip(nk, xs)]
    yield

    ax = [_mm(jnp.concatenate([a_rb[p], a_rk[p]], axis=1),
              jnp.concatenate([stack(xs[p], head_a2), jnp.concatenate([zero_tile, vst[p]], axis=1)], axis=0))
          for p in pairs]
    mg = [_mm_tn(jnp.concatenate([bc_ref[p], kc_ref[p]], axis=0),
                 jnp.concatenate([xs[p], jnp.concatenate([jnp.zeros((ch, LANES), F32), v_ref[p]], axis=1)],
                                 axis=0))
          for p in pairs]
    yield
    for p in pairs:
        q_acc = rt_ref[p] + ax[p][:, :LANES]
        m_mat = jnp.where(same_head, mg[p][:, :LANES], 0.0) + jnp.where(diag, pc_ref[p], 0.0)
        g_mat = jnp.where(same_head, mg[p][:, LANES:], 0.0)
        res = _mm(jnp.concatenate([q_acc, m_mat], axis=0), state_ref[p])
        y_ref[p] = res[:ch] + ax[p][:, LANES:]
        state_ref[p] = res[ch:] + g_mat
    yield

    y = jnp.concatenate([jnp.concatenate([y_ref[b * n_pairs + p] for p in range(n_pairs)], axis=1)
                         for b in range(n_seq)], axis=0)
    inv_n = 1.0 / HEAD_DIM
    mu = head_sums(y) * inv_n
    d = y - mu
    var = head_sums(d * d) * inv_n
    yn = d * lax.rsqrt(var + GN_EPS) * lnw_ref[...] + lnb_ref[...]
    out_ref[:, lo:lo + ch, :] = ((yn + bonus) * gate).reshape(n_seq, ch, width).astype(out_ref.dtype)


N_ATTN_IN, N_ATTN_SCRATCH = 6, 3
N_RWKV_IN, N_RWKV_SCRATCH = 16, 12
RWKV_STAGES_PER_ATTN_STAGE = 1


def _mixers_kernel(*refs, n_heads, width, nb, n_pairs, chunks_per_step):
    attn_in, refs = refs[:N_ATTN_IN], refs[N_ATTN_IN:]
    rwkv_in, refs = refs[:N_RWKV_IN], refs[N_RWKV_IN:]
    attn_out, rw_out, refs = refs[0], refs[1], refs[2:]
    attn_scr, rwkv_scr = refs[:N_ATTN_SCRATCH], refs[N_ATTN_SCRATCH:]
    c = pl.program_id(0)
    state_ref, carry_rkv_ref, carry_lora_ref = rwkv_scr[:3]

    @pl.when(c == 0)
    def _():
        state_ref[...] = jnp.zeros_like(state_ref)
        carry_rkv_ref[...] = jnp.zeros_like(carry_rkv_ref)
        carry_lora_ref[...] = jnp.zeros_like(carry_lora_ref)

    attn = _attn_stages(*attn_in, attn_out, *attn_scr, n_heads=n_heads, first=c == 0,
                        n=(c // n_pairs) % nb, hp=c % n_pairs)
    next(attn)

    def rwkv_chunks():
        for i in range(chunks_per_step):
            yield from _rwkv_stages(*rwkv_in, rw_out, *rwkv_scr, width=width, lo=i * CHUNK)

    rwkv = rwkv_chunks()
    running = [True, True]
    while any(running):
        for _ in range(RWKV_STAGES_PER_ATTN_STAGE):
            if running[0] and next(rwkv, "done") == "done":
                running[0] = False
        if running[1] and next(attn, "done") == "done":
            running[1] = False


def _mixers(qkv, band, rkv, lora, prm, *, batch, seq, n_heads):
    t, w3 = rkv.shape
    width = w3 // 3
    n_lora = lora.shape[1]
    ch = CHUNK
    n_pairs = width // LANES
    nb = seq // ATTN_ROWS
    n_pat = len(DILATED_PATTERNS)
    n_steps = batch * nb * n_pairs
    assert (seq // ch) % n_steps == 0, "RWKV chunks must divide evenly over the attention steps"
    chunks_per_step = (seq // ch) // n_steps
    rows_per_step = chunks_per_step * ch

    a_blk = lambda c: (c // n_pairs) // nb * nb + (c // n_pairs) % nb
    a_prev = lambda c: (c // n_pairs) // nb * nb + jnp.maximum((c // n_pairs) % nb - 1, 0)
    aspec = lambda f: pl.BlockSpec((ATTN_ROWS, LANES), f)
    attn_specs = [
        aspec(lambda c: (a_blk(c), c % n_pairs)),
        aspec(lambda c: (a_prev(c), n_pairs + c % n_pairs)),
        aspec(lambda c: (a_blk(c), n_pairs + c % n_pairs)),
        aspec(lambda c: (a_prev(c), 2 * n_pairs + c % n_pairs)),
        aspec(lambda c: (a_blk(c), 2 * n_pairs + c % n_pairs)),
        pl.BlockSpec(band.shape, lambda c: (0, 0)),
    ]

    row = lambda n: pl.BlockSpec((1, n), lambda c: (0, 0))
    full = lambda a: pl.BlockSpec(a.shape, lambda c: (0,) * a.ndim)
    head_id = jnp.arange(SEG_LANES) // HEAD_DIM
    seg = (head_id[:, None] == head_id[None, :]).astype(BF16)
    pos = jnp.arange(batch * ch)
    tri = ((pos[:, None] >= pos[None, :]) & (pos[:, None] // ch == pos[None, :] // ch)).astype(BF16)
    n_units = batch * n_pairs
    big = lambda: pltpu.VMEM((n_units, ch, LANES), F32)
    rwkv_args = (rkv.reshape(batch, seq, w3), lora.reshape(batch, seq, n_lora), prm["mix_rkv"],
                 prm["mix_lora"], prm["w0"], prm["w_up"], prm["a0"], prm["a_up"], prm["g_up"], prm["k_k"],
                 prm["k_a"], prm["r_k"], prm["ln_w"], prm["ln_b"], seg, tri)
    rwkv_specs = [
        pl.BlockSpec((batch, rows_per_step, w3), lambda c: (0, c, 0)),
        pl.BlockSpec((batch, rows_per_step, n_lora), lambda c: (0, c, 0)),
        row(w3), row(n_lora), row(width), full(prm["w_up"]), row(width), full(prm["a_up"]),
        full(prm["g_up"]), row(width), row(width), row(width), row(width), row(width),
        full(seg), full(tri),
    ]
    assert len(attn_specs) == N_ATTN_IN and len(rwkv_specs) == N_RWKV_IN
    attn, rw = pl.pallas_call(
        functools.partial(_mixers_kernel, n_heads=n_heads, width=width, nb=nb, n_pairs=n_pairs,
                          chunks_per_step=chunks_per_step),
        grid=(n_steps,),
        in_specs=attn_specs + rwkv_specs,
        out_specs=[aspec(lambda c: (a_blk(c), c % n_pairs)),
                   pl.BlockSpec((batch, rows_per_step, width), lambda c: (0, c, 0))],
        out_shape=[jax.ShapeDtypeStruct((t, width), BF16),
                   jax.ShapeDtypeStruct((batch, seq, width), BF16)],
        scratch_shapes=[
            pltpu.VMEM((n_pat * n_heads, ATTN_BLK, 2 * ATTN_BLK), F32),
            pltpu.VMEM((n_pat, ATTN_ROWS, LANES), F32),
            pltpu.VMEM((n_pat, ATTN_ROWS, LANES), F32),
            pltpu.VMEM((n_units, LANES, LANES), F32),
            pltpu.VMEM((batch, 1, w3), F32),
            pltpu.VMEM((batch, 1, n_lora), F32),
            big(), big(), big(), big(), big(), big(), big(),
            pltpu.VMEM((n_units, 1, LANES), F32),
            big(),
        ],
        compiler_params=_params(("arbitrary",)),
        name="mixers",
    )(qkv, qkv, qkv, qkv, qkv, band, *rwkv_args)
    return attn, rw.reshape(t, width)


def _out_proj_kernel(attn_ref, rw_ref, wa_ref, wr_ref, x_ref, out_ref):
    out_ref[...] = (x_ref[...] + jnp.dot(attn_ref[...], wa_ref[...], preferred_element_type=F32)
                    + jnp.dot(rw_ref[...], wr_ref[...], preferred_element_type=F32))


def _out_proj(attn, rw, w_out, x2, *, tm, tn):
    t, width = attn.shape
    d = w_out.shape[1]
    return pl.pallas_call(
        _out_proj_kernel,
        grid=(t // tm, d // tn),
        in_specs=[
            pl.BlockSpec((tm, width), lambda i, j: (i, 0)),
            pl.BlockSpec((tm, width), lambda i, j: (i, 0)),
            pl.BlockSpec((width, tn), lambda i, j: (0, j)),
            pl.BlockSpec((width, tn), lambda i, j: (1, j)),
            pl.BlockSpec((tm, tn), lambda i, j: (i, j)),
        ],
        out_specs=pl.BlockSpec((tm, tn), lambda i, j: (i, j)),
        out_shape=jax.ShapeDtypeStruct((t, d), F32),
        compiler_params=_params(("arbitrary", "arbitrary")),
        name="out_proj",
    )(attn, rw, w_out, w_out, x2)


FFN_DOWN_COLS = 512


def _ffn_kernel(h_ref, g_ref, wg_ref, wu_ref, wd_ref, og_ref, out_ref, hn_ref):
    j = pl.program_id(1)

    @pl.when(j == 0)
    def _():
        h = h_ref[...]
        ms = jnp.mean(h * h, axis=-1, keepdims=True)
        hn_ref[...] = (h * lax.rsqrt(ms + RMS_EPS) * g_ref[...]).astype(BF16)
        out_ref[...] = h

    hn = hn_ref[...]
    gate = jnp.dot(hn, wg_ref[...], preferred_element_type=F32)
    up = jnp.dot(hn, wu_ref[...], preferred_element_type=F32)
    act = (gate * jax.nn.sigmoid(gate) * up).astype(BF16)
    for c0 in range(0, out_ref.shape[1], FFN_DOWN_COLS):
        cols = slice(c0, c0 + FFN_DOWN_COLS)
        out_ref[:, cols] += jnp.dot(act, wd_ref[:, cols], preferred_element_type=F32)

    @pl.when(j == pl.num_programs(1) - 1)
    def _():
        h = out_ref[...]
        ms = jnp.mean(h * h, axis=-1, keepdims=True)
        out_ref[...] = h * lax.rsqrt(ms + RMS_EPS) * og_ref[...]


def _ffn(h, g, w_gate_up, w_down, out_g, *, tm, th):
    t, d = h.shape
    hidden = w_down.shape[0]
    n_h = hidden // th
    return pl.pallas_call(
        _ffn_kernel,
        grid=(t // tm, n_h),
        in_specs=[
            pl.BlockSpec((tm, d), lambda i, j: (i, 0)),
            pl.BlockSpec((1, d), lambda i, j: (0, 0)),
            pl.BlockSpec((d, th), lambda i, j: (0, j)),
            pl.BlockSpec((d, th), lambda i, j: (0, n_h + j)),
            pl.BlockSpec((th, d), lambda i, j: (j, 0)),
            pl.BlockSpec((1, d), lambda i, j: (0, 0)),
        ],
        out_specs=pl.BlockSpec((tm, d), lambda i, j: (i, 0)),
        out_shape=jax.ShapeDtypeStruct((t, d), F32),
        scratch_shapes=[pltpu.VMEM((tm, d), BF16)],
        compiler_params=pltpu.CompilerParams(dimension_semantics=("arbitrary", "arbitrary"),
                                             vmem_limit_bytes=FFN_VMEM_LIMIT),
        name="ffn",
    )(h, g, w_gate_up, w_gate_up, w_down, out_g)


def _pad_rows(a, rows):
    return jnp.pad(a, ((0, rows - a.shape[0]), (0, 0)))


def _layer(h, norm1_g, w_in, shift_mix, w0, w_up, a0, a_up, g_up, k_k, k_a, r_k, ln_w, ln_b,
           rel_bias_table, w_out, norm2_g, w_gate_up, w_down, out_g, *, batch, seq, tile_m):
    d = h.shape[1]
    width = w_out.shape[0] // 2
    n_heads = width // HEAD_DIM
    n_main = 6 * width
    rank_w, rank_a, rank_g = w_up.shape[0], a_up.shape[0], g_up.shape[0]
    n_lora = rank_w + rank_a + rank_g
    n_lora_pad = -(-n_lora // LANES) * LANES

    qkv, rkv, lora = _in_proj(h, norm1_g.reshape(1, d), w_in.astype(BF16), n_main=n_main,
                              n_lora_pad=n_lora_pad, tm=tile_m, tn=width)

    mix_rkv = shift_mix[:3 * width].reshape(1, 3 * width)
    mix_lora = jnp.pad(shift_mix[3 * width:], (0, n_lora_pad - n_lora)).reshape(1, n_lora_pad)
    zeros = lambda n: jnp.zeros((n, width), F32)
    prm = dict(
        mix_rkv=mix_rkv, mix_lora=mix_lora,
        w0=w0.reshape(1, width), a0=a0.reshape(1, width),
        w_up=_pad_rows(w_up, n_lora_pad).astype(BF16),
        a_up=_pad_rows(jnp.concatenate([zeros(rank_w), a_up]), n_lora_pad).astype(BF16),
        g_up=_pad_rows(jnp.concatenate([zeros(rank_w + rank_a), g_up]), n_lora_pad).astype(BF16),
        k_k=k_k.reshape(1, width), k_a=k_a.reshape(1, width), r_k=r_k.reshape(1, width),
        ln_w=ln_w.reshape(1, width), ln_b=ln_b.reshape(1, width),
    )
    attn, rw = _mixers(qkv, _band_rows(rel_bias_table), rkv, lora, prm, batch=batch, seq=seq, n_heads=n_heads)

    h1 = _out_proj(attn, rw, w_out.astype(BF16), h, tm=tile_m // 2, tn=d)
    return _ffn(h1, norm2_g.reshape(1, d), w_gate_up.astype(BF16), w_down.astype(BF16), out_g.reshape(1, d),
                tm=tile_m, th=512)


def kernel(x, norm1_g, w_in, rwkv_shift_mix, rwkv_w0, rwkv_w_up, rwkv_a0, rwkv_a_up, rwkv_g_up, rwkv_k_k, rwkv_k_a, rwkv_r_k, rwkv_ln_w, rwkv_ln_b, rel_bias_table, w_out, norm2_g, w_gate_up, w_down, final_g):
    batch, seq, d = x.shape
    depth = w_in.shape[0]
    assert depth == 1, "the final RMSNorm is fused into the last layer's FFN kernel"
    assert seq % DILATED_PATTERNS[-1][0] == 0 and seq % CHUNK == 0
    h = x.reshape(batch * seq, d)
    out = _layer(h, norm1_g[0], w_in[0], rwkv_shift_mix[0], rwkv_w0[0], rwkv_w_up[0], rwkv_a0[0],
                 rwkv_a_up[0], rwkv_g_up[0], rwkv_k_k[0], rwkv_k_a[0], rwkv_r_k[0], rwkv_ln_w[0],
                 rwkv_ln_b[0], rel_bias_table, w_out[0], norm2_g[0], w_gate_up[0], w_down[0], final_g,
                 batch=batch, seq=seq, tile_m=min(1024, batch * seq))
    return out.reshape(batch, seq, d)
```

```python
import functools
import math

import jax
import jax.numpy as jnp
from jax import lax
from jax.experimental import pallas as pl
from jax.experimental.pallas import tpu as pltpu

F32 = jnp.float32
BF16 = jnp.bfloat16

HEAD_DIM = 64
LANES = 128
DILATED_PATTERNS = ((128, 1), (512, 4), (2048, 16))
REL_BUCKETS = 32
REL_MAX_DIST = 2048
RMS_EPS = 1e-6
GN_EPS = 64e-5
DECAY_SCALE = math.exp(-0.5)
ATTN_SCALE = HEAD_DIM ** -0.5
MASK_VALUE = -1e30
LOG2_E = math.log2(math.e)
CHUNK = 64
SEG_LANES = 256
VMEM_LIMIT = 56 * 1024 * 1024
FFN_VMEM_LIMIT = 60 * 1024 * 1024


def _mm(a, b):
    return jnp.dot(a.astype(BF16), b.astype(BF16), preferred_element_type=F32)


def _mm_nt(a, b):
    return lax.dot_general(a.astype(BF16), b.astype(BF16), (((1,), (1,)), ((), ())),
                           preferred_element_type=F32)


def _mm_tn(a, b):
    return lax.dot_general(a.astype(BF16), b.astype(BF16), (((0,), (0,)), ((), ())),
                           preferred_element_type=F32)


def _mm_split(a, b_exact):
    hi = a.astype(BF16)
    lo = (a - hi.astype(F32)).astype(BF16)
    return (jnp.dot(hi, b_exact, preferred_element_type=F32)
            + jnp.dot(lo, b_exact, preferred_element_type=F32))


def _params(semantics):
    return pltpu.CompilerParams(dimension_semantics=semantics, vmem_limit_bytes=VMEM_LIMIT)


def _in_proj_kernel(x_ref, g_ref, w_ref, wl_ref, qkv_ref, rkv_ref, lora_ref, xn_ref, *, n_qkv, n_lora):
    j = pl.program_id(1)

    @pl.when(j == 0)
    def _():
        x = x_ref[...]
        ms = jnp.mean(x * x, axis=-1, keepdims=True)
        xn_ref[...] = (x * lax.rsqrt(ms + RMS_EPS) * g_ref[...]).astype(BF16)
        col = lax.broadcasted_iota(jnp.int32, (1, wl_ref.shape[1]), 1)
        wl = jnp.where(col < n_lora, wl_ref[...], jnp.zeros((), BF16))
        lora_ref[...] = jnp.dot(xn_ref[...], wl, preferred_element_type=F32)

    @pl.when((j >= 1) & (j <= n_qkv))
    def _():
        qkv_ref[...] = jnp.dot(xn_ref[...], w_ref[...], preferred_element_type=F32)

    @pl.when(j > n_qkv)
    def _():
        rkv_ref[...] = jnp.dot(xn_ref[...], w_ref[...], preferred_element_type=F32)


def _in_proj(x2, g, w, *, n_main, n_lora_pad, tm, tn):
    t, d = x2.shape
    n_lora = w.shape[1] - n_main
    n_qkv = (n_main // 2) // tn
    last = 2 * n_qkv - 1
    assert n_main % n_lora_pad == 0 and n_lora <= n_lora_pad
    return pl.pallas_call(
        functools.partial(_in_proj_kernel, n_qkv=n_qkv, n_lora=n_lora),
        grid=(t // tm, 2 * n_qkv + 1),
        in_specs=[
            pl.BlockSpec((tm, d), lambda i, j: (i, 0)),
            pl.BlockSpec((1, d), lambda i, j: (0, 0)),
            pl.BlockSpec((d, tn), lambda i, j: (0, jnp.clip(j - 1, 0, last))),
            pl.BlockSpec((d, n_lora_pad), lambda i, j: (0, n_main // n_lora_pad)),
        ],
        out_specs=[
            pl.BlockSpec((tm, tn), lambda i, j: (i, jnp.clip(j - 1, 0, n_qkv - 1))),
            pl.BlockSpec((tm, tn), lambda i, j: (i, jnp.clip(j - 1 - n_qkv, 0, n_qkv - 1))),
            pl.BlockSpec((tm, n_lora_pad), lambda i, j: (i, 0)),
        ],
        out_shape=[
            jax.ShapeDtypeStruct((t, n_main // 2), F32),
            jax.ShapeDtypeStruct((t, n_main // 2), F32),
            jax.ShapeDtypeStruct((t, n_lora_pad), F32),
        ],
        scratch_shapes=[pltpu.VMEM((tm, d), BF16)],
        compiler_params=_params(("arbitrary", "arbitrary")),
        name="in_proj",
    )(x2, g, w, w)


ATTN_BLK = 128
ATTN_ROWS = max(w for w, _ in DILATED_PATTERNS)
ATTN_GROUP = 2


def _attn_stages(q_ref, kp_ref, kc_ref, vp_ref, vc_ref, band_ref, out_ref, bias_ref, o_ref, lse_ref,
                 *, n_heads, first, n, hp):
    blk = ATTN_BLK

    @pl.when(first)
    def _():
        for i in range(bias_ref.shape[0]):
            row = jnp.broadcast_to(band_ref[i:i + 1, :], (blk, 2 * blk))
            bias_ref[i] = pltpu.roll(row, 0, 1, stride=1, stride_axis=0)

    yield

    lane = lax.broadcasted_iota(jnp.int32, (1, LANES), 1)
    in_head = [(lane >= half * HEAD_DIM) & (lane < (half + 1) * HEAD_DIM) for half in range(2)]
    key_idx = lax.broadcasted_iota(jnp.int32, (1, 2 * blk), 1)
    first_ok = (key_idx >= blk) | (n > 0)
    ones_tile = jnp.ones((2 * blk, LANES), BF16)
    zero = jnp.zeros((), BF16)

    units = [(bi, window, dilation, sub, r)
             for bi, (window, dilation) in enumerate(DILATED_PATTERNS)
             for sub in range(ATTN_ROWS // window) for r in range(dilation)]
    groups = [units[g0:g0 + ATTN_GROUP] for g0 in range(0, len(units), ATTN_GROUP)]

    def scores(group):
        work = []
        for bi, window, dilation, sub, r in group:
            cur = pl.ds(sub * window + r, blk, stride=dilation)
            if sub == 0:
                prev_k, prev_v = kp_ref, vp_ref
                prev = pl.ds(ATTN_ROWS - window + r, blk, stride=dilation)
            else:
                prev_k, prev_v = kc_ref, vc_ref
                prev = pl.ds((sub - 1) * window + r, blk, stride=dilation)
            q = (q_ref[cur, :] * (ATTN_SCALE * LOG2_E)).astype(BF16)
            k = jnp.concatenate([prev_k[prev, :], kc_ref[cur, :]], axis=0).astype(BF16)
            v = jnp.concatenate([prev_v[prev, :], vc_ref[cur, :]], axis=0).astype(BF16)
            q_both = jnp.concatenate([jnp.where(m, q, zero) for m in in_head], axis=0)
            work.append((bi, cur, sub == 0, _mm_nt(q_both, k), jnp.concatenate([v, ones_tile], axis=1)))
        return work

    def softmax(work):
        done = []
        for bi, cur, at_start, s, v in work:
            bias = bias_ref[pl.ds(bi * n_heads + 2 * hp, 2)]
            su = s + bias.reshape(2 * blk, 2 * blk)
            if at_start:
                su = jnp.where(first_ok, su, MASK_VALUE)
            mx = jnp.max(su, axis=-1, keepdims=True)
            done.append((bi, cur, jnp.exp2(su - mx).astype(BF16), mx, v))
        return done

    def values(done):
        for bi, cur, p, mx, v in done:
            res = jnp.dot(p, v, preferred_element_type=F32)
            ra, rb = res[:blk], res[blk:]
            la, lb = ra[:, LANES:], rb[:, LANES:]
            o_ref[bi, cur, :] = jnp.where(in_head[0], ra[:, :LANES] / la, rb[:, :LANES] / lb)
            lse_ref[bi, cur, :] = jnp.where(in_head[0], mx[:blk] + jnp.log2(la), mx[blk:] + jnp.log2(lb))

    pending = None
    for group in groups:
        work = scores(group)
        if pending is not None:
            values(pending)
        pending = softmax(work)
        yield
    values(pending)
    yield

    lses = [lse_ref[bi] for bi in range(len(DILATED_PATTERNS))]
    top = functools.reduce(jnp.maximum, lses)
    es = [jnp.exp2(l - top) for l in lses]
    num = functools.reduce(jnp.add, [e * o_ref[bi] for bi, e in enumerate(es)])
    out_ref[...] = (num / functools.reduce(jnp.add, es)).astype(out_ref.dtype)


def _t5_bucket(dist):
    exact = REL_BUCKETS // 2
    d_f = jnp.maximum(dist, 1).astype(F32)
    large = exact + (jnp.log(d_f / exact) / math.log(REL_MAX_DIST / exact)
                     * (REL_BUCKETS - exact)).astype(jnp.int32)
    large = jnp.minimum(large, REL_BUCKETS - 1)
    return jnp.where(dist < exact, dist, large)


def _band_rows(bias_table):
    blk = ATTN_BLK
    rel = blk - jnp.arange(2 * blk)
    band = (rel >= 0) & (rel <= blk)
    rows = []
    for _, dilation in DILATED_PATTERNS:
        bias = bias_table[_t5_bucket(jnp.clip(rel, 0, blk) * dilation)]
        rows.append(jnp.where(band[:, None], bias.astype(F32) * LOG2_E, MASK_VALUE).T)
    return jnp.concatenate(rows, axis=0)


def _rwkv_stages(rkv_ref, lora_ref, mix_rkv_ref, mix_lora_ref, w0_ref, w_up_ref, a0_ref, a_up_ref,
                 g_up_ref, kk_ref, ka_ref, rk_ref, lnw_ref, lnb_ref, seg_ref, tri_ref,
                 out_ref,
                 state_ref, carry_rkv_ref, carry_lora_ref,
                 rt_ref, at_ref, bh_ref, kh_ref, bc_ref, kc_ref, v_ref, pc_ref, y_ref, *, width, lo):
    n_pairs = width // LANES
    n_seq, ch = rkv_ref.shape[0], CHUNK
    rows = n_seq * ch

    row = lax.broadcasted_iota(jnp.int32, (rows, 1), 0) % ch

    def per_seq(row_of):
        return jnp.concatenate([jnp.broadcast_to(row_of(b), (ch, row_of(b).shape[-1]))
                                for b in range(n_seq)], axis=0)

    def token_shift(z_ref, carry_ref, mix):
        z = z_ref[:, lo:lo + ch, :].reshape(rows, z_ref.shape[-1])
        prev = jnp.where(row == 0, per_seq(lambda b: carry_ref[b]), pltpu.roll(z, 1, axis=0))
        for b in range(n_seq):
            carry_ref[b] = z[(b + 1) * ch - 1:(b + 1) * ch, :]
        return z + (prev - z) * mix

    z = token_shift(rkv_ref, carry_rkv_ref, mix_rkv_ref[...])
    zl = token_shift(lora_ref, carry_lora_ref, mix_lora_ref[...])
    r, k, v = z[:, :width], z[:, width:2 * width], z[:, 2 * width:]

    lw = -DECAY_SCALE * jax.nn.sigmoid(w0_ref[...] + _mm(jnp.tanh(zl), w_up_ref[...]))
    a_sig = jax.nn.sigmoid(a0_ref[...] + _mm(zl, a_up_ref[...]))
    gate = _mm(jax.nn.sigmoid(zl), g_up_ref[...])

    seg = seg_ref[...]

    def head_sums(x):
        tiles = [_mm_split(x[:, t:t + SEG_LANES], seg) for t in range(0, width, SEG_LANES)]
        return jnp.concatenate(tiles, axis=1)

    kk = k * kk_ref[...]
    kk = kk / jnp.maximum(jnp.sqrt(head_sums(kk * kk)), 1e-12)
    k = k * (1.0 + (a_sig - 1.0) * ka_ref[...])
    bonus = head_sums(r * k * rk_ref[...]) * v
    a_in = -kk
    b_in = kk * a_sig

    lw_hi = lw.astype(BF16)
    lw_lo = (lw - lw_hi.astype(F32)).astype(BF16)
    tri = tri_ref[...]
    cum = (jnp.dot(tri, lw_hi, preferred_element_type=F32)
           + jnp.dot(tri, lw_lo, preferred_element_type=F32))
    cum_end = per_seq(lambda b: cum[(b + 1) * ch - 1:(b + 1) * ch, :])
    e_neg = jnp.exp(-cum)
    e_end = jnp.exp(cum_end - cum)

    n_units = n_seq * n_pairs

    def put(ref, val):
        for b in range(n_seq):
            for p in range(n_pairs):
                ref[b * n_pairs + p] = val[b * ch:b * ch + ref.shape[1], p * LANES:(p + 1) * LANES]

    put(rt_ref, r * jnp.exp(cum))
    put(at_ref, a_in * jnp.exp(cum - lw))
    put(bh_ref, b_in * e_neg)
    put(kh_ref, k * e_neg)
    put(bc_ref, b_in * e_end)
    put(kc_ref, k * e_end)
    put(v_ref, v)
    put(pc_ref, jnp.exp(cum_end))
    yield

    ti = lax.broadcasted_iota(jnp.int32, (ch, 2 * ch), 0)
    si = lax.broadcasted_iota(jnp.int32, (ch, 2 * ch), 1) % ch
    incl = ti >= si
    strict = ti > si
    lane = lax.broadcasted_iota(jnp.int32, (1, 2 * LANES), 1)
    head_a2 = (lane % LANES) < HEAD_DIM
    head_a = head_a2[:, :LANES]
    di = lax.broadcasted_iota(jnp.int32, (LANES, LANES), 0)
    dj = lax.broadcasted_iota(jnp.int32, (LANES, LANES), 1)
    same_head = (di < HEAD_DIM) == (dj < HEAD_DIM)
    diag = di == dj

    zero = jnp.zeros((), BF16)

    def stack(x, mask):
        x = x.astype(BF16)
        return jnp.concatenate([jnp.where(mask, x, zero), jnp.where(mask, zero, x)], axis=0)

    def block_diag(m):
        m = m.astype(BF16)
        return jnp.where(same_head, jnp.concatenate([m, m], axis=0), zero)

    pairs = range(n_units)
    zero_tile = jnp.zeros((2 * ch, LANES), BF16)
    a_rb, a_ab, a_rk, a_ak, vst = [], [], [], [], []
    for p in pairs:
        lhs = jnp.concatenate([rt_ref[p], at_ref[p]], axis=0)
        rhs = jnp.concatenate([stack(bh_ref[p], head_a), stack(kh_ref[p], head_a)], axis=0)
        a_bk = _mm_nt(lhs, rhs).astype(BF16)
        a_rb.append(jnp.where(incl, a_bk[:ch, :LANES], zero))
        a_ab.append(jnp.where(strict, a_bk[ch:, :LANES], zero))
        a_rk.append(jnp.where(incl, a_bk[:ch, LANES:], zero))
        a_ak.append(jnp.where(strict, a_bk[ch:, LANES:], zero))
        vst.append(stack(v_ref[p], head_a))
    yield

    xs = [jnp.concatenate([at_ref[p], _mm(a_ak[p], vst[p])], axis=1) for p in pairs]
    yield
    nk = a_ab
    for _ in range(int(math.log2(ch)) - 1):
        nk_next = [_mm(n, block_diag(n)) for n in nk]
        xs = [x + _mm(n, stack(x, head_a2)) for n, x in zip(nk, xs)]
        nk = nk_next
        yield
    xs = [x + _mm(n, stack(x, head_a2)) for n, x in zip(nk, xs)]
    yield

    ax = [_mm(jnp.concatenate([a_rb[p], a_rk[p]], axis=1),
              jnp.concatenate([stack(xs[p], head_a2), jnp.concatenate([zero_tile, vst[p]], axis=1)], axis=0))
          for p in pairs]
    mg = [_mm_tn(jnp.concatenate([bc_ref[p], kc_ref[p]], axis=0),
                 jnp.concatenate([xs[p], jnp.concatenate([jnp.zeros((ch, LANES), F32), v_ref[p]], axis=1)],
                                 axis=0))
          for p in pairs]
    yield
    for p in pairs:
        q_acc = rt_ref[p] + ax[p][:, :LANES]
        m_mat = jnp.where(same_head, mg[p][:, :LANES], 0.0) + jnp.where(diag, pc_ref[p], 0.0)
        g_mat = jnp.where(same_head, mg[p][:, LANES:], 0.0)
        res = _mm(jnp.concatenate([q_acc, m_mat], axis=0), state_ref[p])
        y_ref[p] = res[:ch] + ax[p][:, LANES:]
        state_ref[p] = res[ch:] + g_mat
    yield

    y = jnp.concatenate([jnp.concatenate([y_ref[b * n_pairs + p] for p in range(n_pairs)], axis=1)
                         for b in range(n_seq)], axis=0)
    inv_n = 1.0 / HEAD_DIM
    mu = head_sums(y) * inv_n
    d = y - mu
    var = head_sums(d * d) * inv_n
    yn = d * lax.rsqrt(var + GN_EPS) * lnw_ref[...] + lnb_ref[...]
    out_ref[:, lo:lo + ch, :] = ((yn + bonus) * gate).reshape(n_seq, ch, width).astype(out_ref.dtype)


N_ATTN_IN, N_ATTN_SCRATCH = 6, 3
N_RWKV_IN, N_RWKV_SCRATCH = 16, 12
RWKV_STAGES_PER_ATTN_STAGE = 1


def _mixers_kernel(*refs, n_heads, width, nb, n_pairs, chunks_per_step):
    attn_in, refs = refs[:N_ATTN_IN], refs[N_ATTN_IN:]
    rwkv_in, refs = refs[:N_RWKV_IN], refs[N_RWKV_IN:]
    attn_out, rw_out, refs = refs[0], refs[1], refs[2:]
    attn_scr, rwkv_scr = refs[:N_ATTN_SCRATCH], refs[N_ATTN_SCRATCH:]
    c = pl.program_id(0)
    state_ref, carry_rkv_ref, carry_lora_ref = rwkv_scr[:3]

    @pl.when(c == 0)
    def _():
        state_ref[...] = jnp.zeros_like(state_ref)
        carry_rkv_ref[...] = jnp.zeros_like(carry_rkv_ref)
        carry_lora_ref[...] = jnp.zeros_like(carry_lora_ref)

    attn = _attn_stages(*attn_in, attn_out, *attn_scr, n_heads=n_heads, first=c == 0,
                        n=(c // n_pairs) % nb, hp=c % n_pairs)
    next(attn)

    def rwkv_chunks():
        for i in range(chunks_per_step):
            yield from _rwkv_stages(*rwkv_in, rw_out, *rwkv_scr, width=width, lo=i * CHUNK)

    rwkv = rwkv_chunks()
    running = [True, True]
    while any(running):
        for _ in range(RWKV_STAGES_PER_ATTN_STAGE):
            if running[0] and next(rwkv, "done") == "done":
                running[0] = False
        if running[1] and next(attn, "done") == "done":
            running[1] = False


def _mixers(qkv, band, rkv, lora, prm, *, batch, seq, n_heads):
    t, w3 = rkv.shape
    width = w3 // 3
    n_lora = lora.shape[1]
    ch = CHUNK
    n_pairs = width // LANES
    nb = seq // ATTN_ROWS
    n_pat = len(DILATED_PATTERNS)
    n_steps = batch * nb * n_pairs
    assert (seq // ch) % n_steps == 0, "RWKV chunks must divide evenly over the attention steps"
    chunks_per_step = (seq // ch) // n_steps
    rows_per_step = chunks_per_step * ch

    a_blk = lambda c: (c // n_pairs) // nb * nb + (c // n_pairs) % nb
    a_prev = lambda c: (c // n_pairs) // nb * nb + jnp.maximum((c // n_pairs) % nb - 1, 0)
    aspec = lambda f: pl.BlockSpec((ATTN_ROWS, LANES), f)
    attn_specs = [
        aspec(lambda c: (a_blk(c), c % n_pairs)),
        aspec(lambda c: (a_prev(c), n_pairs + c % n_pairs)),
        aspec(lambda c: (a_blk(c), n_pairs + c % n_pairs)),
        aspec(lambda c: (a_prev(c), 2 * n_pairs + c % n_pairs)),
        aspec(lambda c: (a_blk(c), 2 * n_pairs + c % n_pairs)),
        pl.BlockSpec(band.shape, lambda c: (0, 0)),
    ]

    row = lambda n: pl.BlockSpec((1, n), lambda c: (0, 0))
    full = lambda a: pl.BlockSpec(a.shape, lambda c: (0,) * a.ndim)
    head_id = jnp.arange(SEG_LANES) // HEAD_DIM
    seg = (head_id[:, None] == head_id[None, :]).astype(BF16)
    pos = jnp.arange(batch * ch)
    tri = ((pos[:, None] >= pos[None, :]) & (pos[:, None] // ch == pos[None, :] // ch)).astype(BF16)
    n_units = batch * n_pairs
    big = lambda: pltpu.VMEM((n_units, ch, LANES), F32)
    rwkv_args = (rkv.reshape(batch, seq, w3), lora.reshape(batch, seq, n_lora), prm["mix_rkv"],
                 prm["mix_lora"], prm["w0"], prm["w_up"], prm["a0"], prm["a_up"], prm["g_up"], prm["k_k"],
                 prm["k_a"], prm["r_k"], prm["ln_w"], prm["ln_b"], seg, tri)
    rwkv_specs = [
        pl.BlockSpec((batch, rows_per_step, w3), lambda c: (0, c, 0)),
        pl.BlockSpec((batch, rows_per_step, n_lora), lambda c: (0, c, 0)),
        row(w3), row(n_lora), row(width), full(prm["w_up"]), row(width), full(prm["a_up"]),
        full(prm["g_up"]), row(width), row(width), row(width), row(width), row(width),
        full(seg), full(tri),
    ]
    assert len(attn_specs) == N_ATTN_IN and len(rwkv_specs) == N_RWKV_IN
    attn, rw = pl.pallas_call(
        functools.partial(_mixers_kernel, n_heads=n_heads, width=width, nb=nb, n_pairs=n_pairs,
                          chunks_per_step=chunks_per_step),
        grid=(n_steps,),
        in_specs=attn_specs + rwkv_specs,
        out_specs=[aspec(lambda c: (a_blk(c), c % n_pairs)),
                   pl.BlockSpec((batch, rows_per_step, width), lambda c: (0, c, 0))],
        out_shape=[jax.ShapeDtypeStruct((t, width), BF16),
                   jax.ShapeDtypeStruct((batch, seq, width), BF16)],
        scratch_shapes=[
            pltpu.VMEM((n_pat * n_heads, ATTN_BLK, 2 * ATTN_BLK), F32),
            pltpu.VMEM((n_pat, ATTN_ROWS, LANES), F32),
            pltpu.VMEM((n_pat, ATTN_ROWS, LANES), F32),
            pltpu.VMEM((n_units, LANES, LANES), F32),
            pltpu.VMEM((batch, 1, w3), F32),
            pltpu.VMEM((batch, 1, n_lora), F32),
            big(), big(), big(), big(), big(), big(), big(),
            pltpu.VMEM((n_units, 1, LANES), F32),
            big(),
        ],
        compiler_params=_params(("arbitrary",)),
        name="mixers",
    )(qkv, qkv, qkv, qkv, qkv, band, *rwkv_args)
    return attn, rw.reshape(t, width)


def _out_proj_kernel(attn_ref, rw_ref, wa_ref, wr_ref, x_ref, out_ref):
    out_ref[...] = (x_ref[...] + jnp.dot(attn_ref[...], wa_ref[...], preferred_element_type=F32)
                    + jnp.dot(rw_ref[...], wr_ref[...], preferred_element_type=F32))


def _out_proj(attn, rw, w_out, x2, *, tm, tn):
    t, width = attn.shape
    d = w_out.shape[1]
    return pl.pallas_call(
        _out_proj_kernel,
        grid=(t // tm, d // tn),
        in_specs=[
            pl.BlockSpec((tm, width), lambda i, j: (i, 0)),
            pl.BlockSpec((tm, width), lambda i, j: (i, 0)),
            pl.BlockSpec((width, tn), lambda i, j: (0, j)),
            pl.BlockSpec((width, tn), lambda i, j: (1, j)),
            pl.BlockSpec((tm, tn), lambda i, j: (i, j)),
        ],
        out_specs=pl.BlockSpec((tm, tn), lambda i, j: (i, j)),
        out_shape=jax.ShapeDtypeStruct((t, d), F32),
        compiler_params=_params(("arbitrary", "arbitrary")),
        name="out_proj",
    )(attn, rw, w_out, w_out, x2)


FFN_DOWN_COLS = 512


def _ffn_kernel(h_ref, g_ref, wg_ref, wu_ref, wd_ref, og_ref, out_ref, hn_ref):
    j = pl.program_id(1)

    @pl.when(j == 0)
    def _():
        h = h_ref[...]
        ms = jnp.mean(h * h, axis=-1, keepdims=True)
        hn_ref[...] = (h * lax.rsqrt(ms + RMS_EPS) * g_ref[...]).astype(BF16)
        out_ref[...] = h

    hn = hn_ref[...]
    gate = jnp.dot(hn, wg_ref[...], preferred_element_type=F32)
    up = jnp.dot(hn, wu_ref[...], preferred_element_type=F32)
    act = (gate * jax.nn.sigmoid(gate) * up).astype(BF16)
    for c0 in range(0, out_ref.shape[1], FFN_DOWN_COLS):
        cols = slice(c0, c0 + FFN_DOWN_COLS)
        out_ref[:, cols] += jnp.dot(act, wd_ref[:, cols], preferred_element_type=F32)

    @pl.when(j == pl.num_programs(1) - 1)
    def _():
        h = out_ref[...]
        ms = jnp.mean(h * h, axis=-1, keepdims=True)
        out_ref[...] = h * lax.rsqrt(ms + RMS_EPS) * og_ref[...]


def _ffn(h, g, w_gate_up, w_down, out_g, *, tm, th):
    t, d = h.shape
    hidden = w_down.shape[0]
    n_h = hidden // th
    return pl.pallas_call(
        _ffn_kernel,
        grid=(t // tm, n_h),
        in_specs=[
            pl.BlockSpec((tm, d), lambda i, j: (i, 0)),
            pl.BlockSpec((1, d), lambda i, j: (0, 0)),
            pl.BlockSpec((d, th), lambda i, j: (0, j)),
            pl.BlockSpec((d, th), lambda i, j: (0, n_h + j)),
            pl.BlockSpec((th, d), lambda i, j: (j, 0)),
            pl.BlockSpec((1, d), lambda i, j: (0, 0)),
        ],
        out_specs=pl.BlockSpec((tm, d), lambda i, j: (i, 0)),
        out_shape=jax.ShapeDtypeStruct((t, d), F32),
        scratch_shapes=[pltpu.VMEM((tm, d), BF16)],
        compiler_params=pltpu.CompilerParams(dimension_semantics=("arbitrary", "arbitrary"),
                                             vmem_limit_bytes=FFN_VMEM_LIMIT),
        name="ffn",
    )(h, g, w_gate_up, w_gate_up, w_down, out_g)


def _pad_rows(a, rows):
    return jnp.pad(a, ((0, rows - a.shape[0]), (0, 0)))


def _layer(h, norm1_g, w_in, shift_mix, w0, w_up, a0, a_up, g_up, k_k, k_a, r_k, ln_w, ln_b,
           rel_bias_table, w_out, norm2_g, w_gate_up, w_down, out_g, *, batch, seq, tile_m):
    d = h.shape[1]
    width = w_out.shape[0] // 2
    n_heads = width // HEAD_DIM
    n_main = 6 * width
    rank_w, rank_a, rank_g = w_up.shape[0], a_up.shape[0], g_up.shape[0]
    n_lora = rank_w + rank_a + rank_g
    n_lora_pad = -(-n_lora // LANES) * LANES

    qkv, rkv, lora = _in_proj(h, norm1_g.reshape(1, d), w_in.astype(BF16), n_main=n_main,
                              n_lora_pad=n_lora_pad, tm=tile_m, tn=width)

    mix_rkv = shift_mix[:3 * width].reshape(1, 3 * width)
    mix_lora = jnp.pad(shift_mix[3 * width:], (0, n_lora_pad - n_lora)).reshape(1, n_lora_pad)
    zeros = lambda n: jnp.zeros((n, width), F32)
    prm = dict(
        mix_rkv=mix_rkv, mix_lora=mix_lora,
        w0=w0.reshape(1, width), a0=a0.reshape(1, width),
        w_up=_pad_rows(w_up, n_lora_pad).astype(BF16),
        a_up=_pad_rows(jnp.concatenate([zeros(rank_w), a_up]), n_lora_pad).astype(BF16),
        g_up=_pad_rows(jnp.concatenate([zeros(rank_w + rank_a), g_up]), n_lora_pad).astype(BF16),
        k_k=k_k.reshape(1, width), k_a=k_a.reshape(1, width), r_k=r_k.reshape(1, width),
        ln_w=ln_w.reshape(1, width), ln_b=ln_b.reshape(1, width),
    )
    attn, rw = _mixers(qkv, _band_rows(rel_bias_table), rkv, lora, prm, batch=batch, seq=seq, n_heads=n_heads)

    h1 = _out_proj(attn, rw, w_out.astype(BF16), h, tm=tile_m // 2, tn=d)
    return _ffn(h1, norm2_g.reshape(1, d), w_gate_up.astype(BF16), w_down.astype(BF16), out_g.reshape(1, d),
                tm=tile_m, th=512)


def kernel(x, norm1_g, w_in, rwkv_shift_mix, rwkv_w0, rwkv_w_up, rwkv_a0, rwkv_a_up, rwkv_g_up, rwkv_k_k, rwkv_k_a, rwkv_r_k, rwkv_ln_w, rwkv_ln_b, rel_bias_table, w_out, norm2_g, w_gate_up, w_down, final_g):
    batch, seq, d = x.shape
    depth = w_in.shape[0]
    assert depth == 1, "the final RMSNorm is fused into the last layer's FFN kernel"
    assert seq % DILATED_PATTERNS[-1][0] == 0 and seq % CHUNK == 0
    h = x.reshape(batch * seq, d)
    out = _layer(h, norm1_g[0], w_in[0], rwkv_shift_mix[0], rwkv_w0[0], rwkv_w_up[0], rwkv_a0[0],
                 rwkv_a_up[0], rwkv_g_up[0], rwkv_k_k[0], rwkv_k_a[0], rwkv_r_k[0], rwkv_ln_w[0],
                 rwkv_ln_b[0], rel_bias_table, w_out[0], norm2_g[0], w_gate_up[0], w_down[0], final_g,
                 batch=batch, seq=seq, tile_m=min(1024, batch * seq))
    return out.reshape(batch, seq, d)
```

```python
import functools
import math

import jax
import jax.numpy as jnp
from jax import lax
from jax.experimental import pallas as pl
from jax.experimental.pallas import tpu as pltpu

F32 = jnp.float32
BF16 = jnp.bfloat16

HEAD_DIM = 64
LANES = 128
DILATED_PATTERNS = ((128, 1), (512, 4), (2048, 16))
REL_BUCKETS = 32
REL_MAX_DIST = 2048
RMS_EPS = 1e-6
GN_EPS = 64e-5
DECAY_SCALE = math.exp(-0.5)
ATTN_SCALE = HEAD_DIM ** -0.5
MASK_VALUE = -1e30
LOG2_E = math.log2(math.e)
CHUNK = 64
SEG_LANES = 256
VMEM_LIMIT = 56 * 1024 * 1024
FFN_VMEM_LIMIT = 60 * 1024 * 1024


def _mm(a, b):
    return jnp.dot(a.astype(BF16), b.astype(BF16), preferred_element_type=F32)


def _mm_nt(a, b):
    return lax.dot_general(a.astype(BF16), b.astype(BF16), (((1,), (1,)), ((), ())),
                           preferred_element_type=F32)


def _mm_tn(a, b):
    return lax.dot_general(a.astype(BF16), b.astype(BF16), (((0,), (0,)), ((), ())),
                           preferred_element_type=F32)


def _mm_split(a, b_exact):
    hi = a.astype(BF16)
    lo = (a - hi.astype(F32)).astype(BF16)
    return (jnp.dot(hi, b_exact, preferred_element_type=F32)
            + jnp.dot(lo, b_exact, preferred_element_type=F32))


def _params(semantics):
    return pltpu.CompilerParams(dimension_semantics=semantics, vmem_limit_bytes=VMEM_LIMIT)


def _in_proj_kernel(x_hbm, g_ref, w_ref, wl_ref, qkv_ref, rkv_ref, lora_ref, xn_ref, xbuf_ref, sem_ref,
                    *, n_qkv, n_lora):
    i, j = pl.program_id(0), pl.program_id(1)
    tm = xbuf_ref.shape[1]
    slot = i % 2

    def x_copy(tile, into):
        rows = pl.ds(pl.multiple_of(tile * tm, tm), tm)
        return pltpu.make_async_copy(x_hbm.at[rows, :], xbuf_ref.at[into], sem_ref.at[into])

    @pl.when((i == 0) & (j == 0))
    def _():
        x_copy(0, 0).start()

    @pl.when((j == 1) & (i + 1 < pl.num_programs(0)))
    def _():
        x_copy(i + 1, 1 - slot).start()

    @pl.when(j == 0)
    def _():
        x_copy(i, slot).wait()
        x = xbuf_ref[slot]
        ms = jnp.mean(x * x, axis=-1, keepdims=True)
        xn_ref[...] = (x * lax.rsqrt(ms + RMS_EPS) * g_ref[...]).astype(BF16)
        col = lax.broadcasted_iota(jnp.int32, (1, wl_ref.shape[1]), 1)
        wl = jnp.where(col < n_lora, wl_ref[...], jnp.zeros((), BF16))
        lora_ref[...] = jnp.dot(xn_ref[...], wl, preferred_element_type=F32)

    @pl.when((j >= 1) & (j <= n_qkv))
    def _():
        qkv_ref[...] = jnp.dot(xn_ref[...], w_ref[...], preferred_element_type=F32)

    @pl.when(j > n_qkv)
    def _():
        rkv_ref[...] = jnp.dot(xn_ref[...], w_ref[...], preferred_element_type=F32)


def _in_proj(x2, g, w, *, n_main, n_lora_pad, tm, tn):
    t, d = x2.shape
    n_lora = w.shape[1] - n_main
    n_qkv = (n_main // 2) // tn
    last = 2 * n_qkv - 1
    assert n_main % n_lora_pad == 0 and n_lora <= n_lora_pad
    return pl.pallas_call(
        functools.partial(_in_proj_kernel, n_qkv=n_qkv, n_lora=n_lora),
        grid=(t // tm, 2 * n_qkv + 1),
        in_specs=[
            pl.BlockSpec(memory_space=pl.ANY),
            pl.BlockSpec((1, d), lambda i, j: (0, 0)),
            pl.BlockSpec((d, tn), lambda i, j: (0, jnp.clip(j - 1, 0, last))),
            pl.BlockSpec((d, n_lora_pad), lambda i, j: (0, n_main // n_lora_pad)),
        ],
        out_specs=[
            pl.BlockSpec((tm, tn), lambda i, j: (i, jnp.clip(j - 1, 0, n_qkv - 1))),
            pl.BlockSpec((tm, tn), lambda i, j: (i, jnp.clip(j - 1 - n_qkv, 0, n_qkv - 1))),
            pl.BlockSpec((tm, n_lora_pad), lambda i, j: (i, 0)),
        ],
        out_shape=[
            jax.ShapeDtypeStruct((t, n_main // 2), F32),
            jax.ShapeDtypeStruct((t, n_main // 2), F32),
            jax.ShapeDtypeStruct((t, n_lora_pad), F32),
        ],
        scratch_shapes=[pltpu.VMEM((tm, d), BF16), pltpu.VMEM((2, tm, d), F32), pltpu.SemaphoreType.DMA((2,))],
        compiler_params=_params(("arbitrary", "arbitrary")),
        name="in_proj",
    )(x2, g, w, w)


ATTN_BLK = 128
ATTN_ROWS = max(w for w, _ in DILATED_PATTERNS)
ATTN_GROUP = 2


def _attn_stages(q_ref, kp_ref, kc_ref, vp_ref, vc_ref, band_ref, out_ref, bias_ref, o_ref, lse_ref,
                 *, n_heads, first, n, hp):
    blk = ATTN_BLK

    @pl.when(first)
    def _():
        for i in range(bias_ref.shape[0]):
            row = jnp.broadcast_to(band_ref[i:i + 1, :], (blk, 2 * blk))
            bias_ref[i] = pltpu.roll(row, 0, 1, stride=1, stride_axis=0)

    yield

    lane = lax.broadcasted_iota(jnp.int32, (1, LANES), 1)
    in_head = [(lane >= half * HEAD_DIM) & (lane < (half + 1) * HEAD_DIM) for half in range(2)]
    key_idx = lax.broadcasted_iota(jnp.int32, (1, 2 * blk), 1)
    first_ok = (key_idx >= blk) | (n > 0)
    ones_tile = jnp.ones((2 * blk, LANES), BF16)
    zero = jnp.zeros((), BF16)

    units = [(bi, window, dilation, sub, r)
             for bi, (window, dilation) in enumerate(DILATED_PATTERNS)
             for sub in range(ATTN_ROWS // window) for r in range(dilation)]
    groups = [units[g0:g0 + ATTN_GROUP] for g0 in range(0, len(units), ATTN_GROUP)]

    def scores(group):
        work = []
        for bi, window, dilation, sub, r in group:
            cur = pl.ds(sub * window + r, blk, stride=dilation)
            if sub == 0:
                prev_k, prev_v = kp_ref, vp_ref
                prev = pl.ds(ATTN_ROWS - window + r, blk, stride=dilation)
            else:
                prev_k, prev_v = kc_ref, vc_ref
                prev = pl.ds((sub - 1) * window + r, blk, stride=dilation)
            q = (q_ref[cur, :] * (ATTN_SCALE * LOG2_E)).astype(BF16)
            k = jnp.concatenate([prev_k[prev, :], kc_ref[cur, :]], axis=0).astype(BF16)
            v = jnp.concatenate([prev_v[prev, :], vc_ref[cur, :]], axis=0).astype(BF16)
            q_both = jnp.concatenate([jnp.where(m, q, zero) for m in in_head], axis=0)
            work.append((bi, cur, sub == 0, _mm_nt(q_both, k), jnp.concatenate([v, ones_tile], axis=1)))
        return work

    def softmax(work):
        done = []
        for bi, cur, at_start, s, v in work:
            bias = bias_ref[pl.ds(bi * n_heads + 2 * hp, 2)]
            su = s + bias.reshape(2 * blk, 2 * blk)
            if at_start:
                su = jnp.where(first_ok, su, MASK_VALUE)
            mx = jnp.max(su, axis=-1, keepdims=True)
            done.append((bi, cur, jnp.exp2(su - mx).astype(BF16), mx, v))
        return done

    def values(done):
        for bi, cur, p, mx, v in done:
            res = jnp.dot(p, v, preferred_element_type=F32)
            ra, rb = res[:blk], res[blk:]
            la, lb = ra[:, LANES:], rb[:, LANES:]
            o_ref[bi, cur, :] = jnp.where(in_head[0], ra[:, :LANES] / la, rb[:, :LANES] / lb)
            lse_ref[bi, cur, :] = jnp.where(in_head[0], mx[:blk] + jnp.log2(la), mx[blk:] + jnp.log2(lb))

    pending = None
    for group in groups:
        work = scores(group)
        if pending is not None:
            values(pending)
        pending = softmax(work)
        yield
    values(pending)
    yield

    lses = [lse_ref[bi] for bi in range(len(DILATED_PATTERNS))]
    top = functools.reduce(jnp.maximum, lses)
    es = [jnp.exp2(l - top) for l in lses]
    num = functools.reduce(jnp.add, [e * o_ref[bi] for bi, e in enumerate(es)])
    out_ref[...] = (num / functools.reduce(jnp.add, es)).astype(out_ref.dtype)


def _t5_bucket(dist):
    exact = REL_BUCKETS // 2
    d_f = jnp.maximum(dist, 1).astype(F32)
    large = exact + (jnp.log(d_f / exact) / math.log(REL_MAX_DIST / exact)
                     * (REL_BUCKETS - exact)).astype(jnp.int32)
    large = jnp.minimum(large, REL_BUCKETS - 1)
    return jnp.where(dist < exact, dist, large)


def _band_rows(bias_table):
    blk = ATTN_BLK
    rel = blk - jnp.arange(2 * blk)
    band = (rel >= 0) & (rel <= blk)
    rows = []
    for _, dilation in DILATED_PATTERNS:
        bias = bias_table[_t5_bucket(jnp.clip(rel, 0, blk) * dilation)]
        rows.append(jnp.where(band[:, None], bias.astype(F32) * LOG2_E, MASK_VALUE).T)
    return jnp.concatenate(rows, axis=0)


def _rwkv_stages(rkv_ref, lora_ref, mix_rkv_ref, mix_lora_ref, w0_ref, w_up_ref, a0_ref, a_up_ref,
                 g_up_ref, kk_ref, ka_ref, rk_ref, lnw_ref, lnb_ref, seg_ref, tri_ref,
                 out_ref,
                 state_ref, carry_rkv_ref, carry_lora_ref,
                 rt_ref, at_ref, bh_ref, kh_ref, bc_ref, kc_ref, v_ref, pc_ref, y_ref, *, width, lo):
    n_pairs = width // LANES
    n_seq, ch = rkv_ref.shape[0], CHUNK
    rows = n_seq * ch

    row = lax.broadcasted_iota(jnp.int32, (rows, 1), 0) % ch

    def per_seq(row_of):
        return jnp.concatenate([jnp.broadcast_to(row_of(b), (ch, row_of(b).shape[-1]))
                                for b in range(n_seq)], axis=0)

    def token_shift(z_ref, carry_ref, mix):
        z = z_ref[:, lo:lo + ch, :].reshape(rows, z_ref.shape[-1])
        prev = jnp.where(row == 0, per_seq(lambda b: carry_ref[b]), pltpu.roll(z, 1, axis=0))
        for b in range(n_seq):
            carry_ref[b] = z[(b + 1) * ch - 1:(b + 1) * ch, :]
        return z + (prev - z) * mix

    z = token_shift(rkv_ref, carry_rkv_ref, mix_rkv_ref[...])
    zl = token_shift(lora_ref, carry_lora_ref, mix_lora_ref[...])
    r, k, v = z[:, :width], z[:, width:2 * width], z[:, 2 * width:]

    lw = -DECAY_SCALE * jax.nn.sigmoid(w0_ref[...] + _mm(jnp.tanh(zl), w_up_ref[...]))
    a_sig = jax.nn.sigmoid(a0_ref[...] + _mm(zl, a_up_ref[...]))
    gate = _mm(jax.nn.sigmoid(zl), g_up_ref[...])

    seg = seg_ref[...]

    def head_sums(x):
        tiles = [_mm_split(x[:, t:t + SEG_LANES], seg) for t in range(0, width, SEG_LANES)]
        return jnp.concatenate(tiles, axis=1)

    kk = k * kk_ref[...]
    kk = kk / jnp.maximum(jnp.sqrt(head_sums(kk * kk)), 1e-12)
    k = k * (1.0 + (a_sig - 1.0) * ka_ref[...])
    bonus = head_sums(r * k * rk_ref[...]) * v
    a_in = -kk
    b_in = kk * a_sig

    lw_hi = lw.astype(BF16)
    lw_lo = (lw - lw_hi.astype(F32)).astype(BF16)
    tri = tri_ref[...]
    cum = (jnp.dot(tri, lw_hi, preferred_element_type=F32)
           + jnp.dot(tri, lw_lo, preferred_element_type=F32))
    cum_end = per_seq(lambda b: cum[(b + 1) * ch - 1:(b + 1) * ch, :])
    e_neg = jnp.exp(-cum)
    e_end = jnp.exp(cum_end - cum)

    n_units = n_seq * n_pairs

    def put(ref, val):
        for b in range(n_seq):
            for p in range(n_pairs):
                ref[b * n_pairs + p] = val[b * ch:b * ch + ref.shape[1], p * LANES:(p + 1) * LANES]

    put(rt_ref, r * jnp.exp(cum))
    put(at_ref, a_in * jnp.exp(cum - lw))
    put(bh_ref, b_in * e_neg)
    put(kh_ref, k * e_neg)
    put(bc_ref, b_in * e_end)
    put(kc_ref, k * e_end)
    put(v_ref, v)
    put(pc_ref, jnp.exp(cum_end))
    yield

    ti = lax.broadcasted_iota(jnp.int32, (ch, 2 * ch), 0)
    si = lax.broadcasted_iota(jnp.int32, (ch, 2 * ch), 1) % ch
    incl = ti >= si
    strict = ti > si
    lane = lax.broadcasted_iota(jnp.int32, (1, 2 * LANES), 1)
    head_a2 = (lane % LANES) < HEAD_DIM
    head_a = head_a2[:, :LANES]
    di = lax.broadcasted_iota(jnp.int32, (LANES, LANES), 0)
    dj = lax.broadcasted_iota(jnp.int32, (LANES, LANES), 1)
    same_head = (di < HEAD_DIM) == (dj < HEAD_DIM)
    diag = di == dj

    zero = jnp.zeros((), BF16)

    def stack(x, mask):
        x = x.astype(BF16)
        return jnp.concatenate([jnp.where(mask, x, zero), jnp.where(mask, zero, x)], axis=0)

    def block_diag(m):
        m = m.astype(BF16)
        return jnp.where(same_head, jnp.concatenate([m, m], axis=0), zero)

    pairs = range(n_units)
    zero_tile = jnp.zeros((2 * ch, LANES), BF16)
    a_rb, a_ab, a_rk, a_ak, vst = [], [], [], [], []
    for p in pairs:
        lhs = jnp.concatenate([rt_ref[p], at_ref[p]], axis=0)
        rhs = jnp.concatenate([stack(bh_ref[p], head_a), stack(kh_ref[p], head_a)], axis=0)
        a_bk = _mm_nt(lhs, rhs).astype(BF16)
        a_rb.append(jnp.where(incl, a_bk[:ch, :LANES], zero))
        a_ab.append(jnp.where(strict, a_bk[ch:, :LANES], zero))
        a_rk.append(jnp.where(incl, a_bk[:ch, LANES:], zero))
        a_ak.append(jnp.where(strict, a_bk[ch:, LANES:], zero))
        vst.append(stack(v_ref[p], head_a))
    yield

    xs = [jnp.concatenate([at_ref[p], _mm(a_ak[p], vst[p])], axis=1) for p in pairs]
    yield
    nk = a_ab
    for _ in range(int(math.log2(ch)) - 1):
        nk_next = [_mm(n, block_diag(n)) for n in nk]
        xs = [x + _mm(n, stack(x, head_a2)) for n, x in zip(nk, xs)]
        nk = nk_next
        yield
    xs = [x + _mm(n, stack(x, head_a2)) for n, x in zip(nk, xs)]
    yield

    ax = [_mm(jnp.concatenate([a_rb[p], a_rk[p]], axis=1),
              jnp.concatenate([stack(xs[p], head_a2), jnp.concatenate([zero_tile, vst[p]], axis=1)], axis=0))
          for p in pairs]
    mg = [_mm_tn(jnp.concatenate([bc_ref[p], kc_ref[p]], axis=0),
                 jnp.concatenate([xs[p], jnp.concatenate([jnp.zeros((ch, LANES), F32), v_ref[p]], axis=1)],
                                 axis=0))
          for p in pairs]
    yield
    for p in pairs:
        q_acc = rt_ref[p] + ax[p][:, :LANES]
        m_mat = jnp.where(same_head, mg[p][:, :LANES], 0.0) + jnp.where(diag, pc_ref[p], 0.0)
        g_mat = jnp.where(same_head, mg[p][:, LANES:], 0.0)
        res = _mm(jnp.concatenate([q_acc, m_mat], axis=0), state_ref[p])
        y_ref[p] = res[:ch] + ax[p][:, LANES:]
        state_ref[p] = res[ch:] + g_mat
    yield

    y = jnp.concatenate([jnp.concatenate([y_ref[b * n_pairs + p] for p in range(n_pairs)], axis=1)
                         for b in range(n_seq)], axis=0)
    inv_n = 1.0 / HEAD_DIM
    mu = head_sums(y) * inv_n
    d = y - mu
    var = head_sums(d * d) * inv_n
    yn = d * lax.rsqrt(var + GN_EPS) * lnw_ref[...] + lnb_ref[...]
    out_ref[:, lo:lo + ch, :] = ((yn + bonus) * gate).reshape(n_seq, ch, width).astype(out_ref.dtype)


N_ATTN_IN, N_ATTN_SCRATCH = 6, 3
N_RWKV_IN, N_RWKV_SCRATCH = 16, 12
RWKV_STAGES_PER_ATTN_STAGE = 1


def _mixers_kernel(*refs, n_heads, width, nb, n_pairs, chunks_per_step):
    attn_in, refs = refs[:N_ATTN_IN], refs[N_ATTN_IN:]
    rwkv_in, refs = refs[:N_RWKV_IN], refs[N_RWKV_IN:]
    attn_out, rw_out, refs = refs[0], refs[1], refs[2:]
    attn_scr, rwkv_scr = refs[:N_ATTN_SCRATCH], refs[N_ATTN_SCRATCH:]
    c = pl.program_id(0)
    state_ref, carry_rkv_ref, carry_lora_ref = rwkv_scr[:3]

    @pl.when(c == 0)
    def _():
        state_ref[...] = jnp.zeros_like(state_ref)
        carry_rkv_ref[...] = jnp.zeros_like(carry_rkv_ref)
        carry_lora_ref[...] = jnp.zeros_like(carry_lora_ref)

    attn = _attn_stages(*attn_in, attn_out, *attn_scr, n_heads=n_heads, first=c == 0,
                        n=(c // n_pairs) % nb, hp=c % n_pairs)
    next(attn)

    def rwkv_chunks():
        for i in range(chunks_per_step):
            yield from _rwkv_stages(*rwkv_in, rw_out, *rwkv_scr, width=width, lo=i * CHUNK)

    rwkv = rwkv_chunks()
    running = [True, True]
    while any(running):
        for _ in range(RWKV_STAGES_PER_ATTN_STAGE):
            if running[0] and next(rwkv, "done") == "done":
                running[0] = False
        if running[1] and next(attn, "done") == "done":
            running[1] = False


def _mixers(qkv, band, rkv, lora, prm, *, batch, seq, n_heads):
    t, w3 = rkv.shape
    width = w3 // 3
    n_lora = lora.shape[1]
    ch = CHUNK
    n_pairs = width // LANES
    nb = seq // ATTN_ROWS
    n_pat = len(DILATED_PATTERNS)
    n_steps = batch * nb * n_pairs
    assert (seq // ch) % n_steps == 0, "RWKV chunks must divide evenly over the attention steps"
    chunks_per_step = (seq // ch) // n_steps
    rows_per_step = chunks_per_step * ch

    a_blk = lambda c: (c // n_pairs) // nb * nb + (c // n_pairs) % nb
    a_prev = lambda c: (c // n_pairs) // nb * nb + jnp.maximum((c // n_pairs) % nb - 1, 0)
    aspec = lambda f: pl.BlockSpec((ATTN_ROWS, LANES), f)
    attn_specs = [
        aspec(lambda c: (a_blk(c), c % n_pairs)),
        aspec(lambda c: (a_prev(c), n_pairs + c % n_pairs)),
        aspec(lambda c: (a_blk(c), n_pairs + c % n_pairs)),
        aspec(lambda c: (a_prev(c), 2 * n_pairs + c % n_pairs)),
        aspec(lambda c: (a_blk(c), 2 * n_pairs + c % n_pairs)),
        pl.BlockSpec(band.shape, lambda c: (0, 0)),
    ]

    row = lambda n: pl.BlockSpec((1, n), lambda c: (0, 0))
    full = lambda a: pl.BlockSpec(a.shape, lambda c: (0,) * a.ndim)
    head_id = jnp.arange(SEG_LANES) // HEAD_DIM
    seg = (head_id[:, None] == head_id[None, :]).astype(BF16)
    pos = jnp.arange(batch * ch)
    tri = ((pos[:, None] >= pos[None, :]) & (pos[:, None] // ch == pos[None, :] // ch)).astype(BF16)
    n_units = batch * n_pairs
    big = lambda: pltpu.VMEM((n_units, ch, LANES), F32)
    rwkv_args = (rkv.reshape(batch, seq, w3), lora.reshape(batch, seq, n_lora), prm["mix_rkv"],
                 prm["mix_lora"], prm["w0"], prm["w_up"], prm["a0"], prm["a_up"], prm["g_up"], prm["k_k"],
                 prm["k_a"], prm["r_k"], prm["ln_w"], prm["ln_b"], seg, tri)
    rwkv_specs = [
        pl.BlockSpec((batch, rows_per_step, w3), lambda c: (0, c, 0)),
        pl.BlockSpec((batch, rows_per_step, n_lora), lambda c: (0, c, 0)),
        row(w3), row(n_lora), row(width), full(prm["w_up"]), row(width), full(prm["a_up"]),
        full(prm["g_up"]), row(width), row(width), row(width), row(width), row(width),
        full(seg), full(tri),
    ]
    assert len(attn_specs) == N_ATTN_IN and len(rwkv_specs) == N_RWKV_IN
    attn, rw = pl.pallas_call(
        functools.partial(_mixers_kernel, n_heads=n_heads, width=width, nb=nb, n_pairs=n_pairs,
                          chunks_per_step=chunks_per_step),
        grid=(n_steps,),
        in_specs=attn_specs + rwkv_specs,
        out_specs=[aspec(lambda c: (a_blk(c), c % n_pairs)),
                   pl.BlockSpec((batch, rows_per_step, width), lambda c: (0, c, 0))],
        out_shape=[jax.ShapeDtypeStruct((t, width), BF16),
                   jax.ShapeDtypeStruct((batch, seq, width), BF16)],
        scratch_shapes=[
            pltpu.VMEM((n_pat * n_heads, ATTN_BLK, 2 * ATTN_BLK), F32),
            pltpu.VMEM((n_pat, ATTN_ROWS, LANES), F32),
            pltpu.VMEM((n_pat, ATTN_ROWS, LANES), F32),
            pltpu.VMEM((n_units, LANES, LANES), F32),
            pltpu.VMEM((batch, 1, w3), F32),
            pltpu.VMEM((batch, 1, n_lora), F32),
            big(), big(), big(), big(), big(), big(), big(),
            pltpu.VMEM((n_units, 1, LANES), F32),
            big(),
        ],
        compiler_params=_params(("arbitrary",)),
        name="mixers",
    )(qkv, qkv, qkv, qkv, qkv, band, *rwkv_args)
    return attn, rw.reshape(t, width)


def _out_proj_kernel(attn_ref, rw_ref, wa_ref, wr_ref, x_ref, out_ref):
    out_ref[...] = (x_ref[...] + jnp.dot(attn_ref[...], wa_ref[...], preferred_element_type=F32)
                    + jnp.dot(rw_ref[...], wr_ref[...], preferred_element_type=F32))


def _out_proj(attn, rw, w_out, x2, *, tm, tn):
    t, width = attn.shape
    d = w_out.shape[1]
    return pl.pallas_call(
        _out_proj_kernel,
        grid=(t // tm, d // tn),
        in_specs=[
            pl.BlockSpec((tm, width), lambda i, j: (i, 0)),
            pl.BlockSpec((tm, width), lambda i, j: (i, 0)),
            pl.BlockSpec((width, tn), lambda i, j: (0, j)),
            pl.BlockSpec((width, tn), lambda i, j: (1, j)),
            pl.BlockSpec((tm, tn), lambda i, j: (i, j)),
        ],
        out_specs=pl.BlockSpec((tm, tn), lambda i, j: (i, j)),
        out_shape=jax.ShapeDtypeStruct((t, d), F32),
        compiler_params=_params(("arbitrary", "arbitrary")),
        name="out_proj",
    )(attn, rw, w_out, w_out, x2)


FFN_DOWN_COLS = 512


def _ffn_kernel(h_ref, g_ref, wg_ref, wu_ref, wd_ref, og_ref, out_ref, hn_ref):
    j = pl.program_id(1)

    @pl.when(j == 0)
    def _():
        h = h_ref[...]
        ms = jnp.mean(h * h, axis=-1, keepdims=True)
        hn_ref[...] = (h * lax.rsqrt(ms + RMS_EPS) * g_ref[...]).astype(BF16)
        out_ref[...] = h

    hn = hn_ref[...]
    gate = jnp.dot(hn, wg_ref[...], preferred_element_type=F32)
    up = jnp.dot(hn, wu_ref[...], preferred_element_type=F32)
    act = (gate * jax.nn.sigmoid(gate) * up).astype(BF16)
    for c0 in range(0, out_ref.shape[1], FFN_DOWN_COLS):
        cols = slice(c0, c0 + FFN_DOWN_COLS)
        out_ref[:, cols] += jnp.dot(act, wd_ref[:, cols], preferred_element_type=F32)

    @pl.when(j == pl.num_programs(1) - 1)
    def _():
        h = out_ref[...]
        ms = jnp.mean(h * h, axis=-1, keepdims=True)
        out_ref[...] = h * lax.rsqrt(ms + RMS_EPS) * og_ref[...]


def _ffn(h, g, w_gate_up, w_down, out_g, *, tm, th):
    t, d = h.shape
    hidden = w_down.shape[0]
    n_h = hidden // th
    return pl.pallas_call(
        _ffn_kernel,
        grid=(t // tm, n_h),
        in_specs=[
            pl.BlockSpec((tm, d), lambda i, j: (i, 0)),
            pl.BlockSpec((1, d), lambda i, j: (0, 0)),
            pl.BlockSpec((d, th), lambda i, j: (0, j)),
            pl.BlockSpec((d, th), lambda i, j: (0, n_h + j)),
            pl.BlockSpec((th, d), lambda i, j: (j, 0)),
            pl.BlockSpec((1, d), lambda i, j: (0, 0)),
        ],
        out_specs=pl.BlockSpec((tm, d), lambda i, j: (i, 0)),
        out_shape=jax.ShapeDtypeStruct((t, d), F32),
        scratch_shapes=[pltpu.VMEM((tm, d), BF16)],
        compiler_params=pltpu.CompilerParams(dimension_semantics=("arbitrary", "arbitrary"),
                                             vmem_limit_bytes=FFN_VMEM_LIMIT),
        name="ffn",
    )(h, g, w_gate_up, w_gate_up, w_down, out_g)


def _pad_rows(a, rows):
    return jnp.pad(a, ((0, rows - a.shape[0]), (0, 0)))


def _layer(h, norm1_g, w_in, shift_mix, w0, w_up, a0, a_up, g_up, k_k, k_a, r_k, ln_w, ln_b,
           rel_bias_table, w_out, norm2_g, w_gate_up, w_down, out_g, *, batch, seq, tile_m):
    d = h.shape[1]
    width = w_out.shape[0] // 2
    n_heads = width // HEAD_DIM
    n_main = 6 * width
    rank_w, rank_a, rank_g = w_up.shape[0], a_up.shape[0], g_up.shape[0]
    n_lora = rank_w + rank_a + rank_g
    n_lora_pad = -(-n_lora // LANES) * LANES

    qkv, rkv, lora = _in_proj(h, norm1_g.reshape(1, d), w_in.astype(BF16), n_main=n_main,
                              n_lora_pad=n_lora_pad, tm=tile_m, tn=width)

    mix_rkv = shift_mix[:3 * width].reshape(1, 3 * width)
    mix_lora = jnp.pad(shift_mix[3 * width:], (0, n_lora_pad - n_lora)).reshape(1, n_lora_pad)
    zeros = lambda n: jnp.zeros((n, width), F32)
    prm = dict(
        mix_rkv=mix_rkv, mix_lora=mix_lora,
        w0=w0.reshape(1, width), a0=a0.reshape(1, width),
        w_up=_pad_rows(w_up, n_lora_pad).astype(BF16),
        a_up=_pad_rows(jnp.concatenate([zeros(rank_w), a_up]), n_lora_pad).astype(BF16),
        g_up=_pad_rows(jnp.concatenate([zeros(rank_w + rank_a), g_up]), n_lora_pad).astype(BF16),
        k_k=k_k.reshape(1, width), k_a=k_a.reshape(1, width), r_k=r_k.reshape(1, width),
        ln_w=ln_w.reshape(1, width), ln_b=ln_b.reshape(1, width),
    )
    attn, rw = _mixers(qkv, _band_rows(rel_bias_table), rkv, lora, prm, batch=batch, seq=seq, n_heads=n_heads)

    h1 = _out_proj(attn, rw, w_out.astype(BF16), h, tm=tile_m // 2, tn=d)
    return _ffn(h1, norm2_g.reshape(1, d), w_gate_up.astype(BF16), w_down.astype(BF16), out_g.reshape(1, d),
                tm=tile_m, th=512)


def kernel(x, norm1_g, w_in, rwkv_shift_mix, rwkv_w0, rwkv_w_up, rwkv_a0, rwkv_a_up, rwkv_g_up, rwkv_k_k, rwkv_k_a, rwkv_r_k, rwkv_ln_w, rwkv_ln_b, rel_bias_table, w_out, norm2_g, w_gate_up, w_down, final_g):
    batch, seq, d = x.shape
    depth = w_in.shape[0]
    assert depth == 1, "the final RMSNorm is fused into the last layer's FFN kernel"
    assert seq % DILATED_PATTERNS[-1][0] == 0 and seq % CHUNK == 0
    h = x.reshape(batch * seq, d)
    out = _layer(h, norm1_g[0], w_in[0], rwkv_shift_mix[0], rwkv_w0[0], rwkv_w_up[0], rwkv_a0[0],
                 rwkv_a_up[0], rwkv_g_up[0], rwkv_k_k[0], rwkv_k_a[0], rwkv_r_k[0], rwkv_ln_w[0],
                 rwkv_ln_b[0], rel_bias_table, w_out[0], norm2_g[0], w_gate_up[0], w_down[0], final_g,
                 batch=batch, seq=seq, tile_m=min(1024, batch * seq))
    return out.reshape(batch, seq, d)
```

```python
import functools
import math

import jax
import jax.numpy as jnp
from jax import lax
from jax.experimental import pallas as pl
from jax.experimental.pallas import tpu as pltpu

F32 = jnp.float32
BF16 = jnp.bfloat16

HEAD_DIM = 64
LANES = 128
DILATED_PATTERNS = ((128, 1), (512, 4), (2048, 16))
REL_BUCKETS = 32
REL_MAX_DIST = 2048
RMS_EPS = 1e-6
GN_EPS = 64e-5
DECAY_SCALE = math.exp(-0.5)
ATTN_SCALE = HEAD_DIM ** -0.5
MASK_VALUE = -1e30
LOG2_E = math.log2(math.e)
CHUNK = 64
SEG_LANES = 256
VMEM_LIMIT = 56 * 1024 * 1024
FFN_VMEM_LIMIT = 60 * 1024 * 1024


def _mm(a, b):
    return jnp.dot(a.astype(BF16), b.astype(BF16), preferred_element_type=F32)


def _mm_nt(a, b):
    return lax.dot_general(a.astype(BF16), b.astype(BF16), (((1,), (1,)), ((), ())),
                           preferred_element_type=F32)


def _mm_tn(a, b):
    return lax.dot_general(a.astype(BF16), b.astype(BF16), (((0,), (0,)), ((), ())),
                           preferred_element_type=F32)


def _mm_split(a, b_exact):
    hi = a.astype(BF16)
    lo = (a - hi.astype(F32)).astype(BF16)
    return (jnp.dot(hi, b_exact, preferred_element_type=F32)
            + jnp.dot(lo, b_exact, preferred_element_type=F32))


def _params(semantics):
    return pltpu.CompilerParams(dimension_semantics=semantics, vmem_limit_bytes=VMEM_LIMIT)


def _in_proj_kernel(x_hbm, g_ref, w_ref, wl_ref, qkv_ref, rkv_ref, lora_ref, xn_ref, xbuf_ref, sem_ref,
                    *, n_qkv, n_lora):
    i, j = pl.program_id(0), pl.program_id(1)
    tm = xbuf_ref.shape[1]
    slot = i % 2

    def x_copy(tile, into):
        rows = pl.ds(pl.multiple_of(tile * tm, tm), tm)
        return pltpu.make_async_copy(x_hbm.at[rows, :], xbuf_ref.at[into], sem_ref.at[into])

    @pl.when((i == 0) & (j == 0))
    def _():
        x_copy(0, 0).start()

    @pl.when((j == 1) & (i + 1 < pl.num_programs(0)))
    def _():
        x_copy(i + 1, 1 - slot).start()

    @pl.when(j == 0)
    def _():
        x_copy(i, slot).wait()
        x = xbuf_ref[slot]
        ms = jnp.mean(x * x, axis=-1, keepdims=True)
        xn_ref[...] = (x * lax.rsqrt(ms + RMS_EPS) * g_ref[...]).astype(BF16)
        col = lax.broadcasted_iota(jnp.int32, (1, wl_ref.shape[1]), 1)
        wl = jnp.where(col < n_lora, wl_ref[...], jnp.zeros((), BF16))
        lora_ref[...] = jnp.dot(xn_ref[...], wl, preferred_element_type=F32)

    @pl.when((j >= 1) & (j <= n_qkv))
    def _():
        qkv_ref[...] = jnp.dot(xn_ref[...], w_ref[...], preferred_element_type=F32)

    @pl.when(j > n_qkv)
    def _():
        rkv_ref[...] = jnp.dot(xn_ref[...], w_ref[...], preferred_element_type=F32)


def _in_proj(x2, g, w, *, n_main, n_lora_pad, tm, tn):
    t, d = x2.shape
    n_lora = w.shape[1] - n_main
    n_qkv = (n_main // 2) // tn
    last = 2 * n_qkv - 1
    assert n_main % n_lora_pad == 0 and n_lora <= n_lora_pad
    return pl.pallas_call(
        functools.partial(_in_proj_kernel, n_qkv=n_qkv, n_lora=n_lora),
        grid=(t // tm, 2 * n_qkv + 1),
        in_specs=[
            pl.BlockSpec(memory_space=pl.ANY),
            pl.BlockSpec((1, d), lambda i, j: (0, 0)),
            pl.BlockSpec((d, tn), lambda i, j: (0, jnp.clip(j - 1, 0, last))),
            pl.BlockSpec((d, n_lora_pad), lambda i, j: (0, n_main // n_lora_pad)),
        ],
        out_specs=[
            pl.BlockSpec((tm, tn), lambda i, j: (i, jnp.clip(j - 1, 0, n_qkv - 1))),
            pl.BlockSpec((tm, tn), lambda i, j: (i, jnp.clip(j - 1 - n_qkv, 0, n_qkv - 1))),
            pl.BlockSpec((tm, n_lora_pad), lambda i, j: (i, 0)),
        ],
        out_shape=[
            jax.ShapeDtypeStruct((t, n_main // 2), F32),
            jax.ShapeDtypeStruct((t, n_main // 2), F32),
            jax.ShapeDtypeStruct((t, n_lora_pad), F32),
        ],
        scratch_shapes=[pltpu.VMEM((tm, d), BF16), pltpu.VMEM((2, tm, d), F32), pltpu.SemaphoreType.DMA((2,))],
        compiler_params=_params(("arbitrary", "arbitrary")),
        name="in_proj",
    )(x2, g, w, w)


ATTN_BLK = 128
ATTN_ROWS = max(w for w, _ in DILATED_PATTERNS)
ATTN_GROUP = 2


def _attn_stages(q_ref, kp_ref, kc_ref, vp_ref, vc_ref, band_ref, out_ref, bias_ref, o_ref, lse_ref,
                 *, n_heads, first, n, hp):
    blk = ATTN_BLK

    @pl.when(first)
    def _():
        for i in range(bias_ref.shape[0]):
            row = jnp.broadcast_to(band_ref[i:i + 1, :], (blk, 2 * blk))
            bias_ref[i] = pltpu.roll(row, 0, 1, stride=1, stride_axis=0)

    yield

    lane = lax.broadcasted_iota(jnp.int32, (1, LANES), 1)
    in_head = [(lane >= half * HEAD_DIM) & (lane < (half + 1) * HEAD_DIM) for half in range(2)]
    key_idx = lax.broadcasted_iota(jnp.int32, (1, 2 * blk), 1)
    first_ok = (key_idx >= blk) | (n > 0)
    ones_tile = jnp.ones((2 * blk, LANES), BF16)
    zero = jnp.zeros((), BF16)

    units = [(bi, window, dilation, sub, r)
             for bi, (window, dilation) in enumerate(DILATED_PATTERNS)
             for sub in range(ATTN_ROWS // window) for r in range(dilation)]
    groups = [units[g0:g0 + ATTN_GROUP] for g0 in range(0, len(units), ATTN_GROUP)]

    def scores(group):
        work = []
        for bi, window, dilation, sub, r in group:
            cur = pl.ds(sub * window + r, blk, stride=dilation)
            if sub == 0:
                prev_k, prev_v = kp_ref, vp_ref
                prev = pl.ds(ATTN_ROWS - window + r, blk, stride=dilation)
            else:
                prev_k, prev_v = kc_ref, vc_ref
                prev = pl.ds((sub - 1) * window + r, blk, stride=dilation)
            q = (q_ref[cur, :] * (ATTN_SCALE * LOG2_E)).astype(BF16)
            k = jnp.concatenate([prev_k[prev, :], kc_ref[cur, :]], axis=0).astype(BF16)
            v = jnp.concatenate([prev_v[prev, :], vc_ref[cur, :]], axis=0).astype(BF16)
            q_both = jnp.concatenate([jnp.where(m, q, zero) for m in in_head], axis=0)
            work.append((bi, cur, sub == 0, _mm_nt(q_both, k), jnp.concatenate([v, ones_tile], axis=1)))
        return work

    def softmax(work):
        done = []
        for bi, cur, at_start, s, v in work:
            bias = bias_ref[pl.ds(bi * n_heads + 2 * hp, 2)]
            su = s + bias.reshape(2 * blk, 2 * blk)
            if at_start:
                su = jnp.where(first_ok, su, MASK_VALUE)
            mx = jnp.max(su, axis=-1, keepdims=True)
            done.append((bi, cur, jnp.exp2(su - mx).astype(BF16), mx, v))
        return done

    def values(done):
        for bi, cur, p, mx, v in done:
            res = jnp.dot(p, v, preferred_element_type=F32)
            ra, rb = res[:blk], res[blk:]
            la, lb = ra[:, LANES:], rb[:, LANES:]
            o_ref[bi, cur, :] = jnp.where(in_head[0], ra[:, :LANES] / la, rb[:, :LANES] / lb)
            lse_ref[bi, cur, :] = jnp.where(in_head[0], mx[:blk] + jnp.log2(la), mx[blk:] + jnp.log2(lb))

    pending = None
    for group in groups:
        work = scores(group)
        if pending is not None:
            values(pending)
        pending = softmax(work)
        yield
    values(pending)
    yield

    lses = [lse_ref[bi] for bi in range(len(DILATED_PATTERNS))]
    top = functools.reduce(jnp.maximum, lses)
    es = [jnp.exp2(l - top) for l in lses]
    num = functools.reduce(jnp.add, [e * o_ref[bi] for bi, e in enumerate(es)])
    out_ref[...] = (num / functools.reduce(jnp.add, es)).astype(out_ref.dtype)


def _t5_bucket(dist):
    exact = REL_BUCKETS // 2
    d_f = jnp.maximum(dist, 1).astype(F32)
    large = exact + (jnp.log(d_f / exact) / math.log(REL_MAX_DIST / exact)
                     * (REL_BUCKETS - exact)).astype(jnp.int32)
    large = jnp.minimum(large, REL_BUCKETS - 1)
    return jnp.where(dist < exact, dist, large)


def _band_rows(bias_table):
    blk = ATTN_BLK
    rel = blk - jnp.arange(2 * blk)
    band = (rel >= 0) & (rel <= blk)
    rows = []
    for _, dilation in DILATED_PATTERNS:
        bias = bias_table[_t5_bucket(jnp.clip(rel, 0, blk) * dilation)]
        rows.append(jnp.where(band[:, None], bias.astype(F32) * LOG2_E, MASK_VALUE).T)
    return jnp.concatenate(rows, axis=0)


def _rwkv_stages(rkv_ref, lora_ref, mix_rkv_ref, mix_lora_ref, w0_ref, w_up_ref, a0_ref, a_up_ref,
                 g_up_ref, kk_ref, ka_ref, rk_ref, lnw_ref, lnb_ref, seg_ref, tri_ref,
                 out_ref,
                 state_ref, carry_rkv_ref, carry_lora_ref,
                 rt_ref, at_ref, bh_ref, kh_ref, bc_ref, kc_ref, v_ref, pc_ref, y_ref, *, width, lo):
    n_pairs = width // LANES
    n_seq, ch = rkv_ref.shape[0], CHUNK
    rows = n_seq * ch

    row = lax.broadcasted_iota(jnp.int32, (rows, 1), 0) % ch

    def per_seq(row_of):
        return jnp.concatenate([jnp.broadcast_to(row_of(b), (ch, row_of(b).shape[-1]))
                                for b in range(n_seq)], axis=0)

    def token_shift(z_ref, carry_ref, mix):
        z = z_ref[:, lo:lo + ch, :].reshape(rows, z_ref.shape[-1])
        prev = jnp.where(row == 0, per_seq(lambda b: carry_ref[b]), pltpu.roll(z, 1, axis=0))
        for b in range(n_seq):
            carry_ref[b] = z[(b + 1) * ch - 1:(b + 1) * ch, :]
        return z + (prev - z) * mix

    z = token_shift(rkv_ref, carry_rkv_ref, mix_rkv_ref[...])
    zl = token_shift(lora_ref, carry_lora_ref, mix_lora_ref[...])
    r, k, v = z[:, :width], z[:, width:2 * width], z[:, 2 * width:]

    lw = -DECAY_SCALE * jax.nn.sigmoid(w0_ref[...] + _mm(jnp.tanh(zl), w_up_ref[...]))
    a_sig = jax.nn.sigmoid(a0_ref[...] + _mm(zl, a_up_ref[...]))
    gate = _mm(jax.nn.sigmoid(zl), g_up_ref[...])

    seg = seg_ref[...]

    def head_sums(x):
        tiles = [_mm_split(x[:, t:t + SEG_LANES], seg) for t in range(0, width, SEG_LANES)]
        return jnp.concatenate(tiles, axis=1)

    kk = k * kk_ref[...]
    kk = kk / jnp.maximum(jnp.sqrt(head_sums(kk * kk)), 1e-12)
    k = k * (1.0 + (a_sig - 1.0) * ka_ref[...])
    bonus = head_sums(r * k * rk_ref[...]) * v
    a_in = -kk
    b_in = kk * a_sig

    lw_hi = lw.astype(BF16)
    lw_lo = (lw - lw_hi.astype(F32)).astype(BF16)
    tri = tri_ref[...]
    cum = (jnp.dot(tri, lw_hi, preferred_element_type=F32)
           + jnp.dot(tri, lw_lo, preferred_element_type=F32))
    cum_end = per_seq(lambda b: cum[(b + 1) * ch - 1:(b + 1) * ch, :])
    e_neg = jnp.exp(-cum)
    e_end = jnp.exp(cum_end - cum)

    n_units = n_seq * n_pairs

    def put(ref, val):
        for b in range(n_seq):
            for p in range(n_pairs):
                ref[b * n_pairs + p] = val[b * ch:b * ch + ref.shape[1], p * LANES:(p + 1) * LANES]

    put(rt_ref, r * jnp.exp(cum))
    put(at_ref, a_in * jnp.exp(cum - lw))
    put(bh_ref, b_in * e_neg)
    put(kh_ref, k * e_neg)
    put(bc_ref, b_in * e_end)
    put(kc_ref, k * e_end)
    put(v_ref, v)
    put(pc_ref, jnp.exp(cum_end))
    yield

    ti = lax.broadcasted_iota(jnp.int32, (ch, 2 * ch), 0)
    si = lax.broadcasted_iota(jnp.int32, (ch, 2 * ch), 1) % ch
    incl = ti >= si
    strict = ti > si
    lane = lax.broadcasted_iota(jnp.int32, (1, 2 * LANES), 1)
    head_a2 = (lane % LANES) < HEAD_DIM
    head_a = head_a2[:, :LANES]
    di = lax.broadcasted_iota(jnp.int32, (LANES, LANES), 0)
    dj = lax.broadcasted_iota(jnp.int32, (LANES, LANES), 1)
    same_head = (di < HEAD_DIM) == (dj < HEAD_DIM)
    diag = di == dj

    zero = jnp.zeros((), BF16)

    def stack(x, mask):
        x = x.astype(BF16)
        return jnp.concatenate([jnp.where(mask, x, zero), jnp.where(mask, zero, x)], axis=0)

    def block_diag(m):
        m = m.astype(BF16)
        return jnp.where(same_head, jnp.concatenate([m, m], axis=0), zero)

    pairs = range(n_units)
    zero_tile = jnp.zeros((2 * ch, LANES), BF16)
    a_rb, a_ab, a_rk, a_ak, vst = [], [], [], [], []
    for p in pairs:
        lhs = jnp.concatenate([rt_ref[p], at_ref[p]], axis=0)
        rhs = jnp.concatenate([stack(bh_ref[p], head_a), stack(kh_ref[p], head_a)], axis=0)
        a_bk = _mm_nt(lhs, rhs).astype(BF16)
        a_rb.append(jnp.where(incl, a_bk[:ch, :LANES], zero))
        a_ab.append(jnp.where(strict, a_bk[ch:, :LANES], zero))
        a_rk.append(jnp.where(incl, a_bk[:ch, LANES:], zero))
        a_ak.append(jnp.where(strict, a_bk[ch:, LANES:], zero))
        vst.append(stack(v_ref[p], head_a))
    yield

    xs = [jnp.concatenate([at_ref[p], _mm(a_ak[p], vst[p])], axis=1) for p in pairs]
    yield
    nk = a_ab
    for _ in range(int(math.log2(ch)) - 1):
        nk_next = [_mm(n, block_diag(n)) for n in nk]
        xs = [x + _mm(n, stack(x, head_a2)) for n, x in zip(nk, xs)]
        nk = nk_next
        yield
    xs = [x + _mm(n, stack(x, head_a2)) for n, x in zip(nk, xs)]
    yield

    ax = [_mm(jnp.concatenate([a_rb[p], a_rk[p]], axis=1),
              jnp.concatenate([stack(xs[p], head_a2), jnp.concatenate([zero_tile, vst[p]], axis=1)], axis=0))
          for p in pairs]
    mg = [_mm_tn(jnp.concatenate([bc_ref[p], kc_ref[p]], axis=0),
                 jnp.concatenate([xs[p], jnp.concatenate([jnp.zeros((ch, LANES), F32), v_ref[p]], axis=1)],
                                 axis=0))
          for p in pairs]
    yield
    for p in pairs:
        q_acc = rt_ref[p] + ax[p][:, :LANES]
        m_mat = jnp.where(same_head, mg[p][:, :LANES], 0.0) + jnp.where(diag, pc_ref[p], 0.0)
        g_mat = jnp.where(same_head, mg[p][:, LANES:], 0.0)
        res = _mm(jnp.concatenate([q_acc, m_mat], axis=0), state_ref[p])
        y_ref[p] = res[:ch] + ax[p][:, LANES:]
        state_ref[p] = res[ch:] + g_mat
    yield

    y = jnp.concatenate([jnp.concatenate([y_ref[b * n_pairs + p] for p in range(n_pairs)], axis=1)
                         for b in range(n_seq)], axis=0)
    inv_n = 1.0 / HEAD_DIM
    mu = head_sums(y) * inv_n
    d = y - mu
    var = head_sums(d * d) * inv_n
    yn = d * lax.rsqrt(var + GN_EPS) * lnw_ref[...] + lnb_ref[...]
    out_ref[:, lo:lo + ch, :] = ((yn + bonus) * gate).reshape(n_seq, ch, width).astype(out_ref.dtype)


N_ATTN_IN, N_ATTN_SCRATCH = 6, 3
N_RWKV_IN, N_RWKV_SCRATCH = 16, 12
RWKV_STAGES_PER_ATTN_STAGE = 1


def _mixers_kernel(*refs, n_heads, width, nb, n_pairs, chunks_per_step):
    attn_in, refs = refs[:N_ATTN_IN], refs[N_ATTN_IN:]
    rwkv_in, refs = refs[:N_RWKV_IN], refs[N_RWKV_IN:]
    attn_out, rw_out, refs = refs[0], refs[1], refs[2:]
    attn_scr, rwkv_scr = refs[:N_ATTN_SCRATCH], refs[N_ATTN_SCRATCH:]
    c = pl.program_id(0)
    state_ref, carry_rkv_ref, carry_lora_ref = rwkv_scr[:3]

    @pl.when(c == 0)
    def _():
        state_ref[...] = jnp.zeros_like(state_ref)
        carry_rkv_ref[...] = jnp.zeros_like(carry_rkv_ref)
        carry_lora_ref[...] = jnp.zeros_like(carry_lora_ref)

    attn = _attn_stages(*attn_in, attn_out, *attn_scr, n_heads=n_heads, first=c == 0,
                        n=(c // n_pairs) % nb, hp=c % n_pairs)
    next(attn)

    def rwkv_chunks():
        for i in range(chunks_per_step):
            yield from _rwkv_stages(*rwkv_in, rw_out, *rwkv_scr, width=width, lo=i * CHUNK)

    rwkv = rwkv_chunks()
    running = [True, True]
    while any(running):
        for _ in range(RWKV_STAGES_PER_ATTN_STAGE):
            if running[0] and next(rwkv, "done") == "done":
                running[0] = False
        if running[1] and next(attn, "done") == "done":
            running[1] = False


def _mixers(qkv, band, rkv, lora, prm, *, batch, seq, n_heads):
    t, w3 = rkv.shape
    width = w3 // 3
    n_lora = lora.shape[1]
    ch = CHUNK
    n_pairs = width // LANES
    nb = seq // ATTN_ROWS
    n_pat = len(DILATED_PATTERNS)
    n_steps = batch * nb * n_pairs
    assert (seq // ch) % n_steps == 0, "RWKV chunks must divide evenly over the attention steps"
    chunks_per_step = (seq // ch) // n_steps
    rows_per_step = chunks_per_step * ch

    a_blk = lambda c: (c // n_pairs) // nb * nb + (c // n_pairs) % nb
    a_prev = lambda c: (c // n_pairs) // nb * nb + jnp.maximum((c // n_pairs) % nb - 1, 0)
    aspec = lambda f: pl.BlockSpec((ATTN_ROWS, LANES), f)
    attn_specs = [
        aspec(lambda c: (a_blk(c), c % n_pairs)),
        aspec(lambda c: (a_prev(c), n_pairs + c % n_pairs)),
        aspec(lambda c: (a_blk(c), n_pairs + c % n_pairs)),
        aspec(lambda c: (a_prev(c), 2 * n_pairs + c % n_pairs)),
        aspec(lambda c: (a_blk(c), 2 * n_pairs + c % n_pairs)),
        pl.BlockSpec(band.shape, lambda c: (0, 0)),
    ]

    row = lambda n: pl.BlockSpec((1, n), lambda c: (0, 0))
    full = lambda a: pl.BlockSpec(a.shape, lambda c: (0,) * a.ndim)
    head_id = jnp.arange(SEG_LANES) // HEAD_DIM
    seg = (head_id[:, None] == head_id[None, :]).astype(BF16)
    pos = jnp.arange(batch * ch)
    tri = ((pos[:, None] >= pos[None, :]) & (pos[:, None] // ch == pos[None, :] // ch)).astype(BF16)
    n_units = batch * n_pairs
    big = lambda: pltpu.VMEM((n_units, ch, LANES), F32)
    rwkv_args = (rkv.reshape(batch, seq, w3), lora.reshape(batch, seq, n_lora), prm["mix_rkv"],
                 prm["mix_lora"], prm["w0"], prm["w_up"], prm["a0"], prm["a_up"], prm["g_up"], prm["k_k"],
                 prm["k_a"], prm["r_k"], prm["ln_w"], prm["ln_b"], seg, tri)
    rwkv_specs = [
        pl.BlockSpec((batch, rows_per_step, w3), lambda c: (0, c, 0)),
        pl.BlockSpec((batch, rows_per_step, n_lora), lambda c: (0, c, 0)),
        row(w3), row(n_lora), row(width), full(prm["w_up"]), row(width), full(prm["a_up"]),
        full(prm["g_up"]), row(width), row(width), row(width), row(width), row(width),
        full(seg), full(tri),
    ]
    assert len(attn_specs) == N_ATTN_IN and len(rwkv_specs) == N_RWKV_IN
    attn, rw = pl.pallas_call(
        functools.partial(_mixers_kernel, n_heads=n_heads, width=width, nb=nb, n_pairs=n_pairs,
                          chunks_per_step=chunks_per_step),
        grid=(n_steps,),
        in_specs=attn_specs + rwkv_specs,
        out_specs=[aspec(lambda c: (a_blk(c), c % n_pairs)),
                   pl.BlockSpec((batch, rows_per_step, width), lambda c: (0, c, 0))],
        out_shape=[jax.ShapeDtypeStruct((t, width), BF16),
                   jax.ShapeDtypeStruct((batch, seq, width), BF16)],
        scratch_shapes=[
            pltpu.VMEM((n_pat * n_heads, ATTN_BLK, 2 * ATTN_BLK), F32),
            pltpu.VMEM((n_pat, ATTN_ROWS, LANES), F32),
            pltpu.VMEM((n_pat, ATTN_ROWS, LANES), F32),
            pltpu.VMEM((n_units, LANES, LANES), F32),
            pltpu.VMEM((batch, 1, w3), F32),
            pltpu.VMEM((batch, 1, n_lora), F32),
            big(), big(), big(), big(), big(), big(), big(),
            pltpu.VMEM((n_units, 1, LANES), F32),
            big(),
        ],
        compiler_params=_params(("arbitrary",)),
        name="mixers",
    )(qkv, qkv, qkv, qkv, qkv, band, *rwkv_args)
    return attn, rw.reshape(t, width)


def _out_proj_kernel(attn_ref, rw_ref, wa_ref, wr_ref, x_ref, out_ref):
    out_ref[...] = (x_ref[...] + jnp.dot(attn_ref[...], wa_ref[...], preferred_element_type=F32)
                    + jnp.dot(rw_ref[...], wr_ref[...], preferred_element_type=F32))


def _out_proj(attn, rw, w_out, x2, *, tm, tn):
    t, width = attn.shape
    d = w_out.shape[1]
    return pl.pallas_call(
        _out_proj_kernel,
        grid=(t // tm, d // tn),
        in_specs=[
            pl.BlockSpec((tm, width), lambda i, j: (i, 0)),
            pl.BlockSpec((tm, width), lambda i, j: (i, 0)),
            pl.BlockSpec((width, tn), lambda i, j: (0, j)),
            pl.BlockSpec((width, tn), lambda i, j: (1, j)),
            pl.BlockSpec((tm, tn), lambda i, j: (i, j)),
        ],
        out_specs=pl.BlockSpec((tm, tn), lambda i, j: (i, j)),
        out_shape=jax.ShapeDtypeStruct((t, d), F32),
        compiler_params=_params(("arbitrary", "arbitrary")),
        name="out_proj",
    )(attn, rw, w_out, w_out, x2)


FFN_DOWN_COLS = 512


def _ffn_kernel(h_hbm, g_ref, wg_ref, wu_ref, wd_ref, og_ref, out_ref, hn_ref, hbuf_ref, sem_ref):
    i, j = pl.program_id(0), pl.program_id(1)
    tm = hbuf_ref.shape[0]

    def h_copy(tile):
        rows = pl.ds(pl.multiple_of(tile * tm, tm), tm)
        return pltpu.make_async_copy(h_hbm.at[rows, :], hbuf_ref, sem_ref)

    @pl.when((i == 0) & (j == 0))
    def _():
        h_copy(0).start()

    @pl.when((j == 1) & (i + 1 < pl.num_programs(0)))
    def _():
        h_copy(i + 1).start()

    @pl.when(j == 0)
    def _():
        h_copy(i).wait()
        h = hbuf_ref[...]
        ms = jnp.mean(h * h, axis=-1, keepdims=True)
        hn_ref[...] = (h * lax.rsqrt(ms + RMS_EPS) * g_ref[...]).astype(BF16)
        out_ref[...] = h

    hn = hn_ref[...]
    gate = jnp.dot(hn, wg_ref[...], preferred_element_type=F32)
    up = jnp.dot(hn, wu_ref[...], preferred_element_type=F32)
    act = (gate * jax.nn.sigmoid(gate) * up).astype(BF16)
    for c0 in range(0, out_ref.shape[1], FFN_DOWN_COLS):
        cols = slice(c0, c0 + FFN_DOWN_COLS)
        out_ref[:, cols] += jnp.dot(act, wd_ref[:, cols], preferred_element_type=F32)

    @pl.when(j == pl.num_programs(1) - 1)
    def _():
        h = out_ref[...]
        ms = jnp.mean(h * h, axis=-1, keepdims=True)
        out_ref[...] = h * lax.rsqrt(ms + RMS_EPS) * og_ref[...]


def _ffn(h, g, w_gate_up, w_down, out_g, *, tm, th):
    t, d = h.shape
    hidden = w_down.shape[0]
    n_h = hidden // th
    return pl.pallas_call(
        _ffn_kernel,
        grid=(t // tm, n_h),
        in_specs=[
            pl.BlockSpec(memory_space=pl.ANY),
            pl.BlockSpec((1, d), lambda i, j: (0, 0)),
            pl.BlockSpec((d, th), lambda i, j: (0, j)),
            pl.BlockSpec((d, th), lambda i, j: (0, n_h + j)),
            pl.BlockSpec((th, d), lambda i, j: (j, 0)),
            pl.BlockSpec((1, d), lambda i, j: (0, 0)),
        ],
        out_specs=pl.BlockSpec((tm, d), lambda i, j: (i, 0)),
        out_shape=jax.ShapeDtypeStruct((t, d), F32),
        scratch_shapes=[pltpu.VMEM((tm, d), BF16), pltpu.VMEM((tm, d), F32), pltpu.SemaphoreType.DMA(())],
        compiler_params=pltpu.CompilerParams(dimension_semantics=("arbitrary", "arbitrary"),
                                             vmem_limit_bytes=FFN_VMEM_LIMIT),
        name="ffn",
    )(h, g, w_gate_up, w_gate_up, w_down, out_g)


def _pad_rows(a, rows):
    return jnp.pad(a, ((0, rows - a.shape[0]), (0, 0)))


def _layer(h, norm1_g, w_in, shift_mix, w0, w_up, a0, a_up, g_up, k_k, k_a, r_k, ln_w, ln_b,
           rel_bias_table, w_out, norm2_g, w_gate_up, w_down, out_g, *, batch, seq, tile_m):
    d = h.shape[1]
    width = w_out.shape[0] // 2
    n_heads = width // HEAD_DIM
    n_main = 6 * width
    rank_w, rank_a, rank_g = w_up.shape[0], a_up.shape[0], g_up.shape[0]
    n_lora = rank_w + rank_a + rank_g
    n_lora_pad = -(-n_lora // LANES) * LANES

    qkv, rkv, lora = _in_proj(h, norm1_g.reshape(1, d), w_in.astype(BF16), n_main=n_main,
                              n_lora_pad=n_lora_pad, tm=tile_m, tn=width)

    mix_rkv = shift_mix[:3 * width].reshape(1, 3 * width)
    mix_lora = jnp.pad(shift_mix[3 * width:], (0, n_lora_pad - n_lora)).reshape(1, n_lora_pad)
    zeros = lambda n: jnp.zeros((n, width), F32)
    prm = dict(
        mix_rkv=mix_rkv, mix_lora=mix_lora,
        w0=w0.reshape(1, width), a0=a0.reshape(1, width),
        w_up=_pad_rows(w_up, n_lora_pad).astype(BF16),
        a_up=_pad_rows(jnp.concatenate([zeros(rank_w), a_up]), n_lora_pad).astype(BF16),
        g_up=_pad_rows(jnp.concatenate([zeros(rank_w + rank_a), g_up]), n_lora_pad).astype(BF16),
        k_k=k_k.reshape(1, width), k_a=k_a.reshape(1, width), r_k=r_k.reshape(1, width),
        ln_w=ln_w.reshape(1, width), ln_b=ln_b.reshape(1, width),
    )
    attn, rw = _mixers(qkv, _band_rows(rel_bias_table), rkv, lora, prm, batch=batch, seq=seq, n_heads=n_heads)

    h1 = _out_proj(attn, rw, w_out.astype(BF16), h, tm=tile_m // 2, tn=d)
    return _ffn(h1, norm2_g.reshape(1, d), w_gate_up.astype(BF16), w_down.astype(BF16), out_g.reshape(1, d),
                tm=tile_m, th=512)


def kernel(x, norm1_g, w_in, rwkv_shift_mix, rwkv_w0, rwkv_w_up, rwkv_a0, rwkv_a_up, rwkv_g_up, rwkv_k_k, rwkv_k_a, rwkv_r_k, rwkv_ln_w, rwkv_ln_b, rel_bias_table, w_out, norm2_g, w_gate_up, w_down, final_g):
    batch, seq, d = x.shape
    depth = w_in.shape[0]
    assert depth == 1, "the final RMSNorm is fused into the last layer's FFN kernel"
    assert seq % DILATED_PATTERNS[-1][0] == 0 and seq % CHUNK == 0
    h = x.reshape(batch * seq, d)
    out = _layer(h, norm1_g[0], w_in[0], rwkv_shift_mix[0], rwkv_w0[0], rwkv_w_up[0], rwkv_a0[0],
                 rwkv_a_up[0], rwkv_g_up[0], rwkv_k_k[0], rwkv_k_a[0], rwkv_r_k[0], rwkv_ln_w[0],
                 rwkv_ln_b[0], rel_bias_table, w_out[0], norm2_g[0], w_gate_up[0], w_down[0], final_g,
                 batch=batch, seq=seq, tile_m=min(1024, batch * seq))
    return out.reshape(batch, seq, d)
```

```python
import functools
import math

import jax
import jax.numpy as jnp
from jax import lax
from jax.experimental import pallas as pl
from jax.experimental.pallas import tpu as pltpu

F32 = jnp.float32
BF16 = jnp.bfloat16

HEAD_DIM = 64
LANES = 128
DILATED_PATTERNS = ((128, 1), (512, 4), (2048, 16))
REL_BUCKETS = 32
REL_MAX_DIST = 2048
RMS_EPS = 1e-6
GN_EPS = 64e-5
DECAY_SCALE = math.exp(-0.5)
ATTN_SCALE = HEAD_DIM ** -0.5
MASK_VALUE = -1e30
LOG2_E = math.log2(math.e)
CHUNK = 64
SEG_LANES = 256
VMEM_LIMIT = 56 * 1024 * 1024
FFN_VMEM_LIMIT = 60 * 1024 * 1024


def _mm(a, b):
    return jnp.dot(a.astype(BF16), b.astype(BF16), preferred_element_type=F32)


def _mm_nt(a, b):
    return lax.dot_general(a.astype(BF16), b.astype(BF16), (((1,), (1,)), ((), ())),
                           preferred_element_type=F32)


def _mm_tn(a, b):
    return lax.dot_general(a.astype(BF16), b.astype(BF16), (((0,), (0,)), ((), ())),
                           preferred_element_type=F32)


def _mm_split(a, b_exact):
    hi = a.astype(BF16)
    lo = (a - hi.astype(F32)).astype(BF16)
    return (jnp.dot(hi, b_exact, preferred_element_type=F32)
            + jnp.dot(lo, b_exact, preferred_element_type=F32))


def _params(semantics):
    return pltpu.CompilerParams(dimension_semantics=semantics, vmem_limit_bytes=VMEM_LIMIT)


def _in_proj_kernel(x_hbm, g_ref, w_ref, wl_ref, qkv_ref, rkv_ref, lora_ref, xn_ref, xbuf_ref, sem_ref,
                    *, n_qkv, n_lora):
    i, j = pl.program_id(0), pl.program_id(1)
    tm = xbuf_ref.shape[1]
    slot = i % 2

    def x_copy(tile, into):
        rows = pl.ds(pl.multiple_of(tile * tm, tm), tm)
        return pltpu.make_async_copy(x_hbm.at[rows, :], xbuf_ref.at[into], sem_ref.at[into])

    @pl.when((i == 0) & (j == 0))
    def _():
        x_copy(0, 0).start(priority=1)

    @pl.when((j == 1) & (i + 1 < pl.num_programs(0)))
    def _():
        x_copy(i + 1, 1 - slot).start(priority=1)

    @pl.when(j == 0)
    def _():
        x_copy(i, slot).wait()
        x = xbuf_ref[slot]
        ms = jnp.mean(x * x, axis=-1, keepdims=True)
        xn_ref[...] = (x * lax.rsqrt(ms + RMS_EPS) * g_ref[...]).astype(BF16)
        col = lax.broadcasted_iota(jnp.int32, (1, wl_ref.shape[1]), 1)
        wl = jnp.where(col < n_lora, wl_ref[...], jnp.zeros((), BF16))
        lora_ref[...] = jnp.dot(xn_ref[...], wl, preferred_element_type=F32)

    @pl.when((j >= 1) & (j <= n_qkv))
    def _():
        qkv_ref[...] = jnp.dot(xn_ref[...], w_ref[...], preferred_element_type=F32)

    @pl.when(j > n_qkv)
    def _():
        rkv_ref[...] = jnp.dot(xn_ref[...], w_ref[...], preferred_element_type=F32)


def _in_proj(x2, g, w, *, n_main, n_lora_pad, tm, tn):
    t, d = x2.shape
    n_lora = w.shape[1] - n_main
    n_qkv = (n_main // 2) // tn
    last = 2 * n_qkv - 1
    assert n_main % n_lora_pad == 0 and n_lora <= n_lora_pad
    return pl.pallas_call(
        functools.partial(_in_proj_kernel, n_qkv=n_qkv, n_lora=n_lora),
        grid=(t // tm, 2 * n_qkv + 1),
        in_specs=[
            pl.BlockSpec(memory_space=pl.ANY),
            pl.BlockSpec((1, d), lambda i, j: (0, 0)),
            pl.BlockSpec((d, tn), lambda i, j: (0, jnp.clip(j - 1, 0, last))),
            pl.BlockSpec((d, n_lora_pad), lambda i, j: (0, n_main // n_lora_pad)),
        ],
        out_specs=[
            pl.BlockSpec((tm, tn), lambda i, j: (i, jnp.clip(j - 1, 0, n_qkv - 1))),
            pl.BlockSpec((tm, tn), lambda i, j: (i, jnp.clip(j - 1 - n_qkv, 0, n_qkv - 1))),
            pl.BlockSpec((tm, n_lora_pad), lambda i, j: (i, 0)),
        ],
        out_shape=[
            jax.ShapeDtypeStruct((t, n_main // 2), F32),
            jax.ShapeDtypeStruct((t, n_main // 2), F32),
            jax.ShapeDtypeStruct((t, n_lora_pad), F32),
        ],
        scratch_shapes=[pltpu.VMEM((tm, d), BF16), pltpu.VMEM((2, tm, d), F32), pltpu.SemaphoreType.DMA((2,))],
        compiler_params=_params(("arbitrary", "arbitrary")),
        name="in_proj",
    )(x2, g, w, w)


ATTN_BLK = 128
ATTN_ROWS = max(w for w, _ in DILATED_PATTERNS)
ATTN_GROUP = 2


def _attn_stages(q_ref, kp_ref, kc_ref, vp_ref, vc_ref, band_ref, out_ref, bias_ref, o_ref, lse_ref,
                 *, n_heads, first, n, hp):
    blk = ATTN_BLK

    @pl.when(first)
    def _():
        for i in range(bias_ref.shape[0]):
            row = jnp.broadcast_to(band_ref[i:i + 1, :], (blk, 2 * blk))
            bias_ref[i] = pltpu.roll(row, 0, 1, stride=1, stride_axis=0)

    yield

    lane = lax.broadcasted_iota(jnp.int32, (1, LANES), 1)
    in_head = [(lane >= half * HEAD_DIM) & (lane < (half + 1) * HEAD_DIM) for half in range(2)]
    key_idx = lax.broadcasted_iota(jnp.int32, (1, 2 * blk), 1)
    first_ok = (key_idx >= blk) | (n > 0)
    ones_tile = jnp.ones((2 * blk, LANES), BF16)
    zero = jnp.zeros((), BF16)

    units = [(bi, window, dilation, sub, r)
             for bi, (window, dilation) in enumerate(DILATED_PATTERNS)
             for sub in range(ATTN_ROWS // window) for r in range(dilation)]
    groups = [units[g0:g0 + ATTN_GROUP] for g0 in range(0, len(units), ATTN_GROUP)]

    def scores(group):
        work = []
        for bi, window, dilation, sub, r in group:
            cur = pl.ds(sub * window + r, blk, stride=dilation)
            if sub == 0:
                prev_k, prev_v = kp_ref, vp_ref
                prev = pl.ds(ATTN_ROWS - window + r, blk, stride=dilation)
            else:
                prev_k, prev_v = kc_ref, vc_ref
                prev = pl.ds((sub - 1) * window + r, blk, stride=dilation)
            q = (q_ref[cur, :] * (ATTN_SCALE * LOG2_E)).astype(BF16)
            k = jnp.concatenate([prev_k[prev, :], kc_ref[cur, :]], axis=0).astype(BF16)
            v = jnp.concatenate([prev_v[prev, :], vc_ref[cur, :]], axis=0).astype(BF16)
            q_both = jnp.concatenate([jnp.where(m, q, zero) for m in in_head], axis=0)
            work.append((bi, cur, sub == 0, _mm_nt(q_both, k), jnp.concatenate([v, ones_tile], axis=1)))
        return work

    def softmax(work):
        done = []
        for bi, cur, at_start, s, v in work:
            bias = bias_ref[pl.ds(bi * n_heads + 2 * hp, 2)]
            su = s + bias.reshape(2 * blk, 2 * blk)
            if at_start:
                su = jnp.where(first_ok, su, MASK_VALUE)
            mx = jnp.max(su, axis=-1, keepdims=True)
            done.append((bi, cur, jnp.exp2(su - mx).astype(BF16), mx, v))
        return done

    def values(done):
        for bi, cur, p, mx, v in done:
            res = jnp.dot(p, v, preferred_element_type=F32)
            ra, rb = res[:blk], res[blk:]
            la, lb = ra[:, LANES:], rb[:, LANES:]
            o_ref[bi, cur, :] = jnp.where(in_head[0], ra[:, :LANES] / la, rb[:, :LANES] / lb)
            lse_ref[bi, cur, :] = jnp.where(in_head[0], mx[:blk] + jnp.log2(la), mx[blk:] + jnp.log2(lb))

    pending = None
    for group in groups:
        work = scores(group)
        if pending is not None:
            values(pending)
        pending = softmax(work)
        yield
    values(pending)
    yield

    lses = [lse_ref[bi] for bi in range(len(DILATED_PATTERNS))]
    top = functools.reduce(jnp.maximum, lses)
    es = [jnp.exp2(l - top) for l in lses]
    num = functools.reduce(jnp.add, [e * o_ref[bi] for bi, e in enumerate(es)])
    out_ref[...] = (num / functools.reduce(jnp.add, es)).astype(out_ref.dtype)


def _t5_bucket(dist):
    exact = REL_BUCKETS // 2
    d_f = jnp.maximum(dist, 1).astype(F32)
    large = exact + (jnp.log(d_f / exact) / math.log(REL_MAX_DIST / exact)
                     * (REL_BUCKETS - exact)).astype(jnp.int32)
    large = jnp.minimum(large, REL_BUCKETS - 1)
    return jnp.where(dist < exact, dist, large)


def _band_rows(bias_table):
    blk = ATTN_BLK
    rel = blk - jnp.arange(2 * blk)
    band = (rel >= 0) & (rel <= blk)
    rows = []
    for _, dilation in DILATED_PATTERNS:
        bias = bias_table[_t5_bucket(jnp.clip(rel, 0, blk) * dilation)]
        rows.append(jnp.where(band[:, None], bias.astype(F32) * LOG2_E, MASK_VALUE).T)
    return jnp.concatenate(rows, axis=0)


def _rwkv_stages(rkv_ref, lora_ref, mix_rkv_ref, mix_lora_ref, w0_ref, w_up_ref, a0_ref, a_up_ref,
                 g_up_ref, kk_ref, ka_ref, rk_ref, lnw_ref, lnb_ref, seg_ref, tri_ref,
                 out_ref,
                 state_ref, carry_rkv_ref, carry_lora_ref,
                 rt_ref, at_ref, bh_ref, kh_ref, bc_ref, kc_ref, v_ref, pc_ref, y_ref, *, width, lo):
    n_pairs = width // LANES
    n_seq, ch = rkv_ref.shape[0], CHUNK
    rows = n_seq * ch

    row = lax.broadcasted_iota(jnp.int32, (rows, 1), 0) % ch

    def per_seq(row_of):
        return jnp.concatenate([jnp.broadcast_to(row_of(b), (ch, row_of(b).shape[-1]))
                                for b in range(n_seq)], axis=0)

    def token_shift(z_ref, carry_ref, mix):
        z = z_ref[:, lo:lo + ch, :].reshape(rows, z_ref.shape[-1])
        prev = jnp.where(row == 0, per_seq(lambda b: carry_ref[b]), pltpu.roll(z, 1, axis=0))
        for b in range(n_seq):
            carry_ref[b] = z[(b + 1) * ch - 1:(b + 1) * ch, :]
        return z + (prev - z) * mix

    z = token_shift(rkv_ref, carry_rkv_ref, mix_rkv_ref[...])
    zl = token_shift(lora_ref, carry_lora_ref, mix_lora_ref[...])
    r, k, v = z[:, :width], z[:, width:2 * width], z[:, 2 * width:]

    lw = -DECAY_SCALE * jax.nn.sigmoid(w0_ref[...] + _mm(jnp.tanh(zl), w_up_ref[...]))
    a_sig = jax.nn.sigmoid(a0_ref[...] + _mm(zl, a_up_ref[...]))
    gate = _mm(jax.nn.sigmoid(zl), g_up_ref[...])

    seg = seg_ref[...]

    def head_sums(x):
        tiles = [_mm_split(x[:, t:t + SEG_LANES], seg) for t in range(0, width, SEG_LANES)]
        return jnp.concatenate(tiles, axis=1)

    kk = k * kk_ref[...]
    kk = kk / jnp.maximum(jnp.sqrt(head_sums(kk * kk)), 1e-12)
    k = k * (1.0 + (a_sig - 1.0) * ka_ref[...])
    bonus = head_sums(r * k * rk_ref[...]) * v
    a_in = -kk
    b_in = kk * a_sig

    lw_hi = lw.astype(BF16)
    lw_lo = (lw - lw_hi.astype(F32)).astype(BF16)
    tri = tri_ref[...]
    cum = (jnp.dot(tri, lw_hi, preferred_element_type=F32)
           + jnp.dot(tri, lw_lo, preferred_element_type=F32))
    cum_end = per_seq(lambda b: cum[(b + 1) * ch - 1:(b + 1) * ch, :])
    e_neg = jnp.exp(-cum)
    e_end = jnp.exp(cum_end - cum)

    n_units = n_seq * n_pairs

    def put(ref, val):
        for b in range(n_seq):
            for p in range(n_pairs):
                ref[b * n_pairs + p] = val[b * ch:b * ch + ref.shape[1], p * LANES:(p + 1) * LANES]

    put(rt_ref, r * jnp.exp(cum))
    put(at_ref, a_in * jnp.exp(cum - lw))
    put(bh_ref, b_in * e_neg)
    put(kh_ref, k * e_neg)
    put(bc_ref, b_in * e_end)
    put(kc_ref, k * e_end)
    put(v_ref, v)
    put(pc_ref, jnp.exp(cum_end))
    yield

    ti = lax.broadcasted_iota(jnp.int32, (ch, 2 * ch), 0)
    si = lax.broadcasted_iota(jnp.int32, (ch, 2 * ch), 1) % ch
    incl = ti >= si
    strict = ti > si
    lane = lax.broadcasted_iota(jnp.int32, (1, 2 * LANES), 1)
    head_a2 = (lane % LANES) < HEAD_DIM
    head_a = head_a2[:, :LANES]
    di = lax.broadcasted_iota(jnp.int32, (LANES, LANES), 0)
    dj = lax.broadcasted_iota(jnp.int32, (LANES, LANES), 1)
    same_head = (di < HEAD_DIM) == (dj < HEAD_DIM)
    diag = di == dj

    zero = jnp.zeros((), BF16)

    def stack(x, mask):
        x = x.astype(BF16)
        return jnp.concatenate([jnp.where(mask, x, zero), jnp.where(mask, zero, x)], axis=0)

    def block_diag(m):
        m = m.astype(BF16)
        return jnp.where(same_head, jnp.concatenate([m, m], axis=0), zero)

    pairs = range(n_units)
    zero_tile = jnp.zeros((2 * ch, LANES), BF16)
    a_rb, a_ab, a_rk, a_ak, vst = [], [], [], [], []
    for p in pairs:
        lhs = jnp.concatenate([rt_ref[p], at_ref[p]], axis=0)
        rhs = jnp.concatenate([stack(bh_ref[p], head_a), stack(kh_ref[p], head_a)], axis=0)
        a_bk = _mm_nt(lhs, rhs).astype(BF16)
        a_rb.append(jnp.where(incl, a_bk[:ch, :LANES], zero))
        a_ab.append(jnp.where(strict, a_bk[ch:, :LANES], zero))
        a_rk.append(jnp.where(incl, a_bk[:ch, LANES:], zero))
        a_ak.append(jnp.where(strict, a_bk[ch:, LANES:], zero))
        vst.append(stack(v_ref[p], head_a))
    yield

    xs = [jnp.concatenate([at_ref[p], _mm(a_ak[p], vst[p])], axis=1) for p in pairs]
    yield
    nk = a_ab
    for _ in range(int(math.log2(ch)) - 1):
        nk_next = [_mm(n, block_diag(n)) for n in nk]
        xs = [x + _mm(n, stack(x, head_a2)) for n, x in zip(nk, xs)]
        nk = nk_next
        yield
    xs = [x + _mm(n, stack(x, head_a2)) for n, x in zip(nk, xs)]
    yield

    ax = [_mm(jnp.concatenate([a_rb[p], a_rk[p]], axis=1),
              jnp.concatenate([stack(xs[p], head_a2), jnp.concatenate([zero_tile, vst[p]], axis=1)], axis=0))
          for p in pairs]
    mg = [_mm_tn(jnp.concatenate([bc_ref[p], kc_ref[p]], axis=0),
                 jnp.concatenate([xs[p], jnp.concatenate([jnp.zeros((ch, LANES), F32), v_ref[p]], axis=1)],
                                 axis=0))
          for p in pairs]
    yield
    for p in pairs:
        q_acc = rt_ref[p] + ax[p][:, :LANES]
        m_mat = jnp.where(same_head, mg[p][:, :LANES], 0.0) + jnp.where(diag, pc_ref[p], 0.0)
        g_mat = jnp.where(same_head, mg[p][:, LANES:], 0.0)
        res = _mm(jnp.concatenate([q_acc, m_mat], axis=0), state_ref[p])
        y_ref[p] = res[:ch] + ax[p][:, LANES:]
        state_ref[p] = res[ch:] + g_mat
    yield

    y = jnp.concatenate([jnp.concatenate([y_ref[b * n_pairs + p] for p in range(n_pairs)], axis=1)
                         for b in range(n_seq)], axis=0)
    inv_n = 1.0 / HEAD_DIM
    mu = head_sums(y) * inv_n
    d = y - mu
    var = head_sums(d * d) * inv_n
    yn = d * lax.rsqrt(var + GN_EPS) * lnw_ref[...] + lnb_ref[...]
    out_ref[:, lo:lo + ch, :] = ((yn + bonus) * gate).reshape(n_seq, ch, width).astype(out_ref.dtype)


N_ATTN_IN, N_ATTN_SCRATCH = 6, 3
N_RWKV_IN, N_RWKV_SCRATCH = 16, 12
RWKV_STAGES_PER_ATTN_STAGE = 1


def _mixers_kernel(*refs, n_heads, width, nb, n_pairs, chunks_per_step):
    attn_in, refs = refs[:N_ATTN_IN], refs[N_ATTN_IN:]
    rwkv_in, refs = refs[:N_RWKV_IN], refs[N_RWKV_IN:]
    attn_out, rw_out, refs = refs[0], refs[1], refs[2:]
    attn_scr, rwkv_scr = refs[:N_ATTN_SCRATCH], refs[N_ATTN_SCRATCH:]
    c = pl.program_id(0)
    state_ref, carry_rkv_ref, carry_lora_ref = rwkv_scr[:3]

    @pl.when(c == 0)
    def _():
        state_ref[...] = jnp.zeros_like(state_ref)
        carry_rkv_ref[...] = jnp.zeros_like(carry_rkv_ref)
        carry_lora_ref[...] = jnp.zeros_like(carry_lora_ref)

    attn = _attn_stages(*attn_in, attn_out, *attn_scr, n_heads=n_heads, first=c == 0,
                        n=(c // n_pairs) % nb, hp=c % n_pairs)
    next(attn)

    def rwkv_chunks():
        for i in range(chunks_per_step):
            yield from _rwkv_stages(*rwkv_in, rw_out, *rwkv_scr, width=width, lo=i * CHUNK)

    rwkv = rwkv_chunks()
    running = [True, True]
    while any(running):
        for _ in range(RWKV_STAGES_PER_ATTN_STAGE):
            if running[0] and next(rwkv, "done") == "done":
                running[0] = False
        if running[1] and next(attn, "done") == "done":
            running[1] = False


def _mixers(qkv, band, rkv, lora, prm, *, batch, seq, n_heads):
    t, w3 = rkv.shape
    width = w3 // 3
    n_lora = lora.shape[1]
    ch = CHUNK
    n_pairs = width // LANES
    nb = seq // ATTN_ROWS
    n_pat = len(DILATED_PATTERNS)
    n_steps = batch * nb * n_pairs
    assert (seq // ch) % n_steps == 0, "RWKV chunks must divide evenly over the attention steps"
    chunks_per_step = (seq // ch) // n_steps
    rows_per_step = chunks_per_step * ch

    a_blk = lambda c: (c // n_pairs) // nb * nb + (c // n_pairs) % nb
    a_prev = lambda c: (c // n_pairs) // nb * nb + jnp.maximum((c // n_pairs) % nb - 1, 0)
    aspec = lambda f: pl.BlockSpec((ATTN_ROWS, LANES), f)
    attn_specs = [
        aspec(lambda c: (a_blk(c), c % n_pairs)),
        aspec(lambda c: (a_prev(c), n_pairs + c % n_pairs)),
        aspec(lambda c: (a_blk(c), n_pairs + c % n_pairs)),
        aspec(lambda c: (a_prev(c), 2 * n_pairs + c % n_pairs)),
        aspec(lambda c: (a_blk(c), 2 * n_pairs + c % n_pairs)),
        pl.BlockSpec(band.shape, lambda c: (0, 0)),
    ]

    row = lambda n: pl.BlockSpec((1, n), lambda c: (0, 0))
    full = lambda a: pl.BlockSpec(a.shape, lambda c: (0,) * a.ndim)
    head_id = jnp.arange(SEG_LANES) // HEAD_DIM
    seg = (head_id[:, None] == head_id[None, :]).astype(BF16)
    pos = jnp.arange(batch * ch)
    tri = ((pos[:, None] >= pos[None, :]) & (pos[:, None] // ch == pos[None, :] // ch)).astype(BF16)
    n_units = batch * n_pairs
    big = lambda: pltpu.VMEM((n_units, ch, LANES), F32)
    rwkv_args = (rkv.reshape(batch, seq, w3), lora.reshape(batch, seq, n_lora), prm["mix_rkv"],
                 prm["mix_lora"], prm["w0"], prm["w_up"], prm["a0"], prm["a_up"], prm["g_up"], prm["k_k"],
                 prm["k_a"], prm["r_k"], prm["ln_w"], prm["ln_b"], seg, tri)
    rwkv_specs = [
        pl.BlockSpec((batch, rows_per_step, w3), lambda c: (0, c, 0)),
        pl.BlockSpec((batch, rows_per_step, n_lora), lambda c: (0, c, 0)),
        row(w3), row(n_lora), row(width), full(prm["w_up"]), row(width), full(prm["a_up"]),
        full(prm["g_up"]), row(width), row(width), row(width), row(width), row(width),
        full(seg), full(tri),
    ]
    assert len(attn_specs) == N_ATTN_IN and len(rwkv_specs) == N_RWKV_IN
    attn, rw = pl.pallas_call(
        functools.partial(_mixers_kernel, n_heads=n_heads, width=width, nb=nb, n_pairs=n_pairs,
                          chunks_per_step=chunks_per_step),
        grid=(n_steps,),
        in_specs=attn_specs + rwkv_specs,
        out_specs=[aspec(lambda c: (a_blk(c), c % n_pairs)),
                   pl.BlockSpec((batch, rows_per_step, width), lambda c: (0, c, 0))],
        out_shape=[jax.ShapeDtypeStruct((t, width), BF16),
                   jax.ShapeDtypeStruct((batch, seq, width), BF16)],
        scratch_shapes=[
            pltpu.VMEM((n_pat * n_heads, ATTN_BLK, 2 * ATTN_BLK), F32),
            pltpu.VMEM((n_pat, ATTN_ROWS, LANES), F32),
            pltpu.VMEM((n_pat, ATTN_ROWS, LANES), F32),
            pltpu.VMEM((n_units, LANES, LANES), F32),
            pltpu.VMEM((batch, 1, w3), F32),
            pltpu.VMEM((batch, 1, n_lora), F32),
            big(), big(), big(), big(), big(), big(), big(),
            pltpu.VMEM((n_units, 1, LANES), F32),
            big(),
        ],
        compiler_params=_params(("arbitrary",)),
        name="mixers",
    )(qkv, qkv, qkv, qkv, qkv, band, *rwkv_args)
    return attn, rw.reshape(t, width)


def _out_proj_kernel(attn_ref, rw_ref, wa_ref, wr_ref, x_ref, out_ref):
    out_ref[...] = (x_ref[...] + jnp.dot(attn_ref[...], wa_ref[...], preferred_element_type=F32)
                    + jnp.dot(rw_ref[...], wr_ref[...], preferred_element_type=F32))


def _out_proj(attn, rw, w_out, x2, *, tm, tn):
    t, width = attn.shape
    d = w_out.shape[1]
    return pl.pallas_call(
        _out_proj_kernel,
        grid=(t // tm, d // tn),
        in_specs=[
            pl.BlockSpec((tm, width), lambda i, j: (i, 0)),
            pl.BlockSpec((tm, width), lambda i, j: (i, 0)),
            pl.BlockSpec((width, tn), lambda i, j: (0, j)),
            pl.BlockSpec((width, tn), lambda i, j: (1, j)),
            pl.BlockSpec((tm, tn), lambda i, j: (i, j)),
        ],
        out_specs=pl.BlockSpec((tm, tn), lambda i, j: (i, j)),
        out_shape=jax.ShapeDtypeStruct((t, d), F32),
        compiler_params=_params(("arbitrary", "arbitrary")),
        name="out_proj",
    )(attn, rw, w_out, w_out, x2)


FFN_DOWN_COLS = 512


def _ffn_kernel(h_hbm, g_ref, wg_ref, wu_ref, wd_ref, og_ref, out_ref, hn_ref, hbuf_ref, sem_ref):
    i, j = pl.program_id(0), pl.program_id(1)
    tm = hbuf_ref.shape[0]

    def h_copy(tile):
        rows = pl.ds(pl.multiple_of(tile * tm, tm), tm)
        return pltpu.make_async_copy(h_hbm.at[rows, :], hbuf_ref, sem_ref)

    @pl.when((i == 0) & (j == 0))
    def _():
        h_copy(0).start(priority=1)

    @pl.when((j == 1) & (i + 1 < pl.num_programs(0)))
    def _():
        h_copy(i + 1).start(priority=1)

    @pl.when(j == 0)
    def _():
        h_copy(i).wait()
        h = hbuf_ref[...]
        ms = jnp.mean(h * h, axis=-1, keepdims=True)
        hn_ref[...] = (h * lax.rsqrt(ms + RMS_EPS) * g_ref[...]).astype(BF16)
        out_ref[...] = h

    hn = hn_ref[...]
    gate = jnp.dot(hn, wg_ref[...], preferred_element_type=F32)
    up = jnp.dot(hn, wu_ref[...], preferred_element_type=F32)
    act = (gate * jax.nn.sigmoid(gate) * up).astype(BF16)
    for c0 in range(0, out_ref.shape[1], FFN_DOWN_COLS):
        cols = slice(c0, c0 + FFN_DOWN_COLS)
        out_ref[:, cols] += jnp.dot(act, wd_ref[:, cols], preferred_element_type=F32)

    @pl.when(j == pl.num_programs(1) - 1)
    def _():
        h = out_ref[...]
        ms = jnp.mean(h * h, axis=-1, keepdims=True)
        out_ref[...] = h * lax.rsqrt(ms + RMS_EPS) * og_ref[...]


def _ffn(h, g, w_gate_up, w_down, out_g, *, tm, th):
    t, d = h.shape
    hidden = w_down.shape[0]
    n_h = hidden // th
    return pl.pallas_call(
        _ffn_kernel,
        grid=(t // tm, n_h),
        in_specs=[
            pl.BlockSpec(memory_space=pl.ANY),
            pl.BlockSpec((1, d), lambda i, j: (0, 0)),
            pl.BlockSpec((d, th), lambda i, j: (0, j)),
            pl.BlockSpec((d, th), lambda i, j: (0, n_h + j)),
            pl.BlockSpec((th, d), lambda i, j: (j, 0)),
            pl.BlockSpec((1, d), lambda i, j: (0, 0)),
        ],
        out_specs=pl.BlockSpec((tm, d), lambda i, j: (i, 0)),
        out_shape=jax.ShapeDtypeStruct((t, d), F32),
        scratch_shapes=[pltpu.VMEM((tm, d), BF16), pltpu.VMEM((tm, d), F32), pltpu.SemaphoreType.DMA(())],
        compiler_params=pltpu.CompilerParams(dimension_semantics=("arbitrary", "arbitrary"),
                                             vmem_limit_bytes=FFN_VMEM_LIMIT),
        name="ffn",
    )(h, g, w_gate_up, w_gate_up, w_down, out_g)


def _pad_rows(a, rows):
    return jnp.pad(a, ((0, rows - a.shape[0]), (0, 0)))


def _layer(h, norm1_g, w_in, shift_mix, w0, w_up, a0, a_up, g_up, k_k, k_a, r_k, ln_w, ln_b,
           rel_bias_table, w_out, norm2_g, w_gate_up, w_down, out_g, *, batch, seq, tile_m):
    d = h.shape[1]
    width = w_out.shape[0] // 2
    n_heads = width // HEAD_DIM
    n_main = 6 * width
    rank_w, rank_a, rank_g = w_up.shape[0], a_up.shape[0], g_up.shape[0]
    n_lora = rank_w + rank_a + rank_g
    n_lora_pad = -(-n_lora // LANES) * LANES

    qkv, rkv, lora = _in_proj(h, norm1_g.reshape(1, d), w_in.astype(BF16), n_main=n_main,
                              n_lora_pad=n_lora_pad, tm=tile_m, tn=width)

    mix_rkv = shift_mix[:3 * width].reshape(1, 3 * width)
    mix_lora = jnp.pad(shift_mix[3 * width:], (0, n_lora_pad - n_lora)).reshape(1, n_lora_pad)
    zeros = lambda n: jnp.zeros((n, width), F32)
    prm = dict(
        mix_rkv=mix_rkv, mix_lora=mix_lora,
        w0=w0.reshape(1, width), a0=a0.reshape(1, width),
        w_up=_pad_rows(w_up, n_lora_pad).astype(BF16),
        a_up=_pad_rows(jnp.concatenate([zeros(rank_w), a_up]), n_lora_pad).astype(BF16),
        g_up=_pad_rows(jnp.concatenate([zeros(rank_w + rank_a), g_up]), n_lora_pad).astype(BF16),
        k_k=k_k.reshape(1, width), k_a=k_a.reshape(1, width), r_k=r_k.reshape(1, width),
        ln_w=ln_w.reshape(1, width), ln_b=ln_b.reshape(1, width),
    )
    attn, rw = _mixers(qkv, _band_rows(rel_bias_table), rkv, lora, prm, batch=batch, seq=seq, n_heads=n_heads)

    h1 = _out_proj(attn, rw, w_out.astype(BF16), h, tm=tile_m // 2, tn=d)
    return _ffn(h1, norm2_g.reshape(1, d), w_gate_up.astype(BF16), w_down.astype(BF16), out_g.reshape(1, d),
                tm=tile_m, th=512)


def kernel(x, norm1_g, w_in, rwkv_shift_mix, rwkv_w0, rwkv_w_up, rwkv_a0, rwkv_a_up, rwkv_g_up, rwkv_k_k, rwkv_k_a, rwkv_r_k, rwkv_ln_w, rwkv_ln_b, rel_bias_table, w_out, norm2_g, w_gate_up, w_down, final_g):
    batch, seq, d = x.shape
    depth = w_in.shape[0]
    assert depth == 1, "the final RMSNorm is fused into the last layer's FFN kernel"
    assert seq % DILATED_PATTERNS[-1][0] == 0 and seq % CHUNK == 0
    h = x.reshape(batch * seq, d)
    out = _layer(h, norm1_g[0], w_in[0], rwkv_shift_mix[0], rwkv_w0[0], rwkv_w_up[0], rwkv_a0[0],
                 rwkv_a_up[0], rwkv_g_up[0], rwkv_k_k[0], rwkv_k_a[0], rwkv_r_k[0], rwkv_ln_w[0],
                 rwkv_ln_b[0], rel_bias_table, w_out[0], norm2_g[0], w_gate_up[0], w_down[0], final_g,
                 batch=batch, seq=seq, tile_m=min(1024, batch * seq))
    return out.reshape(batch, seq, d)
```

```python
import functools
import math

import jax
import jax.numpy as jnp
from jax import lax
from jax.experimental import pallas as pl
from jax.experimental.pallas import tpu as pltpu

F32 = jnp.float32
BF16 = jnp.bfloat16

HEAD_DIM = 64
LANES = 128
DILATED_PATTERNS = ((128, 1), (512, 4), (2048, 16))
REL_BUCKETS = 32
REL_MAX_DIST = 2048
RMS_EPS = 1e-6
GN_EPS = 64e-5
DECAY_SCALE = math.exp(-0.5)
ATTN_SCALE = HEAD_DIM ** -0.5
MASK_VALUE = -1e30
LOG2_E = math.log2(math.e)
CHUNK = 64
SEG_LANES = 256
VMEM_LIMIT = 56 * 1024 * 1024
FFN_VMEM_LIMIT = 60 * 1024 * 1024


def _mm(a, b):
    return jnp.dot(a.astype(BF16), b.astype(BF16), preferred_element_type=F32)


def _mm_nt(a, b):
    return lax.dot_general(a.astype(BF16), b.astype(BF16), (((1,), (1,)), ((), ())),
                           preferred_element_type=F32)


def _mm_tn(a, b):
    return lax.dot_general(a.astype(BF16), b.astype(BF16), (((0,), (0,)), ((), ())),
                           preferred_element_type=F32)


def _mm_split(a, b_exact):
    hi = a.astype(BF16)
    lo = (a - hi.astype(F32)).astype(BF16)
    return (jnp.dot(hi, b_exact, preferred_element_type=F32)
            + jnp.dot(lo, b_exact, preferred_element_type=F32))


def _params(semantics):
    return pltpu.CompilerParams(dimension_semantics=semantics, vmem_limit_bytes=VMEM_LIMIT)


def _in_proj_kernel(x_hbm, g_ref, w_ref, wl_ref, qkv_ref, rkv_ref, lora_ref, xn_ref, xbuf_ref, sem_ref,
                    *, n_qkv, n_lora):
    i, j = pl.program_id(0), pl.program_id(1)
    tm = xbuf_ref.shape[0]

    def x_copy(tile):
        rows = pl.ds(pl.multiple_of(tile * tm, tm), tm)
        return pltpu.make_async_copy(x_hbm.at[rows, :], xbuf_ref, sem_ref)

    @pl.when((i == 0) & (j == 0))
    def _():
        x_copy(0).start()

    @pl.when((j == 1) & (i + 1 < pl.num_programs(0)))
    def _():
        x_copy(i + 1).start()

    @pl.when(j == 0)
    def _():
        x_copy(i).wait()
        x = xbuf_ref[...]
        ms = jnp.mean(x * x, axis=-1, keepdims=True)
        xn_ref[...] = (x * lax.rsqrt(ms + RMS_EPS) * g_ref[...]).astype(BF16)
        col = lax.broadcasted_iota(jnp.int32, (1, wl_ref.shape[1]), 1)
        wl = jnp.where(col < n_lora, wl_ref[...], jnp.zeros((), BF16))
        lora_ref[...] = jnp.dot(xn_ref[...], wl, preferred_element_type=F32)

    @pl.when((j >= 1) & (j <= n_qkv))
    def _():
        qkv_ref[...] = jnp.dot(xn_ref[...], w_ref[...], preferred_element_type=F32)

    @pl.when(j > n_qkv)
    def _():
        rkv_ref[...] = jnp.dot(xn_ref[...], w_ref[...], preferred_element_type=F32)


def _in_proj(x2, g, w, *, n_main, n_lora_pad, tm, tn):
    t, d = x2.shape
    n_lora = w.shape[1] - n_main
    n_qkv = (n_main // 2) // tn
    last = 2 * n_qkv - 1
    assert n_main % n_lora_pad == 0 and n_lora <= n_lora_pad
    return pl.pallas_call(
        functools.partial(_in_proj_kernel, n_qkv=n_qkv, n_lora=n_lora),
        grid=(t // tm, 2 * n_qkv + 1),
        in_specs=[
            pl.BlockSpec(memory_space=pl.ANY),
            pl.BlockSpec((1, d), lambda i, j: (0, 0)),
            pl.BlockSpec((d, tn), lambda i, j: (0, jnp.clip(j - 1, 0, last))),
            pl.BlockSpec((d, n_lora_pad), lambda i, j: (0, n_main // n_lora_pad)),
        ],
        out_specs=[
            pl.BlockSpec((tm, tn), lambda i, j: (i, jnp.clip(j - 1, 0, n_qkv - 1))),
            pl.BlockSpec((tm, tn), lambda i, j: (i, jnp.clip(j - 1 - n_qkv, 0, n_qkv - 1))),
            pl.BlockSpec((tm, n_lora_pad), lambda i, j: (i, 0)),
        ],
        out_shape=[
            jax.ShapeDtypeStruct((t, n_main // 2), F32),
            jax.ShapeDtypeStruct((t, n_main // 2), F32),
            jax.ShapeDtypeStruct((t, n_lora_pad), F32),
        ],
        scratch_shapes=[pltpu.VMEM((tm, d), BF16), pltpu.VMEM((tm, d), F32), pltpu.SemaphoreType.DMA(())],
        compiler_params=_params(("arbitrary", "arbitrary")),
        name="in_proj",
    )(x2, g, w, w)


ATTN_BLK = 128
ATTN_ROWS = max(w for w, _ in DILATED_PATTERNS)
ATTN_GROUP = 2


def _attn_stages(q_ref, kp_ref, kc_ref, vp_ref, vc_ref, band_ref, out_ref, bias_ref, o_ref, lse_ref,
                 *, n_heads, first, n, hp):
    blk = ATTN_BLK

    @pl.when(first)
    def _():
        for i in range(bias_ref.shape[0]):
            row = jnp.broadcast_to(band_ref[i:i + 1, :], (blk, 2 * blk))
            bias_ref[i] = pltpu.roll(row, 0, 1, stride=1, stride_axis=0)

    yield

    lane = lax.broadcasted_iota(jnp.int32, (1, LANES), 1)
    in_head = [(lane >= half * HEAD_DIM) & (lane < (half + 1) * HEAD_DIM) for half in range(2)]
    key_idx = lax.broadcasted_iota(jnp.int32, (1, 2 * blk), 1)
    first_ok = (key_idx >= blk) | (n > 0)
    ones_tile = jnp.ones((2 * blk, LANES), BF16)
    zero = jnp.zeros((), BF16)

    units = [(bi, window, dilation, sub, r)
             for bi, (window, dilation) in enumerate(DILATED_PATTERNS)
             for sub in range(ATTN_ROWS // window) for r in range(dilation)]
    groups = [units[g0:g0 + ATTN_GROUP] for g0 in range(0, len(units), ATTN_GROUP)]

    def scores(group):
        work = []
        for bi, window, dilation, sub, r in group:
            cur = pl.ds(sub * window + r, blk, stride=dilation)
            if sub == 0:
                prev_k, prev_v = kp_ref, vp_ref
                prev = pl.ds(ATTN_ROWS - window + r, blk, stride=dilation)
            else:
                prev_k, prev_v = kc_ref, vc_ref
                prev = pl.ds((sub - 1) * window + r, blk, stride=dilation)
            q = (q_ref[cur, :] * (ATTN_SCALE * LOG2_E)).astype(BF16)
            k = jnp.concatenate([prev_k[prev, :], kc_ref[cur, :]], axis=0).astype(BF16)
            v = jnp.concatenate([prev_v[prev, :], vc_ref[cur, :]], axis=0).astype(BF16)
            q_both = jnp.concatenate([jnp.where(m, q, zero) for m in in_head], axis=0)
            work.append((bi, cur, sub == 0, _mm_nt(q_both, k), jnp.concatenate([v, ones_tile], axis=1)))
        return work

    def softmax(work):
        done = []
        for bi, cur, at_start, s, v in work:
            bias = bias_ref[pl.ds(bi * n_heads + 2 * hp, 2)]
            su = s + bias.reshape(2 * blk, 2 * blk)
            if at_start:
                su = jnp.where(first_ok, su, MASK_VALUE)
            mx = jnp.max(su, axis=-1, keepdims=True)
            done.append((bi, cur, jnp.exp2(su - mx).astype(BF16), mx, v))
        return done

    def values(done):
        for bi, cur, p, mx, v in done:
            res = jnp.dot(p, v, preferred_element_type=F32)
            ra, rb = res[:blk], res[blk:]
            la, lb = ra[:, LANES:], rb[:, LANES:]
            o_ref[bi, cur, :] = jnp.where(in_head[0], ra[:, :LANES] / la, rb[:, :LANES] / lb)
            lse_ref[bi, cur, :] = jnp.where(in_head[0], mx[:blk] + jnp.log2(la), mx[blk:] + jnp.log2(lb))

    pending = None
    for group in groups:
        work = scores(group)
        if pending is not None:
            values(pending)
        pending = softmax(work)
        yield
    values(pending)
    yield

    lses = [lse_ref[bi] for bi in range(len(DILATED_PATTERNS))]
    top = functools.reduce(jnp.maximum, lses)
    es = [jnp.exp2(l - top) for l in lses]
    num = functools.reduce(jnp.add, [e * o_ref[bi] for bi, e in enumerate(es)])
    out_ref[...] = (num / functools.reduce(jnp.add, es)).astype(out_ref.dtype)


def _t5_bucket(dist):
    exact = REL_BUCKETS // 2
    d_f = jnp.maximum(dist, 1).astype(F32)
    large = exact + (jnp.log(d_f / exact) / math.log(REL_MAX_DIST / exact)
                     * (REL_BUCKETS - exact)).astype(jnp.int32)
    large = jnp.minimum(large, REL_BUCKETS - 1)
    return jnp.where(dist < exact, dist, large)


def _band_rows(bias_table):
    blk = ATTN_BLK
    rel = blk - jnp.arange(2 * blk)
    band = (rel >= 0) & (rel <= blk)
    rows = []
    for _, dilation in DILATED_PATTERNS:
        bias = bias_table[_t5_bucket(jnp.clip(rel, 0, blk) * dilation)]
        rows.append(jnp.where(band[:, None], bias.astype(F32) * LOG2_E, MASK_VALUE).T)
    return jnp.concatenate(rows, axis=0)


def _rwkv_stages(rkv_ref, lora_ref, mix_rkv_ref, mix_lora_ref, w0_ref, w_up_ref, a0_ref, a_up_ref,
                 g_up_ref, kk_ref, ka_ref, rk_ref, lnw_ref, lnb_ref, seg_ref, tri_ref,
                 out_ref,
                 state_ref, carry_rkv_ref, carry_lora_ref,
                 rt_ref, at_ref, bh_ref, kh_ref, bc_ref, kc_ref, v_ref, pc_ref, y_ref, *, width, lo):
    n_pairs = width // LANES
    n_seq, ch = rkv_ref.shape[0], CHUNK
    rows = n_seq * ch

    row = lax.broadcasted_iota(jnp.int32, (rows, 1), 0) % ch

    def per_seq(row_of):
        return jnp.concatenate([jnp.broadcast_to(row_of(b), (ch, row_of(b).shape[-1]))
                                for b in range(n_seq)], axis=0)

    def token_shift(z_ref, carry_ref, mix):
        z = z_ref[:, lo:lo + ch, :].reshape(rows, z_ref.shape[-1])
        prev = jnp.where(row == 0, per_seq(lambda b: carry_ref[b]), pltpu.roll(z, 1, axis=0))
        for b in range(n_seq):
            carry_ref[b] = z[(b + 1) * ch - 1:(b + 1) * ch, :]
        return z + (prev - z) * mix

    z = token_shift(rkv_ref, carry_rkv_ref, mix_rkv_ref[...])
    zl = token_shift(lora_ref, carry_lora_ref, mix_lora_ref[...])
    r, k, v = z[:, :width], z[:, width:2 * width], z[:, 2 * width:]

    lw = -DECAY_SCALE * jax.nn.sigmoid(w0_ref[...] + _mm(jnp.tanh(zl), w_up_ref[...]))
    a_sig = jax.nn.sigmoid(a0_ref[...] + _mm(zl, a_up_ref[...]))
    gate = _mm(jax.nn.sigmoid(zl), g_up_ref[...])

    seg = seg_ref[...]

    def head_sums(x):
        tiles = [_mm_split(x[:, t:t + SEG_LANES], seg) for t in range(0, width, SEG_LANES)]
        return jnp.concatenate(tiles, axis=1)

    kk = k * kk_ref[...]
    kk = kk / jnp.maximum(jnp.sqrt(head_sums(kk * kk)), 1e-12)
    k = k * (1.0 + (a_sig - 1.0) * ka_ref[...])
    bonus = head_sums(r * k * rk_ref[...]) * v
    a_in = -kk
    b_in = kk * a_sig

    lw_hi = lw.astype(BF16)
    lw_lo = (lw - lw_hi.astype(F32)).astype(BF16)
    tri = tri_ref[...]
    cum = (jnp.dot(tri, lw_hi, preferred_element_type=F32)
           + jnp.dot(tri, lw_lo, preferred_element_type=F32))
    cum_end = per_seq(lambda b: cum[(b + 1) * ch - 1:(b + 1) * ch, :])
    e_neg = jnp.exp(-cum)
    e_end = jnp.exp(cum_end - cum)

    n_units = n_seq * n_pairs

    def put(ref, val):
        for b in range(n_seq):
            for p in range(n_pairs):
                ref[b * n_pairs + p] = val[b * ch:b * ch + ref.shape[1], p * LANES:(p + 1) * LANES]

    put(rt_ref, r * jnp.exp(cum))
    put(at_ref, a_in * jnp.exp(cum - lw))
    put(bh_ref, b_in * e_neg)
    put(kh_ref, k * e_neg)
    put(bc_ref, b_in * e_end)
    put(kc_ref, k * e_end)
    put(v_ref, v)
    put(pc_ref, jnp.exp(cum_end))
    yield

    ti = lax.broadcasted_iota(jnp.int32, (ch, 2 * ch), 0)
    si = lax.broadcasted_iota(jnp.int32, (ch, 2 * ch), 1) % ch
    incl = ti >= si
    strict = ti > si
    lane = lax.broadcasted_iota(jnp.int32, (1, 2 * LANES), 1)
    head_a2 = (lane % LANES) < HEAD_DIM
    head_a = head_a2[:, :LANES]
    di = lax.broadcasted_iota(jnp.int32, (LANES, LANES), 0)
    dj = lax.broadcasted_iota(jnp.int32, (LANES, LANES), 1)
    same_head = (di < HEAD_DIM) == (dj < HEAD_DIM)
    diag = di == dj

    zero = jnp.zeros((), BF16)

    def stack(x, mask):
        x = x.astype(BF16)
        return jnp.concatenate([jnp.where(mask, x, zero), jnp.where(mask, zero, x)], axis=0)

    def block_diag(m):
        m = m.astype(BF16)
        return jnp.where(same_head, jnp.concatenate([m, m], axis=0), zero)

    pairs = range(n_units)
    zero_tile = jnp.zeros((2 * ch, LANES), BF16)
    a_rb, a_ab, a_rk, a_ak, vst = [], [], [], [], []
    for p in pairs:
        lhs = jnp.concatenate([rt_ref[p], at_ref[p]], axis=0)
        rhs = jnp.concatenate([stack(bh_ref[p], head_a), stack(kh_ref[p], head_a)], axis=0)
        a_bk = _mm_nt(lhs, rhs).astype(BF16)
        a_rb.append(jnp.where(incl, a_bk[:ch, :LANES], zero))
        a_ab.append(jnp.where(strict, a_bk[ch:, :LANES], zero))
        a_rk.append(jnp.where(incl, a_bk[:ch, LANES:], zero))
        a_ak.append(jnp.where(strict, a_bk[ch:, LANES:], zero))
        vst.append(stack(v_ref[p], head_a))
    yield

    xs = [jnp.concatenate([at_ref[p], _mm(a_ak[p], vst[p])], axis=1) for p in pairs]
    yield
    nk = a_ab
    for _ in range(int(math.log2(ch)) - 1):
        nk_next = [_mm(n, block_diag(n)) for n in nk]
        xs = [x + _mm(n, stack(x, head_a2)) for n, x in zip(nk, xs)]
        nk = nk_next
        yield
    xs = [x + _mm(n, stack(x, head_a2)) for n, x in zip(nk, xs)]
    yield

    ax = [_mm(jnp.concatenate([a_rb[p], a_rk[p]], axis=1),
              jnp.concatenate([stack(xs[p], head_a2), jnp.concatenate([zero_tile, vst[p]], axis=1)], axis=0))
          for p in pairs]
    mg = [_mm_tn(jnp.concatenate([bc_ref[p], kc_ref[p]], axis=0),
                 jnp.concatenate([xs[p], jnp.concatenate([jnp.zeros((ch, LANES), F32), v_ref[p]], axis=1)],
                                 axis=0))
          for p in pairs]
    yield
    for p in pairs:
        q_acc = rt_ref[p] + ax[p][:, :LANES]
        m_mat = jnp.where(same_head, mg[p][:, :LANES], 0.0) + jnp.where(diag, pc_ref[p], 0.0)
        g_mat = jnp.where(same_head, mg[p][:, LANES:], 0.0)
        res = _mm(jnp.concatenate([q_acc, m_mat], axis=0), state_ref[p])
        y_ref[p] = res[:ch] + ax[p][:, LANES:]
        state_ref[p] = res[ch:] + g_mat
    yield

    y = jnp.concatenate([jnp.concatenate([y_ref[b * n_pairs + p] for p in range(n_pairs)], axis=1)
                         for b in range(n_seq)], axis=0)
    inv_n = 1.0 / HEAD_DIM
    mu = head_sums(y) * inv_n
    d = y - mu
    var = head_sums(d * d) * inv_n
    yn = d * lax.rsqrt(var + GN_EPS) * lnw_ref[...] + lnb_ref[...]
    out_ref[:, lo:lo + ch, :] = ((yn + bonus) * gate).reshape(n_seq, ch, width).astype(out_ref.dtype)


N_ATTN_IN, N_ATTN_SCRATCH = 6, 3
N_RWKV_IN, N_RWKV_SCRATCH = 16, 12
RWKV_STAGES_PER_ATTN_STAGE = 1


def _mixers_kernel(*refs, n_heads, width, nb, n_pairs, chunks_per_step):
    attn_in, refs = refs[:N_ATTN_IN], refs[N_ATTN_IN:]
    rwkv_in, refs = refs[:N_RWKV_IN], refs[N_RWKV_IN:]
    attn_out, rw_out, refs = refs[0], refs[1], refs[2:]
    attn_scr, rwkv_scr = refs[:N_ATTN_SCRATCH], refs[N_ATTN_SCRATCH:]
    c = pl.program_id(0)
    state_ref, carry_rkv_ref, carry_lora_ref = rwkv_scr[:3]

    @pl.when(c == 0)
    def _():
        state_ref[...] = jnp.zeros_like(state_ref)
        carry_rkv_ref[...] = jnp.zeros_like(carry_rkv_ref)
        carry_lora_ref[...] = jnp.zeros_like(carry_lora_ref)

    attn = _attn_stages(*attn_in, attn_out, *attn_scr, n_heads=n_heads, first=c == 0,
                        n=(c // n_pairs) % nb, hp=c % n_pairs)
    next(attn)

    def rwkv_chunks():
        for i in range(chunks_per_step):
            yield from _rwkv_stages(*rwkv_in, rw_out, *rwkv_scr, width=width, lo=i * CHUNK)

    rwkv = rwkv_chunks()
    running = [True, True]
    while any(running):
        for _ in range(RWKV_STAGES_PER_ATTN_STAGE):
            if running[0] and next(rwkv, "done") == "done":
                running[0] = False
        if running[1] and next(attn, "done") == "done":
            running[1] = False


def _mixers(qkv, band, rkv, lora, prm, *, batch, seq, n_heads):
    t, w3 = rkv.shape
    width = w3 // 3
    n_lora = lora.shape[1]
    ch = CHUNK
    n_pairs = width // LANES
    nb = seq // ATTN_ROWS
    n_pat = len(DILATED_PATTERNS)
    n_steps = batch * nb * n_pairs
    assert (seq // ch) % n_steps == 0, "RWKV chunks must divide evenly over the attention steps"
    chunks_per_step = (seq // ch) // n_steps
    rows_per_step = chunks_per_step * ch

    a_blk = lambda c: (c // n_pairs) // nb * nb + (c // n_pairs) % nb
    a_prev = lambda c: (c // n_pairs) // nb * nb + jnp.maximum((c // n_pairs) % nb - 1, 0)
    aspec = lambda f: pl.BlockSpec((ATTN_ROWS, LANES), f)
    attn_specs = [
        aspec(lambda c: (a_blk(c), c % n_pairs)),
        aspec(lambda c: (a_prev(c), n_pairs + c % n_pairs)),
        aspec(lambda c: (a_blk(c), n_pairs + c % n_pairs)),
        aspec(lambda c: (a_prev(c), 2 * n_pairs + c % n_pairs)),
        aspec(lambda c: (a_blk(c), 2 * n_pairs + c % n_pairs)),
        pl.BlockSpec(band.shape, lambda c: (0, 0)),
    ]

    row = lambda n: pl.BlockSpec((1, n), lambda c: (0, 0))
    full = lambda a: pl.BlockSpec(a.shape, lambda c: (0,) * a.ndim)
    head_id = jnp.arange(SEG_LANES) // HEAD_DIM
    seg = (head_id[:, None] == head_id[None, :]).astype(BF16)
    pos = jnp.arange(batch * ch)
    tri = ((pos[:, None] >= pos[None, :]) & (pos[:, None] // ch == pos[None, :] // ch)).astype(BF16)
    n_units = batch * n_pairs
    big = lambda: pltpu.VMEM((n_units, ch, LANES), F32)
    rwkv_args = (rkv.reshape(batch, seq, w3), lora.reshape(batch, seq, n_lora), prm["mix_rkv"],
                 prm["mix_lora"], prm["w0"], prm["w_up"], prm["a0"], prm["a_up"], prm["g_up"], prm["k_k"],
                 prm["k_a"], prm["r_k"], prm["ln_w"], prm["ln_b"], seg, tri)
    rwkv_specs = [
        pl.BlockSpec((batch, rows_per_step, w3), lambda c: (0, c, 0)),
        pl.BlockSpec((batch, rows_per_step, n_lora), lambda c: (0, c, 0)),
        row(w3), row(n_lora), row(width), full(prm["w_up"]), row(width), full(prm["a_up"]),
        full(prm["g_up"]), row(width), row(width), row(width), row(width), row(width),
        full(seg), full(tri),
    ]
    assert len(attn_specs) == N_ATTN_IN and len(rwkv_specs) == N_RWKV_IN
    attn, rw = pl.pallas_call(
        functools.partial(_mixers_kernel, n_heads=n_heads, width=width, nb=nb, n_pairs=n_pairs,
                          chunks_per_step=chunks_per_step),
        grid=(n_steps,),
        in_specs=attn_specs + rwkv_specs,
        out_specs=[aspec(lambda c: (a_blk(c), c % n_pairs)),
                   pl.BlockSpec((batch, rows_per_step, width), lambda c: (0, c, 0))],
        out_shape=[jax.ShapeDtypeStruct((t, width), BF16),
                   jax.ShapeDtypeStruct((batch, seq, width), BF16)],
        scratch_shapes=[
            pltpu.VMEM((n_pat * n_heads, ATTN_BLK, 2 * ATTN_BLK), F32),
            pltpu.VMEM((n_pat, ATTN_ROWS, LANES), F32),
            pltpu.VMEM((n_pat, ATTN_ROWS, LANES), F32),
            pltpu.VMEM((n_units, LANES, LANES), F32),
            pltpu.VMEM((batch, 1, w3), F32),
            pltpu.VMEM((batch, 1, n_lora), F32),
            big(), big(), big(), big(), big(), big(), big(),
            pltpu.VMEM((n_units, 1, LANES), F32),
            big(),
        ],
        compiler_params=_params(("arbitrary",)),
        name="mixers",
    )(qkv, qkv, qkv, qkv, qkv, band, *rwkv_args)
    return attn, rw.reshape(t, width)


def _out_proj_kernel(attn_ref, rw_ref, wa_ref, wr_ref, x_ref, out_ref):
    out_ref[...] = (x_ref[...] + jnp.dot(attn_ref[...], wa_ref[...], preferred_element_type=F32)
                    + jnp.dot(rw_ref[...], wr_ref[...], preferred_element_type=F32))


def _out_proj(attn, rw, w_out, x2, *, tm, tn):
    t, width = attn.shape
    d = w_out.shape[1]
    return pl.pallas_call(
        _out_proj_kernel,
        grid=(t // tm, d // tn),
        in_specs=[
            pl.BlockSpec((tm, width), lambda i, j: (i, 0)),
            pl.BlockSpec((tm, width), lambda i, j: (i, 0)),
            pl.BlockSpec((width, tn), lambda i, j: (0, j)),
            pl.BlockSpec((width, tn), lambda i, j: (1, j)),
            pl.BlockSpec((tm, tn), lambda i, j: (i, j)),
        ],
        out_specs=pl.BlockSpec((tm, tn), lambda i, j: (i, j)),
        out_shape=jax.ShapeDtypeStruct((t, d), F32),
        compiler_params=_params(("arbitrary", "arbitrary")),
        name="out_proj",
    )(attn, rw, w_out, w_out, x2)


FFN_DOWN_COLS = 512


def _ffn_kernel(h_hbm, g_ref, wg_ref, wu_ref, wd_ref, og_ref, out_ref, hn_ref, hbuf_ref, sem_ref):
    i, j = pl.program_id(0), pl.program_id(1)
    tm = hbuf_ref.shape[0]

    def h_copy(tile):
        rows = pl.ds(pl.multiple_of(tile * tm, tm), tm)
        return pltpu.make_async_copy(h_hbm.at[rows, :], hbuf_ref, sem_ref)

    @pl.when((i == 0) & (j == 0))
    def _():
        h_copy(0).start()

    @pl.when((j == 1) & (i + 1 < pl.num_programs(0)))
    def _():
        h_copy(i + 1).start()

    @pl.when(j == 0)
    def _():
        h_copy(i).wait()
        h = hbuf_ref[...]
        ms = jnp.mean(h * h, axis=-1, keepdims=True)
        hn_ref[...] = (h * lax.rsqrt(ms + RMS_EPS) * g_ref[...]).astype(BF16)
        out_ref[...] = h

    hn = hn_ref[...]
    gate = jnp.dot(hn, wg_ref[...], preferred_element_type=F32)
    up = jnp.dot(hn, wu_ref[...], preferred_element_type=F32)
    act = (gate * jax.nn.sigmoid(gate) * up).astype(BF16)
    for c0 in range(0, out_ref.shape[1], FFN_DOWN_COLS):
        cols = slice(c0, c0 + FFN_DOWN_COLS)
        out_ref[:, cols] += jnp.dot(act, wd_ref[:, cols], preferred_element_type=F32)

    @pl.when(j == pl.num_programs(1) - 1)
    def _():
        h = out_ref[...]
        ms = jnp.mean(h * h, axis=-1, keepdims=True)
        out_ref[...] = h * lax.rsqrt(ms + RMS_EPS) * og_ref[...]


def _ffn(h, g, w_gate_up, w_down, out_g, *, tm, th):
    t, d = h.shape
    hidden = w_down.shape[0]
    n_h = hidden // th
    return pl.pallas_call(
        _ffn_kernel,
        grid=(t // tm, n_h),
        in_specs=[
            pl.BlockSpec(memory_space=pl.ANY),
            pl.BlockSpec((1, d), lambda i, j: (0, 0)),
            pl.BlockSpec((d, th), lambda i, j: (0, j)),
            pl.BlockSpec((d, th), lambda i, j: (0, n_h + j)),
            pl.BlockSpec((th, d), lambda i, j: (j, 0)),
            pl.BlockSpec((1, d), lambda i, j: (0, 0)),
        ],
        out_specs=pl.BlockSpec((tm, d), lambda i, j: (i, 0)),
        out_shape=jax.ShapeDtypeStruct((t, d), F32),
        scratch_shapes=[pltpu.VMEM((tm, d), BF16), pltpu.VMEM((tm, d), F32), pltpu.SemaphoreType.DMA(())],
        compiler_params=pltpu.CompilerParams(dimension_semantics=("arbitrary", "arbitrary"),
                                             vmem_limit_bytes=FFN_VMEM_LIMIT),
        name="ffn",
    )(h, g, w_gate_up, w_gate_up, w_down, out_g)


def _pad_rows(a, rows):
    return jnp.pad(a, ((0, rows - a.shape[0]), (0, 0)))


def _layer(h, norm1_g, w_in, shift_mix, w0, w_up, a0, a_up, g_up, k_k, k_a, r_k, ln_w, ln_b,
           rel_bias_table, w_out, norm2_g, w_gate_up, w_down, out_g, *, batch, seq, tile_m):
    d = h.shape[1]
    width = w_out.shape[0] // 2
    n_heads = width // HEAD_DIM
    n_main = 6 * width
    rank_w, rank_a, rank_g = w_up.shape[0], a_up.shape[0], g_up.shape[0]
    n_lora = rank_w + rank_a + rank_g
    n_lora_pad = -(-n_lora // LANES) * LANES

    qkv, rkv, lora = _in_proj(h, norm1_g.reshape(1, d), w_in.astype(BF16), n_main=n_main,
                              n_lora_pad=n_lora_pad, tm=tile_m, tn=width)

    mix_rkv = shift_mix[:3 * width].reshape(1, 3 * width)
    mix_lora = jnp.pad(shift_mix[3 * width:], (0, n_lora_pad - n_lora)).reshape(1, n_lora_pad)
    zeros = lambda n: jnp.zeros((n, width), F32)
    prm = dict(
        mix_rkv=mix_rkv, mix_lora=mix_lora,
        w0=w0.reshape(1, width), a0=a0.reshape(1, width),
        w_up=_pad_rows(w_up, n_lora_pad).astype(BF16),
        a_up=_pad_rows(jnp.concatenate([zeros(rank_w), a_up]), n_lora_pad).astype(BF16),
        g_up=_pad_rows(jnp.concatenate([zeros(rank_w + rank_a), g_up]), n_lora_pad).astype(BF16),
        k_k=k_k.reshape(1, width), k_a=k_a.reshape(1, width), r_k=r_k.reshape(1, width),
        ln_w=ln_w.reshape(1, width), ln_b=ln_b.reshape(1, width),
    )
    attn, rw = _mixers(qkv, _band_rows(rel_bias_table), rkv, lora, prm, batch=batch, seq=seq, n_heads=n_heads)

    h1 = _out_proj(attn, rw, w_out.astype(BF16), h, tm=tile_m // 2, tn=d)
    return _ffn(h1, norm2_g.reshape(1, d), w_gate_up.astype(BF16), w_down.astype(BF16), out_g.reshape(1, d),
                tm=tile_m, th=512)


def kernel(x, norm1_g, w_in, rwkv_shift_mix, rwkv_w0, rwkv_w_up, rwkv_a0, rwkv_a_up, rwkv_g_up, rwkv_k_k, rwkv_k_a, rwkv_r_k, rwkv_ln_w, rwkv_ln_b, rel_bias_table, w_out, norm2_g, w_gate_up, w_down, final_g):
    batch, seq, d = x.shape
    depth = w_in.shape[0]
    assert depth == 1, "the final RMSNorm is fused into the last layer's FFN kernel"
    assert seq % DILATED_PATTERNS[-1][0] == 0 and seq % CHUNK == 0
    h = x.reshape(batch * seq, d)
    out = _layer(h, norm1_g[0], w_in[0], rwkv_shift_mix[0], rwkv_w0[0], rwkv_w_up[0], rwkv_a0[0],
                 rwkv_a_up[0], rwkv_g_up[0], rwkv_k_k[0], rwkv_k_a[0], rwkv_r_k[0], rwkv_ln_w[0],
                 rwkv_ln_b[0], rel_bias_table, w_out[0], norm2_g[0], w_gate_up[0], w_down[0], final_g,
                 batch=batch, seq=seq, tile_m=min(1024, batch * seq))
    return out.reshape(batch, seq, d)
```
